```python
import math
import jax
import jax.numpy as jnp
from jax import lax
import numpy as np

D_MODEL = 4096
BATCH = 16
SEQ = 256
DEPTH = 2
DEC_BATCH = 4
DEC_SEQ = 2048
PAST_LEN = 256

GRID_W = 64
N_AB_LAYERS = (DEPTH + 1) // 2
N_C_LAYERS = DEPTH // 2
D_A = D_MODEL // 2
D_B = D_MODEL // 2
HGRN_DK = 128
HGRN_HEADS = D_A // HGRN_DK
HGRN_DV = D_A // HGRN_HEADS
HGRN_CHUNK = 32
S5_CH = 16
S5_GROUPS = D_B // S5_CH
S5_P = 64
S5_DT_MIN = 0.001
S5_DT_MAX = 0.1
FNET_GROUPS = 8
FNET_GROUP_DIM = D_MODEL // FNET_GROUPS
MOE_GROUPS = 4
MOE_PER_GROUP = 8
MOE_EXPERTS = MOE_GROUPS * MOE_PER_GROUP
MOE_TOPK = 2
D_EXPERT = D_MODEL // 8
AB_IN = 5 * D_A + D_B
EPS = 1e-6

kernel_name = 'hgrn2_s5_fnet_hmoe_diffusion_step'


def rmsnorm(x, g):
    xf = x.astype(jnp.float32)
    y = xf * lax.rsqrt(jnp.mean(xf * xf, axis=-1, keepdims=True) + EPS)
    return y.astype(x.dtype) * g


def modulate(x, shift, scale):
    return x * (1 + scale) + shift


def hgrn2_chunk_scan(q, k, v, logf, s0):
    b_, l_, h_, dk = q.shape
    dv = v.shape[-1]
    nc = l_ // HGRN_CHUNK

    def to_chunks(t):
        return t.reshape(b_, nc, HGRN_CHUNK, h_, t.shape[-1]).transpose(1, 0, 3, 2, 4)

    qc, kc, vc, gc = (to_chunks(t) for t in (q, k, v, logf))
    mask = jnp.tril(jnp.ones((HGRN_CHUNK, HGRN_CHUNK), dtype=bool))[:, :, None]

    def step(s, inp):
        qb, kb, vb, gb = inp
        bcum = jnp.cumsum(gb, axis=2)
        o_inter = jnp.einsum('bhck,bhkv->bhcv', qb * jnp.exp(bcum), s)
        diff = bcum[:, :, :, None, :] - bcum[:, :, None, :, :]
        decay = jnp.where(mask, jnp.exp(jnp.where(mask, diff, 0.0)), 0.0)
        att = jnp.einsum('bhtk,bhsk,bhtsk->bhts', qb, kb, decay)
        o_intra = jnp.einsum('bhts,bhsv->bhtv', att, vb)
        b_last = bcum[:, :, -1:, :]
        s_new = jnp.exp(b_last[:, :, 0, :])[..., None] * s + jnp.einsum('bhck,bhcv->bhkv', kb * jnp.exp(b_last - bcum), vb)
        return s_new, o_inter + o_intra

    s_final, o = lax.scan(step, s0, (qc, kc, vc, gc))
    o = o.transpose(1, 0, 3, 2, 4).reshape(b_, l_, h_, dv)
    return o, s_final


def hgrn2_direction(q, fz, v, lb, s0, reverse):
    b_, l_, _ = fz.shape
    f = lb + (1.0 - lb) * jax.nn.sigmoid(fz.astype(jnp.float32))
    k = (1.0 - f).reshape(b_, l_, HGRN_HEADS, HGRN_DK)
    logf = jnp.log(f).reshape(b_, l_, HGRN_HEADS, HGRN_DK)
    if reverse:
        q, k, v, logf = (jnp.flip(t, axis=1) for t in (q, k, v, logf))
    o, s_final = hgrn2_chunk_scan(q, k, v, logf, s0)
    if reverse:
        o = jnp.flip(o, axis=1)
    return o, s_final


def _linear_combine(e1, e2):
    a1, b1 = e1
    a2, b2 = e2
    return (a2 * a1, a2 * b1 + b2)


def s5_direction(u, lam_re, lam_im, b_re, b_im, c_re, c_im, log_step, x0, reverse):
    f32 = jnp.float32
    lam = lax.complex(jnp.minimum(lam_re.astype(f32), -1e-4), lam_im.astype(f32))
    dt = jnp.exp(log_step.astype(f32))[:, None]
    lam_bar = jnp.exp(lam * dt)
    b_bar = ((lam_bar - 1.0) / lam)[..., None] * lax.complex(b_re.astype(f32), b_im.astype(f32))
    if reverse:
        u = jnp.flip(u, axis=1)
    bu = jnp.einsum('gpc,blgc->blgp', b_bar, u.astype(jnp.complex64))
    bu = bu.at[:, 0].add(lam_bar * x0)
    a = jnp.broadcast_to(lam_bar, bu.shape)
    _, xs = lax.associative_scan(_linear_combine, (a, bu), axis=1)
    y = jnp.einsum('gcp,blgp->blgc', lax.complex(c_re.astype(f32), c_im.astype(f32)), xs).real
    if reverse:
        y = jnp.flip(y, axis=1)
    return y, xs[:, -1]


def mixer_ab(xn, w_in, w_out, lb, gnorm, lam_re, lam_im, b_re, b_im, c_re, c_im, log_step,
             s5_d, glu_w, glu_b, h0, x0):
    f32 = jnp.float32
    b_, l_, _ = xn.shape
    proj = xn @ w_in
    q, f_fw, f_bw, v, g, u = jnp.split(proj, [D_A, 2 * D_A, 3 * D_A, 4 * D_A, 5 * D_A], axis=-1)
    q = jax.nn.silu(q.astype(f32)).reshape(b_, l_, HGRN_HEADS, HGRN_DK)
    v = v.astype(f32).reshape(b_, l_, HGRN_HEADS, HGRN_DV)
    o_fw, h_fw = hgrn2_direction(q, f_fw, v, lb[0], h0[:, 0], False)
    o_bw, h_bw = hgrn2_direction(q, f_bw, v, lb[1], h0[:, 1], True)
    o = o_fw + o_bw
    o = o * lax.rsqrt(jnp.mean(o * o, axis=-1, keepdims=True) + EPS) * gnorm.astype(f32).reshape(HGRN_HEADS, HGRN_DV)
    o_a = o.reshape(b_, l_, D_A) * jax.nn.silu(g.astype(f32))
    u = u.astype(f32)
    ug = u.reshape(b_, l_, S5_GROUPS, S5_CH)
    y_fw, x_fw = s5_direction(ug, lam_re[0], lam_im[0], b_re[0], b_im[0], c_re[0], c_im[0], log_step[0], x0[:, 0], False)
    y_bw, x_bw = s5_direction(ug, lam_re[1], lam_im[1], b_re[1], b_im[1], c_re[1], c_im[1], log_step[1], x0[:, 1], True)
    y = (y_fw + y_bw).reshape(b_, l_, D_B) + s5_d.astype(f32) * u
    y = jax.nn.gelu(y)
    y = y * jax.nn.sigmoid(y @ glu_w.astype(f32) + glu_b.astype(f32))
    out = jnp.concatenate([o_a, y], axis=-1).astype(xn.dtype) @ w_out
    return out, jnp.stack([h_fw, h_bw], axis=1), jnp.stack([x_fw, x_bw], axis=1)


def fourier_mix_context(xn):
    b_, l_, d_ = xn.shape
    z = xn.astype(jnp.float32).reshape(b_, l_, FNET_GROUPS, FNET_GROUP_DIM)
    return jnp.fft.fftn(z, axes=(1, 3), norm='ortho').real.reshape(b_, l_, d_).astype(xn.dtype)


def fourier_mix_latent(xn):
    b_, l_, d_ = xn.shape
    rows = l_ // GRID_W
    z = xn.astype(jnp.float32).reshape(b_, rows, GRID_W, FNET_GROUPS, FNET_GROUP_DIM)
    return jnp.fft.fftn(z, axes=(1, 2, 4), norm='ortho').real.reshape(b_, l_, d_).astype(xn.dtype)


def hier_moe(x, wr_g, br_g, wr_e, br_e, w_gate, w_up, w_down):
    f32 = jnp.float32
    t = x.shape[0]
    p_group = jax.nn.softmax((x @ wr_g + br_g).astype(f32), axis=-1)
    g_prob, g_idx = lax.top_k(p_group, 1)
    e_logits = (x @ wr_e + br_e).astype(f32).reshape(t, MOE_GROUPS, MOE_PER_GROUP)
    e_sel = jnp.take_along_axis(e_logits, g_idx[:, :, None], axis=1)[:, 0]
    e_top, e_idx = lax.top_k(e_sel, MOE_TOPK)
    e_w = jax.nn.softmax(e_top, axis=-1) * g_prob
    within = jnp.sum(jax.nn.one_hot(e_idx, MOE_PER_GROUP, dtype=f32) * e_w[..., None], axis=1)
    gate = (jax.nn.one_hot(g_idx[:, 0], MOE_GROUPS, dtype=f32)[:, :, None] * within[:, None, :]).reshape(t, MOE_EXPERTS)
    h = jax.nn.silu(jnp.einsum('td,edf->tef', x, w_gate)) * jnp.einsum('td,edf->tef', x, w_up)
    return jnp.einsum('tef,efd->td', h * gate[..., None].astype(h.dtype), w_down)


def setup_inputs(seed: int = 0) -> dict:
    key = jax.random.key(seed)
    ks = iter(jax.random.split(key, 48))

    def nrm(shape, scale):
        return scale * jax.random.normal(next(ks), shape, jnp.float32)

    n_idx = jnp.arange(S5_P, dtype=jnp.float32)
    return {
        'x_prompt': nrm((BATCH, SEQ, D_MODEL), 1.0),
        'x_sample': nrm((DEC_BATCH, DEC_SEQ, D_MODEL), 1.0),
        'state_hgrn': nrm((DEC_BATCH, N_AB_LAYERS, 2, HGRN_HEADS, HGRN_DK, HGRN_DV), 0.5),
        'state_s5_re': nrm((DEC_BATCH, N_AB_LAYERS, 2, S5_GROUPS, S5_P), 0.1),
        'state_s5_im': nrm((DEC_BATCH, N_AB_LAYERS, 2, S5_GROUPS, S5_P), 0.1),
        'c': nrm((DEC_BATCH, D_MODEL), 1.0),
        'c_ctx': nrm((D_MODEL,), 1.0),
        'ada_w': nrm((DEPTH, D_MODEL, 6 * D_MODEL), 0.5 * D_MODEL ** -0.5),
        'ada_b': nrm((DEPTH, 6 * D_MODEL), 0.02),
        'norm_mix': 1.0 + nrm((DEPTH, D_MODEL), 0.02),
        'norm_ffn': 1.0 + nrm((DEPTH, D_MODEL), 0.02),
        'norm_final': 1.0 + nrm((D_MODEL,), 0.02),
        'ab_w_in': nrm((N_AB_LAYERS, D_MODEL, AB_IN), D_MODEL ** -0.5),
        'ab_w_out': nrm((N_AB_LAYERS, D_A + D_B, D_MODEL), (D_A + D_B) ** -0.5),
        'hgrn_lb_theta': nrm((2, N_AB_LAYERS + 1, D_A), 0.5),
        'hgrn_gnorm': 1.0 + nrm((N_AB_LAYERS, D_A), 0.02),
        's5_lam_re': -0.5 + nrm((N_AB_LAYERS, 2, S5_GROUPS, S5_P), 0.01),
        's5_lam_im': math.pi * n_idx + nrm((N_AB_LAYERS, 2, S5_GROUPS, S5_P), 0.01),
        's5_b_re': nrm((N_AB_LAYERS, 2, S5_GROUPS, S5_P, S5_CH), (2 * S5_CH) ** -0.5),
        's5_b_im': nrm((N_AB_LAYERS, 2, S5_GROUPS, S5_P, S5_CH), (2 * S5_CH) ** -0.5),
        's5_c_re': nrm((N_AB_LAYERS, 2, S5_GROUPS, S5_CH, S5_P), (2 * S5_P) ** -0.5),
        's5_c_im': nrm((N_AB_LAYERS, 2, S5_GROUPS, S5_CH, S5_P), (2 * S5_P) ** -0.5),
        's5_log_step': jax.random.uniform(next(ks), (N_AB_LAYERS, 2, S5_GROUPS), jnp.float32,
                                          math.log(S5_DT_MIN), math.log(S5_DT_MAX)),
        's5_d': nrm((N_AB_LAYERS, D_B), 1.0),
        's5_glu_w': nrm((N_AB_LAYERS, D_B, D_B), D_B ** -0.5),
        's5_glu_b': nrm((N_AB_LAYERS, D_B), 0.02),
        'fnet_w_out': nrm((N_C_LAYERS, D_MODEL, D_MODEL), D_MODEL ** -0.5),
        'moe_wr_group': nrm((DEPTH, D_MODEL, MOE_GROUPS), D_MODEL ** -0.5),
        'moe_br_group': nrm((DEPTH, MOE_GROUPS), 0.01),
        'moe_wr_expert': nrm((DEPTH, D_MODEL, MOE_EXPERTS), D_MODEL ** -0.5),
        'moe_br_expert': nrm((DEPTH, MOE_EXPERTS), 0.01),
        'moe_w_gate': nrm((DEPTH, MOE_EXPERTS, D_MODEL, D_EXPERT), D_MODEL ** -0.5),
        'moe_w_up': nrm((DEPTH, MOE_EXPERTS, D_MODEL, D_EXPERT), D_MODEL ** -0.5),
        'moe_w_down': nrm((DEPTH, MOE_EXPERTS, D_EXPERT, D_MODEL), D_EXPERT ** -0.5),
    }


def reference(x_prompt, x_sample, state_hgrn, state_s5_re, state_s5_im, c, c_ctx,
              ada_w, ada_b, norm_mix, norm_ffn, norm_final,
              ab_w_in, ab_w_out, hgrn_lb_theta, hgrn_gnorm,
              s5_lam_re, s5_lam_im, s5_b_re, s5_b_im, s5_c_re, s5_c_im, s5_log_step,
              s5_d, s5_glu_w, s5_glu_b, fnet_w_out,
              moe_wr_group, moe_br_group, moe_wr_expert, moe_br_expert,
              moe_w_gate, moe_w_up, moe_w_down):
    f32 = jnp.float32
    xp, xs = x_prompt, x_sample
    bp = xp.shape[0]
    lb_all = jnp.cumsum(jax.nn.softmax(hgrn_lb_theta.astype(f32), axis=1), axis=1)
    h0_ctx = jnp.zeros((bp, 2, HGRN_HEADS, HGRN_DK, HGRN_DV), f32)
    x0_ctx = jnp.zeros((bp, 2, S5_GROUPS, S5_P), jnp.complex64)
    new_h, new_re, new_im = [], [], []
    i_ab = 0
    i_c = 0
    for l in range(DEPTH):
        mod_p = jnp.split(jax.nn.silu(c_ctx) @ ada_w[l] + ada_b[l], 6, axis=-1)
        mod_s = [m[:, None, :] for m in jnp.split(jax.nn.silu(c) @ ada_w[l] + ada_b[l], 6, axis=-1)]
        xn_p = modulate(rmsnorm(xp, norm_mix[l]), mod_p[0], mod_p[1])
        xn_s = modulate(rmsnorm(xs, norm_mix[l]), mod_s[0], mod_s[1])
        if l % 2 == 0:
            ab_params = (ab_w_in[i_ab], ab_w_out[i_ab], lb_all[:, i_ab], hgrn_gnorm[i_ab],
                         s5_lam_re[i_ab], s5_lam_im[i_ab], s5_b_re[i_ab], s5_b_im[i_ab],
                         s5_c_re[i_ab], s5_c_im[i_ab], s5_log_step[i_ab],
                         s5_d[i_ab], s5_glu_w[i_ab], s5_glu_b[i_ab])
            out_p, h_fin, x_fin = mixer_ab(xn_p, *ab_params, h0_ctx, x0_ctx)
            h0_s = state_hgrn[:, i_ab].astype(f32)
            x0_s = lax.complex(state_s5_re[:, i_ab].astype(f32), state_s5_im[:, i_ab].astype(f32))
            out_s, _, _ = mixer_ab(xn_s, *ab_params, h0_s, x0_s)
            new_h.append(h_fin)
            new_re.append(x_fin.real)
            new_im.append(x_fin.imag)
            i_ab += 1
        else:
            out_p = fourier_mix_context(xn_p) @ fnet_w_out[i_c]
            out_s = fourier_mix_latent(xn_s) @ fnet_w_out[i_c]
            i_c += 1
        xp = xp + (mod_p[2] * out_p).astype(xp.dtype)
        xs = xs + (mod_s[2] * out_s).astype(xs.dtype)
        moe_params = (moe_wr_group[l], moe_br_group[l], moe_wr_expert[l], moe_br_expert[l],
                      moe_w_gate[l], moe_w_up[l], moe_w_down[l])
        xn_p = modulate(rmsnorm(xp, norm_ffn[l]), mod_p[3], mod_p[4])
        xn_s = modulate(rmsnorm(xs, norm_ffn[l]), mod_s[3], mod_s[4])
        ff_p = hier_moe(xn_p.reshape(-1, D_MODEL), *moe_params).reshape(xp.shape)
        ff_s = hier_moe(xn_s.reshape(-1, D_MODEL), *moe_params).reshape(xs.shape)
        xp = xp + (mod_p[5] * ff_p).astype(xp.dtype)
        xs = xs + (mod_s[5] * ff_s).astype(xs.dtype)
    y_prompt = rmsnorm(xp, norm_final)
    y_sample = rmsnorm(xs, norm_final)
    new_state_hgrn = jnp.stack(new_h, axis=1)
    new_state_s5_re = jnp.stack(new_re, axis=1)
    new_state_s5_im = jnp.stack(new_im, axis=1)
    return (y_prompt, y_sample, new_state_hgrn, new_state_s5_re, new_state_s5_im)
```

```python
import functools
import math

import jax
import jax.numpy as jnp
from jax import lax
from jax.experimental import pallas as pl
from jax.experimental.pallas import tpu as pltpu

F32 = jnp.float32
BF16 = jnp.bfloat16
EPS = 1e-6
NEG = -1e30
V7X_VMEM_LIMIT_BYTES = 56 * 1024 * 1024
FNET_GROUPS = 8
GRID_W = 64
HGRN_CHUNK = 128
S5_CHUNK = 16


def _cparams(*sem):
    return pltpu.CompilerParams(dimension_semantics=sem, vmem_limit_bytes=V7X_VMEM_LIMIT_BYTES)


def _tile(n, pref, mult=8):
    if n <= pref:
        return n
    for t in range(pref - pref % mult, 0, -mult):
        if n % t == 0:
            return t
    raise ValueError(f"no tile for {n}")


def _dot(a, b):
    return jnp.dot(a, b, preferred_element_type=F32)


def _dot_nt(a, b):
    return lax.dot_general(a, b, (((1,), (1,)), ((), ())), preferred_element_type=F32)


def _dot_tn(a, b):
    return lax.dot_general(a, b, (((0,), (0,)), ((), ())), preferred_element_type=F32)


def _silu(x):
    return x * jax.nn.sigmoid(x)


def _ada_kernel(c_ref, w_ref, b_ref, o_ref):
    a = _silu(c_ref[...]).astype(BF16)
    o_ref[0] = _dot(a, w_ref[0].astype(BF16)) + b_ref[0]


def _ada(c8, ada_w, ada_b):
    depth, d, n = ada_w.shape
    tn = _tile(n, 1024, 128)
    return pl.pallas_call(
        _ada_kernel,
        grid=(depth, n // tn),
        in_specs=[
            pl.BlockSpec((8, d), lambda l, j: (0, 0)),
            pl.BlockSpec((1, d, tn), lambda l, j: (l, 0, j)),
            pl.BlockSpec((1, 1, tn), lambda l, j: (l, 0, j)),
        ],
        out_specs=pl.BlockSpec((1, 8, tn), lambda l, j: (l, 0, j)),
        out_shape=jax.ShapeDtypeStruct((depth, 8, n), F32),
        compiler_params=_cparams("parallel", "parallel"),
        name="ada",
    )(c8, ada_w, ada_b.reshape(depth, 1, n))


def _mod_row(i, tm, tp, ls):
    r0 = i * tm
    return jnp.where(r0 < tp, 0, 1 + (r0 - tp) // ls)


def _norm_mod_kernel(x_ref, g_ref, sh_ref, sc_ref, o_ref):
    x = x_ref[...]
    y = x * lax.rsqrt(jnp.mean(x * x, axis=-1, keepdims=True) + EPS) * g_ref[...]
    o_ref[...] = (y * (1.0 + sc_ref[0]) + sh_ref[0]).astype(o_ref.dtype)


def _norm_mod(x, g, shift, scale, tp, ls, out_dtype):
    t, d = x.shape
    tm = _tile(math.gcd(tp, ls), 256)
    mod_spec = pl.BlockSpec((1, 1, d), lambda i: (_mod_row(i, tm, tp, ls), 0, 0))
    return pl.pallas_call(
        _norm_mod_kernel,
        grid=(t // tm,),
        in_specs=[pl.BlockSpec((tm, d), lambda i: (i, 0)), pl.BlockSpec((1, d), lambda i: (0, 0)), mod_spec, mod_spec],
        out_specs=pl.BlockSpec((tm, d), lambda i: (i, 0)),
        out_shape=jax.ShapeDtypeStruct((t, d), out_dtype),
        compiler_params=_cparams("parallel"),
        name="norm_mod",
    )(x, g.reshape(1, d), shift, scale)


def _rmsnorm_kernel(x_ref, g_ref, o_ref):
    x = x_ref[...]
    o_ref[...] = x * lax.rsqrt(jnp.mean(x * x, axis=-1, keepdims=True) + EPS) * g_ref[...]


def _rmsnorm(x, g):
    t, d = x.shape
    tm = _tile(t, 256)
    return pl.pallas_call(
        _rmsnorm_kernel,
        grid=(t // tm,),
        in_specs=[pl.BlockSpec((tm, d), lambda i: (i, 0)), pl.BlockSpec((1, d), lambda i: (0, 0))],
        out_specs=pl.BlockSpec((tm, d), lambda i: (i, 0)),
        out_shape=jax.ShapeDtypeStruct((t, d), F32),
        compiler_params=_cparams("parallel"),
        name="final_norm",
    )(x, g.reshape(1, d))


def _mm_kernel(*refs, n_pairs, epilogue):
    o_ref = refs[-1]
    acc = None
    for a_ref, b_ref in zip(refs[:n_pairs], refs[n_pairs : 2 * n_pairs]):
        part = _dot(a_ref[...].astype(BF16), b_ref[...])
        acc = part if acc is None else acc + part
    o_ref[...] = epilogue(acc, *refs[2 * n_pairs : -1]).astype(o_ref.dtype)


def _plain(acc):
    return acc


def _residual(acc, x_ref, g_ref):
    return x_ref[...] + g_ref[0] * acc


def _glu(acc, y_ref, b_ref):
    return y_ref[...] * jax.nn.sigmoid(acc + b_ref[...])


def _matmul(pairs, n, out_dtype, tm, tn, epilogue=_plain, extra=(), extra_specs=(), name="matmul"):
    t = pairs[0][0].shape[0]
    a_specs = [pl.BlockSpec((tm, a.shape[1]), lambda i, j: (i, 0)) for a, _ in pairs]
    b_specs = [pl.BlockSpec((b.shape[0], tn), lambda i, j: (0, j)) for _, b in pairs]
    return pl.pallas_call(
        functools.partial(_mm_kernel, n_pairs=len(pairs), epilogue=epilogue),
        grid=(t // tm, n // tn),
        in_specs=a_specs + b_specs + list(extra_specs),
        out_specs=pl.BlockSpec((tm, tn), lambda i, j: (i, j)),
        out_shape=jax.ShapeDtypeStruct((t, n), out_dtype),
        compiler_params=_cparams("parallel", "parallel"),
        name=name,
    )(*[a for a, _ in pairs], *[b for _, b in pairs], *extra)


def _matmul_residual(pairs, x, gate, tp, ls, name):
    t, d = x.shape
    tm = _tile(math.gcd(tp, ls), 1024)
    tn = _tile(d, 1024, 128)
    specs = [
        pl.BlockSpec((tm, tn), lambda i, j: (i, j)),
        pl.BlockSpec((1, 1, tn), lambda i, j: (_mod_row(i, tm, tp, ls), 0, j)),
    ]
    return _matmul(pairs, d, F32, tm, tn, _residual, (x, gate), specs, name)


def _split3(x):
    hi = x.astype(BF16)
    r1 = x - hi.astype(F32)
    mid = r1.astype(BF16)
    lo = (r1 - mid.astype(F32)).astype(BF16)
    return hi, mid, lo


def _hgrn_kernel(q_ref, f_ref, v_ref, lb_ref, s0_ref, o_ref, sfin_ref, st_ref, *, reverse, hb, c, dk):
    ci = pl.program_id(2)

    @pl.when(ci == 0)
    def _():
        for j in range(hb):
            st_ref[j] = s0_ref[0, j].T

    row = lax.broadcasted_iota(jnp.int32, (c, c), 0)
    col = lax.broadcasted_iota(jnp.int32, (c, c), 1)
    tri = jnp.where((col >= row) if reverse else (col <= row), 1.0, 0.0).astype(BF16)
    before = (col > row) if reverse else (col < row)
    diff = row ^ col
    trow = lax.broadcasted_iota(jnp.int32, (c, dk), 0)
    n_levels = int(math.log2(c))

    for j in range(hb):
        sl = slice(j * dk, (j + 1) * dk)
        q = _silu(q_ref[:, sl])
        lb = lb_ref[:, sl]
        f = lb + (1.0 - lb) * jax.nn.sigmoid(f_ref[:, sl])
        k = 1.0 - f
        v = v_ref[:, sl]
        vb = v.astype(BF16)
        hi, mid, lo = _split3(jnp.log(f))
        bcum = _dot(tri, hi) + _dot(tri, mid) + _dot(tri, lo)
        btot = bcum[0:1] if reverse else bcum[c - 1 : c]

        st = st_ref[j]
        o = _dot_nt((q * jnp.exp(bcum)).astype(BF16), st.astype(BF16))
        ke = (k * jnp.exp(btot - bcum)).astype(BF16)
        st_ref[j] = st * jnp.exp(btot) + _dot_tn(vb, ke)

        att = jnp.where(row == col, _dot_nt(q.astype(BF16), k.astype(BF16)), 0.0)
        f_prev = pltpu.roll(f, 1, 0)
        f_next = pltpu.roll(f, c - 1, 0)
        for h in range(n_levels):
            b = 1 << h
            later = ((trow & b) == 0) if reverse else ((trow & b) != 0)
            if b == 1:
                ex, ey = f, 1.0
            elif b == 2:
                o4 = trow & 3
                if reverse:
                    ex = jnp.where(o4 == 0, f * f_next, f)
                    ey = jnp.where(o4 == 3, f_prev, 1.0)
                else:
                    ex = jnp.where(o4 == 3, f * f_prev, f)
                    ey = jnp.where(o4 == 0, f_next, 1.0)
            else:
                b3 = bcum.reshape(c // (2 * b), 2 * b, dk)
                edge = b if reverse else b - 1
                ex = ey = jnp.exp(-jnp.abs(b3 - b3[:, edge : edge + 1, :])).reshape(c, dk)
            x_l = jnp.where(later, q * ex, 0.0).astype(BF16)
            y_l = jnp.where(later, 0.0, k * ey).astype(BF16)
            att = jnp.where(before & ((diff >> h) == 1), _dot_nt(x_l, y_l), att)
        o_ref[:, sl] = o + _dot(att.astype(BF16), vb)

    @pl.when(ci == pl.num_programs(2) - 1)
    def _():
        for j in range(hb):
            sfin_ref[0, j] = st_ref[j].T


def _hgrn(proj, lb, s0, row0, nb, l, d_a, f_section, reverse):
    h, dk, dv = s0.shape[1:]
    c = min(HGRN_CHUNK, l)
    nc = l // c
    hb = 4 if h % 4 == 0 else 1
    hw = hb * dk
    npb = d_a // hw

    def chunk(ci):
        return (nc - 1 - ci) if reverse else ci

    def in_spec(section):
        return pl.BlockSpec((c, hw), lambda b, g, ci: (row0 // c + b * nc + chunk(ci), section * npb + g))

    return pl.pallas_call(
        functools.partial(_hgrn_kernel, reverse=reverse, hb=hb, c=c, dk=dk),
        grid=(nb, npb, nc),
        in_specs=[
            in_spec(0),
            in_spec(f_section),
            in_spec(3),
            pl.BlockSpec((1, hw), lambda b, g, ci: (0, g)),
            pl.BlockSpec((1, hb, dk, dv), lambda b, g, ci: (b, g, 0, 0)),
        ],
        out_specs=[
            pl.BlockSpec((c, hw), lambda b, g, ci: (b * nc + chunk(ci), g)),
            pl.BlockSpec((1, hb, dk, dv), lambda b, g, ci: (b, g, 0, 0)),
        ],
        out_shape=[jax.ShapeDtypeStruct((nb * l, d_a), F32), jax.ShapeDtypeStruct((nb, h, dk, dv), F32)],
        scratch_shapes=[pltpu.VMEM((hb, dv, dk), F32)],
        compiler_params=_cparams("parallel", "parallel", "arbitrary"),
        name="hgrn_bwd" if reverse else "hgrn_fwd",
    )(proj, proj, proj, lb.reshape(1, d_a), s0)


def _hgrn_post_kernel(of_ref, ob_ref, g_ref, gn_ref, o_ref, *, nh, dv):
    for h in range(nh):
        sl = slice(h * dv, (h + 1) * dv)
        o = of_ref[:, sl] + ob_ref[:, sl]
        o = o * lax.rsqrt(jnp.mean(o * o, axis=-1, keepdims=True) + EPS) * gn_ref[:, sl]
        o_ref[:, sl] = (o * _silu(g_ref[:, sl])).astype(o_ref.dtype)


def _hgrn_post(o_fw, o_bw, proj, gnorm, nh, dv):
    t, d_a = o_fw.shape
    tm = _tile(t, 256)
    spec = pl.BlockSpec((tm, d_a), lambda i: (i, 0))
    return pl.pallas_call(
        functools.partial(_hgrn_post_kernel, nh=nh, dv=dv),
        grid=(t // tm,),
        in_specs=[spec, spec, pl.BlockSpec((tm, d_a), lambda i: (i, 4)), pl.BlockSpec((1, d_a), lambda i: (0, 0))],
        out_specs=spec,
        out_shape=jax.ShapeDtypeStruct((t, d_a), BF16),
        compiler_params=_cparams("parallel"),
        name="hgrn_post",
    )(o_fw, o_bw, proj, gnorm.reshape(1, d_a))


def _s5_tables(lam_re, lam_im, b_re, b_im, c_re, c_im, log_step, n_levels):
    hp = lax.Precision.HIGHEST
    tc = S5_CHUNK
    g, p, ch = b_re.shape[1:]
    lam = lax.complex(jnp.minimum(lam_re.astype(F32), -1e-4), lam_im.astype(F32))
    z = lam * jnp.exp(log_step.astype(F32))[..., None]
    b_bar = ((jnp.exp(z) - 1.0) / lam)[..., None] * lax.complex(b_re.astype(F32), b_im.astype(F32))
    cc = lax.complex(c_re.astype(F32), c_im.astype(F32))
    n = jnp.arange(tc + 1, dtype=F32)
    pw = jnp.exp(z[:, :, None, :] * n[None, None, :, None])

    kern = jnp.einsum("dgop,dglp,dgpi->dglio", cc, pw[:, :, :tc], b_bar, precision=hp).real
    s_idx = jnp.arange(tc)[:, None]
    t_idx = jnp.arange(tc)[None, :]
    k_fw = jnp.where((t_idx >= s_idx)[None, :, :, None, None], kern[0][:, jnp.maximum(t_idx - s_idx, 0)], 0.0)
    k_bw = jnp.where((s_idx >= t_idx)[None, :, :, None, None], kern[1][:, jnp.maximum(s_idx - t_idx, 0)], 0.0)
    toep = (k_fw + k_bw).transpose(0, 1, 3, 2, 4).reshape(g, tc * ch, tc * ch)

    def ri(zc, swap=False):
        parts = [zc.imag, zc.real] if swap else [zc.real, zc.imag]
        return jnp.concatenate(parts, axis=-1)

    inc_fw = pw[0][:, tc - 1 - jnp.arange(tc), :, None] * b_bar[0][:, None]
    inc_bw = pw[1][:, jnp.arange(tc), :, None] * b_bar[1][:, None]
    inc_fw = inc_fw.transpose(0, 1, 3, 2).reshape(g, tc * ch, p)
    inc_bw = inc_bw.transpose(0, 1, 3, 2).reshape(g, tc * ch, p)
    w1 = jnp.concatenate([ri(inc_fw), ri(inc_fw, True), ri(inc_bw), ri(inc_bw, True)], axis=-1)

    out_fw = cc[0][:, None] * pw[0][:, 1 + jnp.arange(tc), None, :]
    out_bw = cc[1][:, None] * pw[1][:, tc - jnp.arange(tc), None, :]

    def out_rows(w):
        w = w.reshape(g, tc * ch, p).transpose(0, 2, 1)
        return jnp.concatenate([w.real, -w.imag], axis=1)

    w2 = jnp.concatenate([toep, out_rows(out_fw), out_rows(out_bw)], axis=1)

    steps = (tc * (2 ** jnp.arange(n_levels))).astype(F32)
    lp = jnp.exp(z[:, :, None, :] * steps[None, None, :, None])
    a, b = lp.real, lp.imag
    lev = jnp.stack(
        [jnp.concatenate([a, a], -1), jnp.concatenate([-b, b], -1), jnp.concatenate([b, -b], -1)], axis=3
    )
    lev = lev.transpose(1, 0, 2, 3, 4).reshape(g, 2 * n_levels * 3, 2 * p)
    pad = (-lev.shape[1]) % 8
    lev = jnp.pad(lev, ((0, 0), (0, pad), (0, 0)))
    return w1.astype(BF16), w2.astype(BF16), lev


def _shift_rows(x, s, down):
    z = jnp.zeros((s, x.shape[1]), x.dtype)
    if down:
        return jnp.concatenate([z, x[: x.shape[0] - s]], axis=0)
    return jnp.concatenate([x[s:], z], axis=0)


def _s5_kernel(u_ref, w1_ref, w2_ref, lev_ref, h0_ref, y_ref, fin_ref, *, gb, nb, n_levels, max_levels, sw):
    m = u_ref.shape[1]
    for g in range(gb):
        u = u_ref[g]
        inc = _dot(u, w1_ref[g])
        carry = []
        for d in range(2):
            fwd = d == 0
            x = inc[:, (2 * d) * sw : (2 * d + 1) * sw]
            xs = inc[:, (2 * d + 1) * sw : (2 * d + 2) * sw]
            h0 = h0_ref[g, 2 * d]
            h0s = h0_ref[g, 2 * d + 1]

            def const(j, which, d=d, g=g):
                r = (d * max_levels + j) * 3 + which
                return lev_ref[g, r : r + 1, :]

            pad = jnp.zeros((m - nb, sw), F32)
            first = h0 * const(0, 0) + h0s * const(0, 1)
            first_s = h0s * const(0, 0) + h0 * const(0, 2)
            if fwd:
                x = x + jnp.concatenate([first, pad], axis=0)
                xs = xs + jnp.concatenate([first_s, pad], axis=0)
            else:
                x = x + jnp.concatenate([pad, first], axis=0)
                xs = xs + jnp.concatenate([pad, first_s], axis=0)
            for j in range(n_levels):
                sx = _shift_rows(x, nb << j, fwd)
                sxs = _shift_rows(xs, nb << j, fwd)
                x, xs = x + const(j, 0) * sx + const(j, 1) * sxs, xs + const(j, 0) * sxs + const(j, 2) * sx
            if fwd:
                carry.append(jnp.concatenate([h0, x[: m - nb]], axis=0))
                fin_ref[g, d] = x[m - nb :]
            else:
                carry.append(jnp.concatenate([x[nb:], h0], axis=0))
                fin_ref[g, d] = x[:nb]
        lhs = jnp.concatenate([u, carry[0].astype(BF16), carry[1].astype(BF16)], axis=1)
        y_ref[g] = _dot(lhs, w2_ref[g])


def _s5(u, w1, w2, lev, h0, nb, l, max_levels):
    g = w1.shape[0]
    tc = S5_CHUNK
    ch = w1.shape[1] // tc
    sw = w1.shape[2] // 4
    nch = l // tc
    m = nch * nb
    n_levels = int(math.log2(nch))
    u4 = u.astype(BF16).reshape(nb, nch, tc, g, ch).transpose(3, 1, 0, 2, 4).reshape(g, m, tc * ch)
    gb = 4 if g % 4 == 0 else 1
    y4, fin = pl.pallas_call(
        functools.partial(_s5_kernel, gb=gb, nb=nb, n_levels=n_levels, max_levels=max_levels, sw=sw),
        grid=(g // gb,),
        in_specs=[
            pl.BlockSpec((gb, m, tc * ch), lambda i: (i, 0, 0)),
            pl.BlockSpec((gb,) + w1.shape[1:], lambda i: (i, 0, 0)),
            pl.BlockSpec((gb,) + w2.shape[1:], lambda i: (i, 0, 0)),
            pl.BlockSpec((gb,) + lev.shape[1:], lambda i: (i, 0, 0)),
            pl.BlockSpec((gb, 4, nb, sw), lambda i: (i, 0, 0, 0)),
        ],
        out_specs=[
            pl.BlockSpec((gb, m, tc * ch), lambda i: (i, 0, 0)),
            pl.BlockSpec((gb, 2, nb, sw), lambda i: (i, 0, 0, 0)),
        ],
        out_shape=[jax.ShapeDtypeStruct((g, m, tc * ch), F32), jax.ShapeDtypeStruct((g, 2, nb, sw), F32)],
        compiler_params=_cparams("parallel"),
        name="s5",
    )(u4, w1, w2, lev, h0)
    y = y4.reshape(g, nch, nb, tc, ch).transpose(2, 1, 3, 0, 4).reshape(nb * l, g * ch)
    return y, fin


def _s5_post_kernel(y_ref, u_ref, d_ref, o_ref, ob_ref):
    y = jax.nn.gelu(y_ref[...] + d_ref[...] * u_ref[...])
    o_ref[...] = y
    ob_ref[...] = y.astype(BF16)


def _s5_post(y, proj, s5_d):
    t, d_b = y.shape
    tm = _tile(t, 512)
    spec = pl.BlockSpec((tm, d_b), lambda i: (i, 0))
    return pl.pallas_call(
        _s5_post_kernel,
        grid=(t // tm,),
        in_specs=[spec, pl.BlockSpec((tm, d_b), lambda i: (i, 5)), pl.BlockSpec((1, d_b), lambda i: (0, 0))],
        out_specs=[spec, spec],
        out_shape=[jax.ShapeDtypeStruct((t, d_b), F32), jax.ShapeDtypeStruct((t, d_b), BF16)],
        compiler_params=_cparams("parallel"),
        name="s5_post",
    )(y, proj, s5_d.reshape(1, d_b))


def _dft_cos_sin(num, den, scale):
    ang = (2.0 * math.pi / den) * (num % den).astype(F32)
    return (jnp.cos(ang) * scale).astype(BF16), (jnp.sin(ang) * scale).astype(BF16)


def _fnet_pos_kernel(c_ref, s_ref, uv_ref, o_ref, *, gd):
    uv = uv_ref[...]
    o_ref[...] = (_dot(c_ref[...], uv[:, :gd]) - _dot(s_ref[...], uv[:, gd:])).astype(o_ref.dtype)


def _fnet_pos(cp, sp, uv, row0, nb, l, gd):
    ng = uv.shape[1] // (2 * gd)
    const = pl.BlockSpec((l, l), lambda b, g: (0, 0))
    return pl.pallas_call(
        functools.partial(_fnet_pos_kernel, gd=gd),
        grid=(nb, ng),
        in_specs=[const, const, pl.BlockSpec((l, 2 * gd), lambda b, g: (row0 // l + b, g))],
        out_specs=pl.BlockSpec((l, gd), lambda b, g: (b, g)),
        out_shape=jax.ShapeDtypeStruct((nb * l, ng * gd), BF16),
        compiler_params=_cparams("parallel", "parallel"),
        name="fnet_pos",
    )(cp, sp, uv)


def _fourier_mix(xn, tp, lp, ls):
    t, d = xn.shape
    gd = d // FNET_GROUPS
    kk = jnp.arange(gd)
    cc, sc = _dft_cos_sin(kk[:, None] * kk[None, :], gd, gd**-0.5)
    cs = jnp.concatenate([cc, sc], axis=1)
    tm = _tile(t, 1024)
    uv = pl.pallas_call(
        functools.partial(_mm_kernel, n_pairs=1, epilogue=_plain),
        grid=(t // tm, FNET_GROUPS),
        in_specs=[pl.BlockSpec((tm, gd), lambda i, g: (i, g)), pl.BlockSpec((gd, 2 * gd), lambda i, g: (0, 0))],
        out_specs=pl.BlockSpec((tm, 2 * gd), lambda i, g: (i, g)),
        out_shape=jax.ShapeDtypeStruct((t, 2 * d), BF16),
        compiler_params=_cparams("parallel", "parallel"),
        name="fnet_chan",
    )(xn, cs)
    pp = jnp.arange(lp)
    cp_p, sp_p = _dft_cos_sin(pp[:, None] * pp[None, :], lp, lp**-0.5)
    ps = jnp.arange(ls)
    r, c = ps // GRID_W, ps % GRID_W
    rows = ls // GRID_W
    lcm = rows * GRID_W // math.gcd(rows, GRID_W)
    num = (lcm // rows) * (r[:, None] * r[None, :]) + (lcm // GRID_W) * (c[:, None] * c[None, :])
    cp_s, sp_s = _dft_cos_sin(num, lcm, ls**-0.5)
    z_p = _fnet_pos(cp_p, sp_p, uv, 0, tp // lp, lp, gd)
    z_s = _fnet_pos(cp_s, sp_s, uv, tp, (t - tp) // ls, ls, gd)
    return jnp.concatenate([z_p, z_s], axis=0)


def _router_kernel(x_ref, w_ref, b_ref, o_ref, *, ng, npg):
    logits = _dot(x_ref[...].astype(BF16), w_ref[...]) + b_ref[...]
    lane = lax.broadcasted_iota(jnp.int32, logits.shape, 1).astype(F32)
    big = 1e9

    def first_max(vals):
        m = jnp.max(vals, axis=-1, keepdims=True)
        return m, jnp.min(jnp.where(vals == m, lane, big), axis=-1, keepdims=True)

    gl = jnp.where(lane < ng, logits, NEG)
    gmax, gidx = first_max(gl)
    gprob = 1.0 / jnp.sum(jnp.exp(gl - gmax), axis=-1, keepdims=True)
    lo = ng + gidx * npg
    el = jnp.where((lane >= lo) & (lane < lo + npg), logits, NEG)
    m1, i1 = first_max(el)
    m2, i2 = first_max(jnp.where(lane == i1, NEG, el))
    e = jnp.exp(m2 - m1)
    w1 = gprob / (1.0 + e)
    out = jnp.where(lane == 0, i1 - ng, jnp.where(lane == 1, i2 - ng, jnp.where(lane == 2, w1, jnp.where(lane == 3, w1 * e, 0.0))))
    o_ref[...] = out


def _router(xn, wr_g, br_g, wr_e, br_e):
    t, d = xn.shape
    ng, ne = wr_g.shape[1], wr_e.shape[1]
    w = jnp.zeros((d, 128), F32).at[:, :ng].set(wr_g).at[:, ng : ng + ne].set(wr_e).astype(BF16)
    b = jnp.zeros((1, 128), F32).at[0, :ng].set(br_g).at[0, ng : ng + ne].set(br_e)
    tm = _tile(t, 512)
    return pl.pallas_call(
        functools.partial(_router_kernel, ng=ng, npg=ne // ng),
        grid=(t // tm,),
        in_specs=[pl.BlockSpec((tm, d), lambda i: (i, 0)), pl.BlockSpec((d, 128), lambda i: (0, 0)), pl.BlockSpec((1, 128), lambda i: (0, 0))],
        out_specs=pl.BlockSpec((tm, 128), lambda i: (i, 0)),
        out_shape=jax.ShapeDtypeStruct((t, 128), F32),
        compiler_params=_cparams("parallel"),
        name="router",
    )(xn, w, b)


def _moe_plan(rout, n_exp, tm, nt):
    t = rout.shape[0]
    flat = rout[:, :2].astype(jnp.int32).T.reshape(-1)
    oh = (flat[:, None] == jnp.arange(n_exp)[None, :]).astype(jnp.int32)
    blk = _tile(2 * t, 128)
    c1 = jnp.cumsum(oh.reshape(-1, blk, n_exp), axis=1)
    tot = c1[:, -1]
    cs = (c1 + (jnp.cumsum(tot, axis=0) - tot)[:, None]).reshape(2 * t, n_exp)
    rank = jnp.sum(oh * cs, axis=1) - 1
    counts = cs[-1]
    tiles = (counts + tm - 1) // tm
    tile_end = jnp.cumsum(tiles)
    pos = (tile_end - tiles)[flat] * tm + rank
    row_token = jnp.zeros((nt * tm,), jnp.int32).at[pos].set(jnp.tile(jnp.arange(t, dtype=jnp.int32), 2))
    tile_ids = jnp.arange(nt)
    tile_expert = jnp.minimum(jnp.searchsorted(tile_end, tile_ids, side="right"), n_exp - 1).astype(jnp.int32)
    tile_valid = (tile_ids < tile_end[-1]).astype(jnp.int32)
    return row_token, pos.astype(jnp.int32), tile_expert, tile_valid


def _moe_kernel(te_ref, tv_ref, rt_ref, x_hbm, wg_ref, wu_ref, wd_ref, o_ref, xbuf, sem, *, tm):
    del te_ref
    i = pl.program_id(0)
    nt = pl.num_programs(0)
    slot = i % 2

    def issue(tile, s):
        def body(r, carry):
            tok = rt_ref[tile * tm + r]
            pltpu.make_async_copy(x_hbm.at[pl.ds(tok, 1)], xbuf.at[s, pl.ds(r, 1)], sem.at[s]).start()
            return carry

        lax.fori_loop(0, tm, body, 0)

    @pl.when(i == 0)
    def _():
        issue(0, 0)

    nxt = jnp.minimum(i + 1, nt - 1)

    @pl.when(jnp.logical_and(i + 1 < nt, tv_ref[nxt] == 1))
    def _():
        issue(i + 1, 1 - slot)

    @pl.when(tv_ref[i] == 1)
    def _():
        pltpu.make_async_copy(x_hbm.at[pl.ds(0, tm)], xbuf.at[slot], sem.at[slot]).wait()
        x = xbuf[slot].astype(BF16)
        hg = _dot(x, wg_ref[0])
        h = (_silu(hg) * _dot(x, wu_ref[0])).astype(BF16)
        o_ref[...] = _dot(h, wd_ref[0])

    @pl.when(tv_ref[i] == 0)
    def _():
        o_ref[...] = jnp.zeros_like(o_ref)


def _combine_kernel(pos_ref, y_hbm, x_ref, r_ref, g_ref, o_ref, buf, sem, *, tm, t):
    i = pl.program_id(0)
    n = pl.num_programs(0)
    slot = i % 2

    def issue(tile, s):
        def body(r, carry):
            for k in range(2):
                p = pos_ref[k * t + tile * tm + r]
                pltpu.make_async_copy(y_hbm.at[pl.ds(p, 1)], buf.at[s, k, pl.ds(r, 1)], sem.at[s]).start()
            return carry

        lax.fori_loop(0, tm, body, 0)

    @pl.when(i == 0)
    def _():
        issue(0, 0)

    @pl.when(i + 1 < n)
    def _():
        issue(i + 1, 1 - slot)

    for k in range(2):
        pltpu.make_async_copy(y_hbm.at[pl.ds(0, tm)], buf.at[slot, k], sem.at[slot]).wait()
    w = r_ref[...]
    o_ref[...] = x_ref[...] + g_ref[0] * (w[:, 2:3] * buf[slot, 0] + w[:, 3:4] * buf[slot, 1])


def _moe(xn, x, gate, params, tp, ls):
    wr_g, br_g, wr_e, br_e, w_gate, w_up, w_down = params
    t, d = x.shape
    n_exp, _, f = w_gate.shape
    rout = _router(xn, wr_g, br_g, wr_e, br_e)
    tm = 256 if t >= 4096 else 64
    nt = (2 * t + tm - 1) // tm + n_exp
    row_token, pos, tile_expert, tile_valid = _moe_plan(rout, n_exp, tm, nt)
    y = pl.pallas_call(
        functools.partial(_moe_kernel, tm=tm),
        grid_spec=pltpu.PrefetchScalarGridSpec(
            num_scalar_prefetch=3,
            grid=(nt,),
            in_specs=[
                pl.BlockSpec(memory_space=pl.ANY),
                pl.BlockSpec((1, d, f), lambda i, te, tv, rt: (te[i], 0, 0)),
                pl.BlockSpec((1, d, f), lambda i, te, tv, rt: (te[i], 0, 0)),
                pl.BlockSpec((1, f, d), lambda i, te, tv, rt: (te[i], 0, 0)),
            ],
            out_specs=pl.BlockSpec((tm, d), lambda i, te, tv, rt: (i, 0)),
            scratch_shapes=[pltpu.VMEM((2, tm, d), F32), pltpu.SemaphoreType.DMA((2,))],
        ),
        out_shape=jax.ShapeDtypeStruct((nt * tm, d), F32),
        compiler_params=_cparams("arbitrary"),
        name="moe_experts",
    )(tile_expert, tile_valid, row_token, xn, w_gate.astype(BF16), w_up.astype(BF16), w_down.astype(BF16))

    tc = _tile(math.gcd(tp, ls), 128)
    return pl.pallas_call(
        functools.partial(_combine_kernel, tm=tc, t=t),
        grid_spec=pltpu.PrefetchScalarGridSpec(
            num_scalar_prefetch=1,
            grid=(t // tc,),
            in_specs=[
                pl.BlockSpec(memory_space=pl.ANY),
                pl.BlockSpec((tc, d), lambda i, p: (i, 0)),
                pl.BlockSpec((tc, 128), lambda i, p: (i, 0)),
                pl.BlockSpec((1, 1, d), lambda i, p: (_mod_row(i, tc, tp, ls), 0, 0)),
            ],
            out_specs=pl.BlockSpec((tc, d), lambda i, p: (i, 0)),
            scratch_shapes=[pltpu.VMEM((2, 2, tc, d), F32), pltpu.SemaphoreType.DMA((2,))],
        ),
        out_shape=jax.ShapeDtypeStruct((t, d), F32),
        compiler_params=_cparams("arbitrary"),
        name="moe_combine",
    )(pos, y, x, rout, gate)


def kernel(x_prompt, x_sample, state_hgrn, state_s5_re, state_s5_im, c, c_ctx, ada_w, ada_b, norm_mix, norm_ffn, norm_final, ab_w_in, ab_w_out, hgrn_lb_theta, hgrn_gnorm, s5_lam_re, s5_lam_im, s5_b_re, s5_b_im, s5_c_re, s5_c_im, s5_log_step, s5_d, s5_glu_w, s5_glu_b, fnet_w_out, moe_wr_group, moe_br_group, moe_wr_expert, moe_br_expert, moe_w_gate, moe_w_up, moe_w_down):
    bp, lp, d = x_prompt.shape
    bs, ls, _ = x_sample.shape
    tp, ts = bp * lp, bs * ls
    depth = ada_w.shape[0]
    nh, dk, dv = state_hgrn.shape[3:]
    d_a = nh * dk
    n_grp, s5_p = state_s5_re.shape[3:]
    d_b = s5_d.shape[1]
    nmod = 1 + bs

    x = jnp.concatenate([x_prompt.reshape(tp, d), x_sample.reshape(ts, d)], axis=0)
    c8 = jnp.zeros((8, d), F32).at[0].set(c_ctx).at[1:nmod].set(c)
    mod_all = _ada(c8, ada_w, ada_b)
    lb_all = jnp.cumsum(jax.nn.softmax(hgrn_lb_theta.astype(F32), axis=1), axis=1)
    max_levels = int(math.log2(max(lp, ls) // S5_CHUNK))

    new_h, new_re, new_im = [], [], []
    i_ab = i_c = 0
    for l in range(depth):
        mod = mod_all[l, :nmod].reshape(nmod, 6, 1, d).transpose(1, 0, 2, 3)
        xn = _norm_mod(x, norm_mix[l], mod[0], mod[1], tp, ls, BF16)
        if l % 2 == 0:
            tm = _tile(math.gcd(tp, ls), 1024)
            n_in = ab_w_in.shape[2]
            proj = _matmul([(xn, ab_w_in[i_ab].astype(BF16))], n_in, F32, tm, _tile(n_in, 1024, 128), name="proj_in")
            w1, w2, lev = _s5_tables(s5_lam_re[i_ab], s5_lam_im[i_ab], s5_b_re[i_ab], s5_b_im[i_ab], s5_c_re[i_ab], s5_c_im[i_ab], s5_log_step[i_ab], max_levels)
            u = proj[:, 5 * d_a :]
            o_parts, y_parts, h_fin, x_fin = [[], []], [], [], None
            for row0, nb, ln, h0, x0 in (
                (0, bp, lp, jnp.zeros((bp, 2, nh, dk, dv), F32), jnp.zeros((bp, 2, n_grp, 2 * s5_p), F32)),
                (tp, bs, ls, state_hgrn[:, i_ab].astype(F32), jnp.concatenate([state_s5_re[:, i_ab], state_s5_im[:, i_ab]], axis=-1).astype(F32)),
            ):
                fins = []
                for dr in range(2):
                    o_dir, s_fin = _hgrn(proj, lb_all[dr, i_ab], h0[:, dr], row0, nb, ln, d_a, 1 + dr, dr == 1)
                    o_parts[dr].append(o_dir)
                    fins.append(s_fin)
                x0s = jnp.concatenate([x0[..., s5_p:], x0[..., :s5_p]], axis=-1)
                h0_s5 = jnp.stack([x0[:, 0], x0s[:, 0], x0[:, 1], x0s[:, 1]], axis=0).transpose(2, 0, 1, 3)
                y_set, fin_s5 = _s5(u[row0 : row0 + nb * ln], w1, w2, lev, h0_s5, nb, ln, max_levels)
                y_parts.append(y_set)
                if row0 == 0:
                    h_fin = jnp.stack(fins, axis=1)
                    x_fin = fin_s5.transpose(2, 1, 0, 3)
            o_a = _hgrn_post(jnp.concatenate(o_parts[0], axis=0), jnp.concatenate(o_parts[1], axis=0), proj, hgrn_gnorm[i_ab], nh, dv)
            yg, yg_b = _s5_post(jnp.concatenate(y_parts, axis=0), proj, s5_d[i_ab])
            tn = _tile(d_b, 1024, 128)
            y_glu = _matmul(
                [(yg_b, s5_glu_w[i_ab].astype(BF16))], d_b, BF16, tm, tn, _glu, (yg, s5_glu_b[i_ab].reshape(1, d_b)),
                (pl.BlockSpec((tm, tn), lambda i, j: (i, j)), pl.BlockSpec((1, tn), lambda i, j: (0, j))), "s5_glu",
            )
            w_out = ab_w_out[i_ab].astype(BF16)
            x = _matmul_residual([(o_a, w_out[:d_a]), (y_glu, w_out[d_a:])], x, mod[2], tp, ls, "mix_out")
            new_h.append(h_fin)
            new_re.append(x_fin[..., :s5_p])
            new_im.append(x_fin[..., s5_p:])
            i_ab += 1
        else:
            z = _fourier_mix(xn, tp, lp, ls)
            x = _matmul_residual([(z, fnet_w_out[i_c].astype(BF16))], x, mod[2], tp, ls, "fnet_out")
            i_c += 1
        xn = _norm_mod(x, norm_ffn[l], mod[3], mod[4], tp, ls, F32)
        moe_params = (moe_wr_group[l], moe_br_group[l], moe_wr_expert[l], moe_br_expert[l], moe_w_gate[l], moe_w_up[l], moe_w_down[l])
        x = _moe(xn, x, mod[5], moe_params, tp, ls)
    y = _rmsnorm(x, norm_final)
    return (
        y[:tp].reshape(bp, lp, d),
        y[tp:].reshape(bs, ls, d),
        jnp.stack(new_h, axis=1),
        jnp.stack(new_re, axis=1),
        jnp.stack(new_im, axis=1),
    )
```

```python
import functools
import math

import jax
import jax.numpy as jnp
from jax import lax
from jax.experimental import pallas as pl
from jax.experimental.pallas import tpu as pltpu

F32 = jnp.float32
BF16 = jnp.bfloat16
EPS = 1e-6
NEG = -1e30
V7X_VMEM_LIMIT_BYTES = 56 * 1024 * 1024
FNET_GROUPS = 8
GRID_W = 64
HGRN_CHUNK = 128
S5_CHUNK = 16
S5_GB = 8


def _cparams(*sem):
    return pltpu.CompilerParams(dimension_semantics=sem, vmem_limit_bytes=V7X_VMEM_LIMIT_BYTES)


def _tile(n, pref, mult=8):
    if n <= pref:
        return n
    for t in range(pref - pref % mult, 0, -mult):
        if n % t == 0:
            return t
    raise ValueError(f"no tile for {n}")


def _drop_arg(body, index):
    def wrapped(*refs):
        return body(*refs[:index], *refs[index + 1 :])

    return wrapped


def _dot(a, b):
    return jnp.dot(a, b, preferred_element_type=F32)


def _dot_nt(a, b):
    return lax.dot_general(a, b, (((1,), (1,)), ((), ())), preferred_element_type=F32)


def _dot_tn(a, b):
    return lax.dot_general(a, b, (((0,), (0,)), ((), ())), preferred_element_type=F32)


def _silu(x):
    return x * jax.nn.sigmoid(x)


def _ada_kernel(c_ref, w_ref, b_ref, o_ref):
    a = _silu(c_ref[...]).astype(BF16)
    o_ref[0] = _dot(a, w_ref[0].astype(BF16)) + b_ref[0]


def _ada(c8, ada_w, ada_b):
    depth, d, n = ada_w.shape
    tn = _tile(n, 1024, 128)
    return pl.pallas_call(
        _ada_kernel,
        grid=(depth, n // tn),
        in_specs=[
            pl.BlockSpec((8, d), lambda l, j: (0, 0)),
            pl.BlockSpec((1, d, tn), lambda l, j: (l, 0, j)),
            pl.BlockSpec((1, 1, tn), lambda l, j: (l, 0, j)),
        ],
        out_specs=pl.BlockSpec((1, 8, tn), lambda l, j: (l, 0, j)),
        out_shape=jax.ShapeDtypeStruct((depth, 8, n), F32),
        compiler_params=_cparams("parallel", "parallel"),
        name="ada",
    )(c8, ada_w, ada_b.reshape(depth, 1, n))


def _mod_row(i, tm, tp, ls):
    r0 = i * tm
    return jnp.where(r0 < tp, 0, 1 + (r0 - tp) // ls)


def _norm_mod_kernel(x_ref, g_ref, sh_ref, sc_ref, o_ref):
    x = x_ref[...]
    y = x * lax.rsqrt(jnp.mean(x * x, axis=-1, keepdims=True) + EPS) * g_ref[...]
    o_ref[...] = (y * (1.0 + sc_ref[0]) + sh_ref[0]).astype(o_ref.dtype)


def _norm_mod(x, g, shift, scale, tp, ls, out_dtype):
    t, d = x.shape
    tm = _tile(math.gcd(tp, ls), 256)
    mod_spec = pl.BlockSpec((1, 1, d), lambda i: (_mod_row(i, tm, tp, ls), 0, 0))
    return pl.pallas_call(
        _norm_mod_kernel,
        grid=(t // tm,),
        in_specs=[pl.BlockSpec((tm, d), lambda i: (i, 0)), pl.BlockSpec((1, d), lambda i: (0, 0)), mod_spec, mod_spec],
        out_specs=pl.BlockSpec((tm, d), lambda i: (i, 0)),
        out_shape=jax.ShapeDtypeStruct((t, d), out_dtype),
        compiler_params=_cparams("parallel"),
        name="norm_mod",
    )(x, g.reshape(1, d), shift, scale)


def _rmsnorm_kernel(x_ref, g_ref, o_ref):
    x = x_ref[...]
    o_ref[...] = x * lax.rsqrt(jnp.mean(x * x, axis=-1, keepdims=True) + EPS) * g_ref[...]


def _rmsnorm(x, g, row0, n):
    d = x.shape[1]
    tm = _tile(math.gcd(row0, n) if row0 else n, 256)
    return pl.pallas_call(
        _rmsnorm_kernel,
        grid=(n // tm,),
        in_specs=[pl.BlockSpec((tm, d), lambda i: (row0 // tm + i, 0)), pl.BlockSpec((1, d), lambda i: (0, 0))],
        out_specs=pl.BlockSpec((tm, d), lambda i: (i, 0)),
        out_shape=jax.ShapeDtypeStruct((n, d), F32),
        compiler_params=_cparams("parallel"),
        name="final_norm",
    )(x, g.reshape(1, d))


def _mm_kernel(*refs, n_pairs, epilogue):
    o_ref = refs[-1]
    acc = None
    for a_ref, b_ref in zip(refs[:n_pairs], refs[n_pairs : 2 * n_pairs]):
        part = _dot(a_ref[...].astype(BF16), b_ref[...])
        acc = part if acc is None else acc + part
    o_ref[...] = epilogue(acc, *refs[2 * n_pairs : -1]).astype(o_ref.dtype)


def _plain(acc):
    return acc


def _residual(acc, x_ref, g_ref):
    return x_ref[...] + g_ref[0] * acc


def _glu(acc, y_ref, b_ref):
    return y_ref[...] * jax.nn.sigmoid(acc + b_ref[...])


def _matmul(pairs, n, out_dtype, tm, tn, epilogue=_plain, extra=(), extra_specs=(), name="matmul"):
    t = pairs[0][0].shape[0]
    a_specs = [pl.BlockSpec((tm, a.shape[1]), lambda i, j: (i, 0)) for a, _ in pairs]
    b_specs = [pl.BlockSpec((b.shape[0], tn), lambda i, j: (0, j)) for _, b in pairs]
    return pl.pallas_call(
        functools.partial(_mm_kernel, n_pairs=len(pairs), epilogue=epilogue),
        grid=(t // tm, n // tn),
        in_specs=a_specs + b_specs + list(extra_specs),
        out_specs=pl.BlockSpec((tm, tn), lambda i, j: (i, j)),
        out_shape=jax.ShapeDtypeStruct((t, n), out_dtype),
        compiler_params=_cparams("parallel", "parallel"),
        name=name,
    )(*[a for a, _ in pairs], *[b for _, b in pairs], *extra)


def _matmul_residual(pairs, x, gate, tp, ls, name):
    t, d = x.shape
    tm = _tile(math.gcd(tp, ls), 1024)
    tn = _tile(d, 1024, 128)
    specs = [
        pl.BlockSpec((tm, tn), lambda i, j: (i, j)),
        pl.BlockSpec((1, 1, tn), lambda i, j: (_mod_row(i, tm, tp, ls), 0, j)),
    ]
    return _matmul(pairs, d, F32, tm, tn, _residual, (x, gate), specs, name)


def _split3(x):
    hi = x.astype(BF16)
    r1 = x - hi.astype(F32)
    mid = r1.astype(BF16)
    lo = (r1 - mid.astype(F32)).astype(BF16)
    return hi, mid, lo


def _hgrn_kernel(q_ref, f_ref, v_ref, lb_ref, s0_ref, o_ref, sfin_ref, st_ref, *, reverse, hb, c, dk):
    ci = pl.program_id(2)

    @pl.when(ci == 0)
    def _():
        for j in range(hb):
            st_ref[j] = s0_ref[0, j].T

    row = lax.broadcasted_iota(jnp.int32, (c, c), 0)
    col = lax.broadcasted_iota(jnp.int32, (c, c), 1)
    tri = jnp.where((col >= row) if reverse else (col <= row), 1.0, 0.0).astype(BF16)
    before = (col > row) if reverse else (col < row)
    diff = row ^ col
    trow = lax.broadcasted_iota(jnp.int32, (c, dk), 0)
    n_levels = int(math.log2(c))

    for j in range(hb):
        sl = slice(j * dk, (j + 1) * dk)
        q = _silu(q_ref[:, sl])
        lb = lb_ref[:, sl]
        f = lb + (1.0 - lb) * jax.nn.sigmoid(f_ref[:, sl])
        k = 1.0 - f
        v = v_ref[:, sl]
        vb = v.astype(BF16)
        hi, mid, lo = _split3(jnp.log(f))
        bcum = _dot(tri, hi) + _dot(tri, mid) + _dot(tri, lo)
        btot = bcum[0:1] if reverse else bcum[c - 1 : c]

        st = st_ref[j]
        o = _dot_nt((q * jnp.exp(bcum)).astype(BF16), st.astype(BF16))
        ke = (k * jnp.exp(btot - bcum)).astype(BF16)
        st_ref[j] = st * jnp.exp(btot) + _dot_tn(vb, ke)

        att = jnp.where(row == col, _dot_nt(q.astype(BF16), k.astype(BF16)), 0.0)
        f_prev = pltpu.roll(f, 1, 0)
        f_next = pltpu.roll(f, c - 1, 0)
        for h in range(n_levels):
            b = 1 << h
            later = ((trow & b) == 0) if reverse else ((trow & b) != 0)
            if b == 1:
                ex, ey = f, 1.0
            elif b == 2:
                o4 = trow & 3
                if reverse:
                    ex = jnp.where(o4 == 0, f * f_next, f)
                    ey = jnp.where(o4 == 3, f_prev, 1.0)
                else:
                    ex = jnp.where(o4 == 3, f * f_prev, f)
                    ey = jnp.where(o4 == 0, f_next, 1.0)
            else:
                b3 = bcum.reshape(c // (2 * b), 2 * b, dk)
                edge = b if reverse else b - 1
                ex = ey = jnp.exp(-jnp.abs(b3 - b3[:, edge : edge + 1, :])).reshape(c, dk)
            x_l = jnp.where(later, q * ex, 0.0).astype(BF16)
            y_l = jnp.where(later, 0.0, k * ey).astype(BF16)
            att = jnp.where(before & ((diff >> h) == 1), _dot_nt(x_l, y_l), att)
        o_ref[:, sl] = o + _dot(att.astype(BF16), vb)

    @pl.when(ci == pl.num_programs(2) - 1)
    def _():
        for j in range(hb):
            sfin_ref[0, j] = st_ref[j].T


def _hgrn(proj, lb, s0, row0, nb, l, d_a, f_section, reverse, prev=None):
    t = proj.shape[0]
    h, dk, dv = s0.shape[1:]
    c = min(HGRN_CHUNK, l)
    nc = l // c
    hb = 4 if h % 4 == 0 else 1
    hw = hb * dk
    npb = d_a // hw

    def rows(b, ci):
        return row0 // c + b * nc + ((nc - 1 - ci) if reverse else ci)

    def in_spec(section):
        return pl.BlockSpec((c, hw), lambda b, g, ci: (rows(b, ci), section * npb + g))

    in_specs = [
        in_spec(0),
        in_spec(f_section),
        in_spec(3),
        pl.BlockSpec((1, hw), lambda b, g, ci: (0, g)),
        pl.BlockSpec((1, hb, dk, dv), lambda b, g, ci: (b, g, 0, 0)),
    ]
    args = [proj, proj, proj, lb.reshape(1, d_a), s0]
    kern = functools.partial(_hgrn_kernel, reverse=reverse, hb=hb, c=c, dk=dk)
    aliases = {}
    if prev is not None:
        in_specs.append(pl.BlockSpec(memory_space=pl.ANY))
        args.append(prev)
        aliases = {5: 0}
        kern = _drop_arg(kern, 5)
    return pl.pallas_call(
        kern,
        grid=(nb, npb, nc),
        in_specs=in_specs,
        out_specs=[
            pl.BlockSpec((c, hw), lambda b, g, ci: (rows(b, ci), g)),
            pl.BlockSpec((1, hb, dk, dv), lambda b, g, ci: (b, g, 0, 0)),
        ],
        out_shape=[jax.ShapeDtypeStruct((t, d_a), F32), jax.ShapeDtypeStruct((nb, h, dk, dv), F32)],
        scratch_shapes=[pltpu.VMEM((hb, dv, dk), F32)],
        input_output_aliases=aliases,
        compiler_params=_cparams("parallel", "parallel", "arbitrary"),
        name="hgrn_bwd" if reverse else "hgrn_fwd",
    )(*args)


def _hgrn_post_kernel(of_ref, ob_ref, g_ref, gn_ref, o_ref, *, nh, dv):
    for h in range(nh):
        sl = slice(h * dv, (h + 1) * dv)
        o = of_ref[:, sl] + ob_ref[:, sl]
        o = o * lax.rsqrt(jnp.mean(o * o, axis=-1, keepdims=True) + EPS) * gn_ref[:, sl]
        o_ref[:, sl] = (o * _silu(g_ref[:, sl])).astype(o_ref.dtype)


def _hgrn_post(o_fw, o_bw, proj, gnorm, nh, dv):
    t, d_a = o_fw.shape
    tm = _tile(t, 256)
    spec = pl.BlockSpec((tm, d_a), lambda i: (i, 0))
    return pl.pallas_call(
        functools.partial(_hgrn_post_kernel, nh=nh, dv=dv),
        grid=(t // tm,),
        in_specs=[spec, spec, pl.BlockSpec((tm, d_a), lambda i: (i, 4)), pl.BlockSpec((1, d_a), lambda i: (0, 0))],
        out_specs=spec,
        out_shape=jax.ShapeDtypeStruct((t, d_a), BF16),
        compiler_params=_cparams("parallel"),
        name="hgrn_post",
    )(o_fw, o_bw, proj, gnorm.reshape(1, d_a))


def _s5_tables(lam_re, lam_im, b_re, b_im, c_re, c_im, log_step, n_levels):
    hp = lax.Precision.HIGHEST
    tc = S5_CHUNK
    gb = S5_GB
    g, p, ch = b_re.shape[1:]
    nblk = g // gb
    lam = lax.complex(jnp.minimum(lam_re.astype(F32), -1e-4), lam_im.astype(F32))
    z = lam * jnp.exp(log_step.astype(F32))[..., None]
    b_bar = ((jnp.exp(z) - 1.0) / lam)[..., None] * lax.complex(b_re.astype(F32), b_im.astype(F32))
    cc = lax.complex(c_re.astype(F32), c_im.astype(F32))
    n = jnp.arange(tc + 1, dtype=F32)
    pw = jnp.exp(z[:, :, None, :] * n[None, None, :, None])

    kern = jnp.einsum("dgop,dglp,dgpi->dglio", cc, pw[:, :, :tc], b_bar, precision=hp).real
    lag_g = jnp.concatenate([kern[1][:, :0:-1], (kern[0][:, :1] + kern[1][:, :1]), kern[0][:, 1:]], axis=1)
    lagk = lag_g.reshape(nblk, gb, 2 * tc - 1, ch, ch).transpose(0, 2, 3, 1, 4).reshape(nblk, 2 * tc - 1, ch, gb * ch)

    inc_fw = (pw[0][:, tc - 1 - jnp.arange(tc), :, None] * b_bar[0][:, None]).transpose(0, 1, 3, 2)
    inc_bw = (pw[1][:, jnp.arange(tc), :, None] * b_bar[1][:, None]).transpose(0, 1, 3, 2)
    inc = jnp.concatenate([inc_fw.real, inc_fw.imag, inc_bw.real, inc_bw.imag], axis=-1)
    inct = inc.reshape(nblk, gb, tc, ch, 4 * p).transpose(0, 2, 3, 1, 4).reshape(nblk, tc, ch, gb * 4 * p)

    out_fw = cc[0][:, None] * pw[0][:, 1 + jnp.arange(tc), None, :]
    out_bw = cc[1][:, None] * pw[1][:, tc - jnp.arange(tc), None, :]
    out = jnp.concatenate([out_fw.real, -out_fw.imag, out_bw.real, -out_bw.imag], axis=-1)
    outt = out.reshape(nblk, gb, tc, ch, 4 * p).transpose(0, 2, 4, 1, 3).reshape(nblk, tc, 4 * p, gb * ch)

    steps = (tc * (2 ** jnp.arange(n_levels))).astype(F32)
    lp = jnp.exp(z[:, :, None, :] * steps[None, None, :, None])
    a, b = lp.real, lp.imag
    lev = jnp.stack(
        [jnp.concatenate([a, a], -1), jnp.concatenate([-b, b], -1), jnp.concatenate([b, -b], -1)], axis=3
    )
    lev = lev.transpose(1, 0, 2, 3, 4).reshape(g, 2 * n_levels * 3, 2 * p)
    pad = (-lev.shape[1]) % 8
    lev = jnp.pad(lev, ((0, 0), (0, pad), (0, 0)))
    return lagk, inct.astype(BF16), outt.astype(BF16), lev


def _s5_kernel(u_ref, lagk_ref, inct_ref, outt_ref, lev_ref, h0_ref, d_ref, *rest, nch, n_levels, max_levels, want_fin):
    if want_fin:
        o_ref, fin_ref, w1_ref, wt_ref, w2_ref, a_ref = rest
    else:
        o_ref, w1_ref, wt_ref, w2_ref, a_ref = rest
        fin_ref = None
    tc = S5_CHUNK
    gb = S5_GB
    lanes = u_ref.shape[1]
    ch = lanes // gb
    sw = lev_ref.shape[2]
    m = u_ref.shape[0] // tc
    nbk = m // nch
    b0 = pl.program_id(1) * nbk

    @pl.when(pl.program_id(1) == 0)
    def _():
        r1 = lax.broadcasted_iota(jnp.int32, (lanes, gb * 2 * sw), 0) // ch
        c1 = lax.broadcasted_iota(jnp.int32, (lanes, gb * 2 * sw), 1) // (2 * sw)
        for s in range(tc):
            w1_ref[s * lanes : (s + 1) * lanes, :] = jnp.where(r1 == c1, jnp.tile(inct_ref[0, s], (gb, 1)), 0.0)
        rd = lax.broadcasted_iota(jnp.int32, (lanes, lanes), 0) // ch
        cd = lax.broadcasted_iota(jnp.int32, (lanes, lanes), 1) // ch
        taps = [jnp.where(rd == cd, jnp.tile(lagk_ref[0, li], (gb, 1)), 0.0).astype(BF16) for li in range(2 * tc - 1)]
        for s in range(tc):
            for t in range(tc):
                wt_ref[s * lanes : (s + 1) * lanes, t * lanes : (t + 1) * lanes] = taps[t - s + tc - 1]
        r2 = lax.broadcasted_iota(jnp.int32, (gb * 2 * sw, lanes), 0) // (2 * sw)
        c2 = lax.broadcasted_iota(jnp.int32, (gb * 2 * sw, lanes), 1) // ch
        for t in range(tc):
            w2_ref[:, t * lanes : (t + 1) * lanes] = jnp.where(r2 == c2, jnp.tile(outt_ref[0, t], (gb, 1)), 0.0)

    xs_f32 = [u_ref[pl.ds(s, m, stride=tc), :] for s in range(tc)]
    xcat = jnp.concatenate([x.astype(BF16) for x in xs_f32], axis=1)
    inc = _dot(xcat, w1_ref[...])

    ridx = lax.broadcasted_iota(jnp.int32, (m, sw), 0)
    cidx = ridx % nch
    bidx = ridx // nch
    carry = []
    for g in range(gb):
        for d in range(2):
            fwd = d == 0
            x = inc[:, (2 * g + d) * sw : (2 * g + d + 1) * sw]
            xr = pltpu.roll(x, sw // 2, 1)
            h0 = jnp.zeros((m, sw), F32)
            h0r = jnp.zeros((m, sw), F32)
            for b in range(nbk):
                h0 = jnp.where(bidx == b, h0_ref[g, 2 * d, pl.ds(b0 + b, 1), :], h0)
                h0r = jnp.where(bidx == b, h0_ref[g, 2 * d + 1, pl.ds(b0 + b, 1), :], h0r)

            def const(j, which, d=d, g=g):
                r = (d * max_levels + j) * 3 + which
                return lev_ref[g, r : r + 1, :]

            entry = (cidx == 0) if fwd else (cidx == nch - 1)
            x = x + jnp.where(entry, h0 * const(0, 0) + h0r * const(0, 1), 0.0)
            xr = xr + jnp.where(entry, h0r * const(0, 0) + h0 * const(0, 2), 0.0)
            for j in range(n_levels):
                k = 1 << j
                keep = (cidx >= k) if fwd else (cidx < nch - k)
                shift = k if fwd else m - k
                sx = jnp.where(keep, pltpu.roll(x, shift, 0), 0.0)
                sxr = jnp.where(keep, pltpu.roll(xr, shift, 0), 0.0)
                x, xr = x + const(j, 0) * sx + const(j, 1) * sxr, xr + const(j, 0) * sxr + const(j, 2) * sx
            carry.append(jnp.where(entry, h0, pltpu.roll(x, 1 if fwd else m - 1, 0)).astype(BF16))
            if want_fin:
                a_ref[...] = x
                fin_ref[g, d] = a_ref[pl.ds(nch - 1 if fwd else 0, nbk, stride=nch), :]
    y = _dot(xcat, wt_ref[...]) + _dot(jnp.concatenate(carry, axis=1), w2_ref[...])
    for t in range(tc):
        o_ref[pl.ds(t, m, stride=tc), :] = jax.nn.gelu(y[:, t * lanes : (t + 1) * lanes] + d_ref[...] * xs_f32[t])


def _s5(proj, u_col0, tables, h0, s5_d, row0, nb, l, max_levels, want_fin, prev=None):
    lagk, inct, outt, lev = tables
    t = proj.shape[0]
    tc, gb = S5_CHUNK, S5_GB
    nblk = lagk.shape[0]
    lanes = lagk.shape[3]
    g = nblk * gb
    sw = lev.shape[2]
    nch = l // tc
    n_levels = int(math.log2(nch))
    nbk = nb
    if not want_fin:
        nbk = max(k for k in range(1, nb + 1) if nb % k == 0 and row0 % (k * l) == 0 and (k == 1 or k * nch <= 256))
    m = nbk * nch
    rows = m * tc
    in_specs = [
        pl.BlockSpec((rows, lanes), lambda i, b: (row0 // rows + b, u_col0 // lanes + i)),
        pl.BlockSpec((1,) + lagk.shape[1:], lambda i, b: (i, 0, 0, 0)),
        pl.BlockSpec((1,) + inct.shape[1:], lambda i, b: (i, 0, 0, 0)),
        pl.BlockSpec((1,) + outt.shape[1:], lambda i, b: (i, 0, 0, 0)),
        pl.BlockSpec((gb,) + lev.shape[1:], lambda i, b: (i, 0, 0)),
        pl.BlockSpec((gb, 4, nb, sw), lambda i, b: (i, 0, 0, 0)),
        pl.BlockSpec((1, lanes), lambda i, b: (0, i)),
    ]
    args = [proj, lagk, inct, outt, lev, h0, s5_d.reshape(1, g * (lanes // gb))]
    out_specs = [pl.BlockSpec((rows, lanes), lambda i, b: (row0 // rows + b, i))]
    out_shape = [jax.ShapeDtypeStruct((t, g * (lanes // gb)), F32)]
    if want_fin:
        out_specs.append(pl.BlockSpec((gb, 2, nbk, sw), lambda i, b: (i, 0, b, 0)))
        out_shape.append(jax.ShapeDtypeStruct((g, 2, nb, sw), F32))
    kern = functools.partial(_s5_kernel, nch=nch, n_levels=n_levels, max_levels=max_levels, want_fin=want_fin)
    aliases = {}
    if prev is not None:
        in_specs.append(pl.BlockSpec(memory_space=pl.ANY))
        args.append(prev)
        aliases = {7: 0}
        kern = _drop_arg(kern, 7)
    out = pl.pallas_call(
        kern,
        grid=(nblk, nb // nbk),
        in_specs=in_specs,
        out_specs=out_specs,
        out_shape=out_shape,
        scratch_shapes=[
            pltpu.VMEM((tc * lanes, gb * 2 * sw), BF16),
            pltpu.VMEM((tc * lanes, tc * lanes), BF16),
            pltpu.VMEM((gb * 2 * sw, tc * lanes), BF16),
            pltpu.VMEM((m, sw), F32),
        ],
        input_output_aliases=aliases,
        compiler_params=_cparams("parallel", "arbitrary"),
        name="s5",
    )(*args)
    return (out[0], out[1]) if want_fin else (out[0], None)


def _dft_cos_sin(num, den, scale):
    ang = (2.0 * math.pi / den) * (num % den).astype(F32)
    return (jnp.cos(ang) * scale).astype(BF16), (jnp.sin(ang) * scale).astype(BF16)


def _fnet_pos_kernel(c_ref, s_ref, uv_ref, o_ref, *, gd):
    uv = uv_ref[...]
    o_ref[...] = (_dot(c_ref[...], uv[:, :gd]) - _dot(s_ref[...], uv[:, gd:])).astype(o_ref.dtype)


def _fnet_pos(cp, sp, uv, row0, nb, l, gd, prev=None):
    t = uv.shape[0]
    ng = uv.shape[1] // (2 * gd)
    const = pl.BlockSpec((l, l), lambda b, g: (0, 0))
    in_specs = [const, const, pl.BlockSpec((l, 2 * gd), lambda b, g: (row0 // l + b, g))]
    args = [cp, sp, uv]
    kern = functools.partial(_fnet_pos_kernel, gd=gd)
    aliases = {}
    if prev is not None:
        in_specs.append(pl.BlockSpec(memory_space=pl.ANY))
        args.append(prev)
        aliases = {3: 0}
        kern = _drop_arg(kern, 3)
    return pl.pallas_call(
        kern,
        grid=(nb, ng),
        in_specs=in_specs,
        out_specs=pl.BlockSpec((l, gd), lambda b, g: (row0 // l + b, g)),
        out_shape=jax.ShapeDtypeStruct((t, ng * gd), BF16),
        input_output_aliases=aliases,
        compiler_params=_cparams("parallel", "parallel"),
        name="fnet_pos",
    )(*args)


def _fourier_mix(xn, tp, lp, ls):
    t, d = xn.shape
    gd = d // FNET_GROUPS
    kk = jnp.arange(gd)
    cc, sc = _dft_cos_sin(kk[:, None] * kk[None, :], gd, gd**-0.5)
    cs = jnp.concatenate([cc, sc], axis=1)
    tm = _tile(t, 1024)
    uv = pl.pallas_call(
        functools.partial(_mm_kernel, n_pairs=1, epilogue=_plain),
        grid=(t // tm, FNET_GROUPS),
        in_specs=[pl.BlockSpec((tm, gd), lambda i, g: (i, g)), pl.BlockSpec((gd, 2 * gd), lambda i, g: (0, 0))],
        out_specs=pl.BlockSpec((tm, 2 * gd), lambda i, g: (i, g)),
        out_shape=jax.ShapeDtypeStruct((t, 2 * d), BF16),
        compiler_params=_cparams("parallel", "parallel"),
        name="fnet_chan",
    )(xn, cs)
    pp = jnp.arange(lp)
    cp_p, sp_p = _dft_cos_sin(pp[:, None] * pp[None, :], lp, lp**-0.5)
    ps = jnp.arange(ls)
    r, c = ps // GRID_W, ps % GRID_W
    rows = ls // GRID_W
    lcm = rows * GRID_W // math.gcd(rows, GRID_W)
    num = (lcm // rows) * (r[:, None] * r[None, :]) + (lcm // GRID_W) * (c[:, None] * c[None, :])
    cp_s, sp_s = _dft_cos_sin(num, lcm, ls**-0.5)
    z = _fnet_pos(cp_p, sp_p, uv, 0, tp // lp, lp, gd)
    return _fnet_pos(cp_s, sp_s, uv, tp, (t - tp) // ls, ls, gd, prev=z)


def _router_kernel(x_ref, w_ref, b_ref, o_ref, *, ng, npg):
    logits = _dot(x_ref[...].astype(BF16), w_ref[...]) + b_ref[...]
    lane = lax.broadcasted_iota(jnp.int32, logits.shape, 1).astype(F32)
    big = 1e9

    def first_max(vals):
        m = jnp.max(vals, axis=-1, keepdims=True)
        return m, jnp.min(jnp.where(vals == m, lane, big), axis=-1, keepdims=True)

    gl = jnp.where(lane < ng, logits, NEG)
    gmax, gidx = first_max(gl)
    gprob = 1.0 / jnp.sum(jnp.exp(gl - gmax), axis=-1, keepdims=True)
    lo = ng + gidx * npg
    el = jnp.where((lane >= lo) & (lane < lo + npg), logits, NEG)
    m1, i1 = first_max(el)
    m2, i2 = first_max(jnp.where(lane == i1, NEG, el))
    e = jnp.exp(m2 - m1)
    w1 = gprob / (1.0 + e)
    out = jnp.where(lane == 0, i1 - ng, jnp.where(lane == 1, i2 - ng, jnp.where(lane == 2, w1, jnp.where(lane == 3, w1 * e, 0.0))))
    o_ref[...] = out


def _router(xn, wr_g, br_g, wr_e, br_e):
    t, d = xn.shape
    ng, ne = wr_g.shape[1], wr_e.shape[1]
    w = jnp.zeros((d, 128), F32).at[:, :ng].set(wr_g).at[:, ng : ng + ne].set(wr_e).astype(BF16)
    b = jnp.zeros((1, 128), F32).at[0, :ng].set(br_g).at[0, ng : ng + ne].set(br_e)
    tm = _tile(t, 512)
    return pl.pallas_call(
        functools.partial(_router_kernel, ng=ng, npg=ne // ng),
        grid=(t // tm,),
        in_specs=[pl.BlockSpec((tm, d), lambda i: (i, 0)), pl.BlockSpec((d, 128), lambda i: (0, 0)), pl.BlockSpec((1, 128), lambda i: (0, 0))],
        out_specs=pl.BlockSpec((tm, 128), lambda i: (i, 0)),
        out_shape=jax.ShapeDtypeStruct((t, 128), F32),
        compiler_params=_cparams("parallel"),
        name="router",
    )(xn, w, b)


def _moe_plan(rout, n_exp, tm, nt):
    t = rout.shape[0]
    flat = rout[:, :2].astype(jnp.int32).T.reshape(-1)
    oh = (flat[:, None] == jnp.arange(n_exp)[None, :]).astype(jnp.int32)
    blk = _tile(2 * t, 128)
    c1 = jnp.cumsum(oh.reshape(-1, blk, n_exp), axis=1)
    tot = c1[:, -1]
    cs = (c1 + (jnp.cumsum(tot, axis=0) - tot)[:, None]).reshape(2 * t, n_exp)
    rank = jnp.sum(oh * cs, axis=1) - 1
    counts = cs[-1]
    tiles = (counts + tm - 1) // tm
    tile_end = jnp.cumsum(tiles)
    pos = (tile_end - tiles)[flat] * tm + rank
    row_token = jnp.zeros((nt * tm,), jnp.int32).at[pos].set(jnp.tile(jnp.arange(t, dtype=jnp.int32), 2))
    tile_ids = jnp.arange(nt)
    tile_expert = jnp.minimum(jnp.searchsorted(tile_end, tile_ids, side="right"), n_exp - 1).astype(jnp.int32)
    tile_valid = (tile_ids < tile_end[-1]).astype(jnp.int32)
    prev_expert = jnp.concatenate([jnp.full((1,), -1, jnp.int32), tile_expert[:-1]])
    tile_first = (tile_valid * (tile_expert != prev_expert)).astype(jnp.int32)
    owner = jnp.where(tiles > 0, jnp.arange(n_exp), n_exp)
    nxt = jnp.concatenate([lax.cummin(owner[::-1])[::-1][1:], jnp.full((1,), n_exp)])
    tile_next = jnp.where(nxt < n_exp, nxt, -1)[tile_expert].astype(jnp.int32)
    return row_token, pos.astype(jnp.int32), tile_expert, tile_valid, tile_first, tile_next


def _moe_kernel(te_ref, tv_ref, tf_ref, tn_ref, rt_ref, x_hbm, wg_hbm, wu_hbm, wd_hbm, y_hbm,
                xbuf, stg_g, stg_u, stg_d, wb_g, wb_u, wb_d, obuf, sem_x, sem_w, sem_o, *, tm, layer):
    i = pl.program_id(0)
    nt = pl.num_programs(0)
    slot = i % 2
    d, f = stg_g.shape

    def issue_rows(tile, s):
        def body(r, carry):
            tok = rt_ref[tile * tm + r]
            pltpu.make_async_copy(x_hbm.at[pl.ds(tok, 1)], xbuf.at[s, pl.ds(r, 1)], sem_x.at[s]).start()
            return carry

        lax.fori_loop(0, tm, body, 0, unroll=8)

    def weight_copies(e):
        return (
            pltpu.make_async_copy(wg_hbm.at[layer, e], stg_g, sem_w.at[0]),
            pltpu.make_async_copy(wu_hbm.at[layer, e], stg_u, sem_w.at[1]),
            pltpu.make_async_copy(wd_hbm.at[layer, e], stg_d, sem_w.at[2]),
        )

    def out_copy(tile):
        return pltpu.make_async_copy(obuf, y_hbm.at[pl.ds(pl.multiple_of(tile * tm, tm), tm)], sem_o.at[0])

    def cast(src, dst, ck):
        def body(k, carry):
            r = pl.multiple_of(k * ck, ck)
            dst[pl.ds(r, ck), :] = src[pl.ds(r, ck), :].astype(BF16)
            return carry

        lax.fori_loop(0, src.shape[0] // ck, body, 0)

    @pl.when(i == 0)
    def _():
        issue_rows(0, 0)
        for cp in weight_copies(te_ref[0]):
            cp.start()

    nxt = jnp.minimum(i + 1, nt - 1)

    @pl.when(jnp.logical_and(i + 1 < nt, tv_ref[nxt] == 1))
    def _():
        issue_rows(i + 1, 1 - slot)

    prev_valid = jnp.logical_and(i > 0, tv_ref[jnp.maximum(i - 1, 0)] == 1)

    @pl.when(prev_valid)
    def _():
        out_copy(i - 1).wait()

    @pl.when(tv_ref[i] == 1)
    def _():
        @pl.when(tf_ref[i] == 1)
        def _():
            for cp in weight_copies(0):
                cp.wait()
            cast(stg_g, wb_g, min(256, d))
            cast(stg_u, wb_u, min(256, d))
            cast(stg_d, wb_d, min(32, f))

            @pl.when(tn_ref[i] >= 0)
            def _():
                for cp in weight_copies(tn_ref[i]):
                    cp.start()

        pltpu.make_async_copy(x_hbm.at[pl.ds(0, tm)], xbuf.at[slot], sem_x.at[slot]).wait()
        x = xbuf[slot].astype(BF16)
        hg = _dot(x, wb_g[...])
        h = (_silu(hg) * _dot(x, wb_u[...])).astype(BF16)
        obuf[...] = _dot(h, wb_d[...])
        out_copy(i).start()

        @pl.when(i == nt - 1)
        def _():
            out_copy(i).wait()


def _combine_kernel(pos_ref, y_hbm, x_ref, r_ref, g_ref, o_ref, buf, sem, *, tm, t):
    i = pl.program_id(0)
    n = pl.num_programs(0)
    slot = i % 2

    def issue(tile, s):
        def body(r, carry):
            for k in range(2):
                p = pos_ref[k * t + tile * tm + r]
                pltpu.make_async_copy(y_hbm.at[pl.ds(p, 1)], buf.at[s, k, pl.ds(r, 1)], sem.at[s]).start()
            return carry

        lax.fori_loop(0, tm, body, 0)

    @pl.when(i == 0)
    def _():
        issue(0, 0)

    @pl.when(i + 1 < n)
    def _():
        issue(i + 1, 1 - slot)

    for k in range(2):
        pltpu.make_async_copy(y_hbm.at[pl.ds(0, tm)], buf.at[slot, k], sem.at[slot]).wait()
    w = r_ref[...]
    o_ref[...] = x_ref[...] + g_ref[0] * (w[:, 2:3] * buf[slot, 0] + w[:, 3:4] * buf[slot, 1])


def _moe(xn, x, gate, params, layer, tp, ls):
    wr_g, br_g, wr_e, br_e, w_gate, w_up, w_down = params
    t, d = x.shape
    n_exp, _, f = w_gate.shape[1:]
    rout = _router(xn, wr_g, br_g, wr_e, br_e)
    tm = 256 if t >= 4096 else 64
    nt = (2 * t + tm - 1) // tm + n_exp
    row_token, pos, tile_expert, tile_valid, tile_first, tile_next = _moe_plan(rout, n_exp, tm, nt)
    hbm = pl.BlockSpec(memory_space=pl.ANY)
    y = pl.pallas_call(
        functools.partial(_moe_kernel, tm=tm, layer=layer),
        grid_spec=pltpu.PrefetchScalarGridSpec(
            num_scalar_prefetch=5,
            grid=(nt,),
            in_specs=[hbm, hbm, hbm, hbm],
            out_specs=hbm,
            scratch_shapes=[
                pltpu.VMEM((2, tm, d), F32),
                pltpu.VMEM((d, f), F32),
                pltpu.VMEM((d, f), F32),
                pltpu.VMEM((f, d), F32),
                pltpu.VMEM((d, f), BF16),
                pltpu.VMEM((d, f), BF16),
                pltpu.VMEM((f, d), BF16),
                pltpu.VMEM((tm, d), F32),
                pltpu.SemaphoreType.DMA((2,)),
                pltpu.SemaphoreType.DMA((3,)),
                pltpu.SemaphoreType.DMA((1,)),
            ],
        ),
        out_shape=jax.ShapeDtypeStruct((nt * tm, d), F32),
        compiler_params=_cparams("arbitrary"),
        name="moe_experts",
    )(tile_expert, tile_valid, tile_first, tile_next, row_token, xn, w_gate, w_up, w_down)

    tc = _tile(math.gcd(tp, ls), 128)
    return pl.pallas_call(
        functools.partial(_combine_kernel, tm=tc, t=t),
        grid_spec=pltpu.PrefetchScalarGridSpec(
            num_scalar_prefetch=1,
            grid=(t // tc,),
            in_specs=[
                pl.BlockSpec(memory_space=pl.ANY),
                pl.BlockSpec((tc, d), lambda i, p: (i, 0)),
                pl.BlockSpec((tc, 128), lambda i, p: (i, 0)),
                pl.BlockSpec((1, 1, d), lambda i, p: (_mod_row(i, tc, tp, ls), 0, 0)),
            ],
            out_specs=pl.BlockSpec((tc, d), lambda i, p: (i, 0)),
            scratch_shapes=[pltpu.VMEM((2, 2, tc, d), F32), pltpu.SemaphoreType.DMA((2,))],
        ),
        out_shape=jax.ShapeDtypeStruct((t, d), F32),
        compiler_params=_cparams("arbitrary"),
        name="moe_combine",
    )(pos, y, x, rout, gate)


def kernel(x_prompt, x_sample, state_hgrn, state_s5_re, state_s5_im, c, c_ctx, ada_w, ada_b, norm_mix, norm_ffn, norm_final, ab_w_in, ab_w_out, hgrn_lb_theta, hgrn_gnorm, s5_lam_re, s5_lam_im, s5_b_re, s5_b_im, s5_c_re, s5_c_im, s5_log_step, s5_d, s5_glu_w, s5_glu_b, fnet_w_out, moe_wr_group, moe_br_group, moe_wr_expert, moe_br_expert, moe_w_gate, moe_w_up, moe_w_down):
    bp, lp, d = x_prompt.shape
    bs, ls, _ = x_sample.shape
    tp, ts = bp * lp, bs * ls
    depth = ada_w.shape[0]
    nh, dk, dv = state_hgrn.shape[3:]
    d_a = nh * dk
    n_grp, s5_p = state_s5_re.shape[3:]
    d_b = s5_d.shape[1]
    nmod = 1 + bs

    x = jnp.concatenate([x_prompt.reshape(tp, d), x_sample.reshape(ts, d)], axis=0)
    c8 = jnp.zeros((8, d), F32).at[0].set(c_ctx).at[1:nmod].set(c)
    mod_all = _ada(c8, ada_w, ada_b)
    lb_all = jnp.cumsum(jax.nn.softmax(hgrn_lb_theta.astype(F32), axis=1), axis=1)
    max_levels = int(math.log2(max(lp, ls) // S5_CHUNK))

    new_h, new_re, new_im = [], [], []
    i_ab = i_c = 0
    for l in range(depth):
        mod = mod_all[l, :nmod].reshape(nmod, 6, 1, d).transpose(1, 0, 2, 3)
        xn = _norm_mod(x, norm_mix[l], mod[0], mod[1], tp, ls, BF16)
        if l % 2 == 0:
            tm = _tile(math.gcd(tp, ls), 1024)
            n_in = ab_w_in.shape[2]
            proj = _matmul([(xn, ab_w_in[i_ab].astype(BF16))], n_in, F32, tm, _tile(n_in, 1024, 128), name="proj_in")
            tables = _s5_tables(s5_lam_re[i_ab], s5_lam_im[i_ab], s5_b_re[i_ab], s5_b_im[i_ab], s5_c_re[i_ab], s5_c_im[i_ab], s5_log_step[i_ab], max_levels)
            o_dirs, yg, h_fin, x_fin = [None, None], None, None, None
            for row0, nb, ln, h0, x0 in (
                (0, bp, lp, jnp.zeros((bp, 2, nh, dk, dv), F32), jnp.zeros((bp, 2, n_grp, 2 * s5_p), F32)),
                (tp, bs, ls, state_hgrn[:, i_ab].astype(F32), jnp.concatenate([state_s5_re[:, i_ab], state_s5_im[:, i_ab]], axis=-1).astype(F32)),
            ):
                fins = []
                for dr in range(2):
                    o_dirs[dr], s_fin = _hgrn(proj, lb_all[dr, i_ab], h0[:, dr], row0, nb, ln, d_a, 1 + dr, dr == 1, prev=o_dirs[dr])
                    fins.append(s_fin)
                x0s = jnp.concatenate([x0[..., s5_p:], x0[..., :s5_p]], axis=-1)
                h0_s5 = jnp.stack([x0[:, 0], x0s[:, 0], x0[:, 1], x0s[:, 1]], axis=0).transpose(2, 0, 1, 3)
                yg, fin_s5 = _s5(proj, 5 * d_a, tables, h0_s5, s5_d[i_ab], row0, nb, ln, max_levels, row0 == 0, prev=yg)
                if row0 == 0:
                    h_fin = jnp.stack(fins, axis=1)
                    x_fin = fin_s5.transpose(2, 1, 0, 3)
            o_a = _hgrn_post(o_dirs[0], o_dirs[1], proj, hgrn_gnorm[i_ab], nh, dv)
            tn = _tile(d_b, 1024, 128)
            y_glu = _matmul(
                [(yg, s5_glu_w[i_ab].astype(BF16))], d_b, BF16, tm, tn, _glu, (yg, s5_glu_b[i_ab].reshape(1, d_b)),
                (pl.BlockSpec((tm, tn), lambda i, j: (i, j)), pl.BlockSpec((1, tn), lambda i, j: (0, j))), "s5_glu",
            )
            w_out = ab_w_out[i_ab].astype(BF16)
            x = _matmul_residual([(o_a, w_out[:d_a]), (y_glu, w_out[d_a:])], x, mod[2], tp, ls, "mix_out")
            new_h.append(h_fin)
            new_re.append(x_fin[..., :s5_p])
            new_im.append(x_fin[..., s5_p:])
            i_ab += 1
        else:
            z = _fourier_mix(xn, tp, lp, ls)
            x = _matmul_residual([(z, fnet_w_out[i_c].astype(BF16))], x, mod[2], tp, ls, "fnet_out")
            i_c += 1
        xn = _norm_mod(x, norm_ffn[l], mod[3], mod[4], tp, ls, F32)
        moe_params = (moe_wr_group[l], moe_br_group[l], moe_wr_expert[l], moe_br_expert[l], moe_w_gate, moe_w_up, moe_w_down)
        x = _moe(xn, x, mod[5], moe_params, l, tp, ls)
    return (
        _rmsnorm(x, norm_final, 0, tp).reshape(bp, lp, d),
        _rmsnorm(x, norm_final, tp, ts).reshape(bs, ls, d),
        jnp.stack(new_h, axis=1),
        jnp.stack(new_re, axis=1),
        jnp.stack(new_im, axis=1),
    )
```

```python
import functools
import math

import jax
import jax.numpy as jnp
from jax import lax
from jax.experimental import pallas as pl
from jax.experimental.pallas import tpu as pltpu

F32 = jnp.float32
BF16 = jnp.bfloat16
EPS = 1e-6
NEG = -1e30
V7X_VMEM_LIMIT_BYTES = 56 * 1024 * 1024
FNET_GROUPS = 8
GRID_W = 64
HGRN_CHUNK = 128
HGRN_HEADS_PER_STEP = 8
S5_CHUNK = 16
S5_GB = 8


def _cparams(*sem):
    return pltpu.CompilerParams(dimension_semantics=sem, vmem_limit_bytes=V7X_VMEM_LIMIT_BYTES)


def _tile(n, pref, mult=8):
    if n <= pref:
        return n
    for t in range(pref - pref % mult, 0, -mult):
        if n % t == 0:
            return t
    raise ValueError(f"no tile for {n}")


def _drop_arg(body, index):
    def wrapped(*refs):
        return body(*refs[:index], *refs[index + 1 :])

    return wrapped


def _dot(a, b):
    return jnp.dot(a, b, preferred_element_type=F32)


def _dot_nt(a, b):
    return lax.dot_general(a, b, (((1,), (1,)), ((), ())), preferred_element_type=F32)


def _dot_tn(a, b):
    return lax.dot_general(a, b, (((0,), (0,)), ((), ())), preferred_element_type=F32)


def _silu(x):
    return x * jax.nn.sigmoid(x)


def _ada_kernel(c_ref, w_ref, b_ref, o_ref):
    a = _silu(c_ref[...]).astype(BF16)
    o_ref[0] = _dot(a, w_ref[0].astype(BF16)) + b_ref[0]


def _ada(c8, ada_w, ada_b):
    depth, d, n = ada_w.shape
    tn = _tile(n, 1024, 128)
    return pl.pallas_call(
        _ada_kernel,
        grid=(depth, n // tn),
        in_specs=[
            pl.BlockSpec((8, d), lambda l, j: (0, 0)),
            pl.BlockSpec((1, d, tn), lambda l, j: (l, 0, j)),
            pl.BlockSpec((1, 1, tn), lambda l, j: (l, 0, j)),
        ],
        out_specs=pl.BlockSpec((1, 8, tn), lambda l, j: (l, 0, j)),
        out_shape=jax.ShapeDtypeStruct((depth, 8, n), F32),
        compiler_params=_cparams("parallel", "parallel"),
        name="ada",
    )(c8, ada_w, ada_b.reshape(depth, 1, n))


def _mod_row(i, tm, tp, ls):
    r0 = i * tm
    return jnp.where(r0 < tp, 0, 1 + (r0 - tp) // ls)


def _pack_halves(y):
    h = y.shape[1] // 2
    return _pack_pair(y[:, :h], y[:, h:])


def _pack_pair(lo, hi):
    def bf16_bits(v):
        return lax.bitcast_convert_type(v.astype(BF16).astype(F32), jnp.uint32)

    return (bf16_bits(lo) >> 16) | (bf16_bits(hi) & jnp.uint32(0xFFFF0000))


def _unpack_halves(w):
    lo = lax.bitcast_convert_type(w << 16, F32)
    hi = lax.bitcast_convert_type(w & jnp.uint32(0xFFFF0000), F32)
    return lo, hi


def _norm_mod_kernel(x_ref, g_ref, sh_ref, sc_ref, o_ref, *, packed):
    x = x_ref[...]
    y = x * lax.rsqrt(jnp.mean(x * x, axis=-1, keepdims=True) + EPS) * g_ref[...]
    y = y * (1.0 + sc_ref[0]) + sh_ref[0]
    o_ref[...] = _pack_halves(y) if packed else y.astype(o_ref.dtype)


def _norm_mod(x, g, shift, scale, tp, ls, packed):
    t, d = x.shape
    tm = _tile(math.gcd(tp, ls), 256)
    mod_spec = pl.BlockSpec((1, 1, d), lambda i: (_mod_row(i, tm, tp, ls), 0, 0))
    dout = d // 2 if packed else d
    return pl.pallas_call(
        functools.partial(_norm_mod_kernel, packed=packed),
        grid=(t // tm,),
        in_specs=[pl.BlockSpec((tm, d), lambda i: (i, 0)), pl.BlockSpec((1, d), lambda i: (0, 0)), mod_spec, mod_spec],
        out_specs=pl.BlockSpec((tm, dout), lambda i: (i, 0)),
        out_shape=jax.ShapeDtypeStruct((t, dout), jnp.uint32 if packed else BF16),
        compiler_params=_cparams("parallel"),
        name="norm_mod",
    )(x, g.reshape(1, d), shift, scale)


def _rmsnorm_kernel(x_ref, g_ref, o_ref):
    x = x_ref[...]
    o_ref[...] = x * lax.rsqrt(jnp.mean(x * x, axis=-1, keepdims=True) + EPS) * g_ref[...]


def _rmsnorm(x, g, row0, n):
    d = x.shape[1]
    tm = _tile(math.gcd(row0, n) if row0 else n, 256)
    return pl.pallas_call(
        _rmsnorm_kernel,
        grid=(n // tm,),
        in_specs=[pl.BlockSpec((tm, d), lambda i: (row0 // tm + i, 0)), pl.BlockSpec((1, d), lambda i: (0, 0))],
        out_specs=pl.BlockSpec((tm, d), lambda i: (i, 0)),
        out_shape=jax.ShapeDtypeStruct((n, d), F32),
        compiler_params=_cparams("parallel"),
        name="final_norm",
    )(x, g.reshape(1, d))


def _mm_kernel(*refs, n_pairs, epilogue):
    o_ref = refs[-1]
    acc = None
    for a_ref, b_ref in zip(refs[:n_pairs], refs[n_pairs : 2 * n_pairs]):
        part = _dot(a_ref[...].astype(BF16), b_ref[...])
        acc = part if acc is None else acc + part
    o_ref[...] = epilogue(acc, *refs[2 * n_pairs : -1]).astype(o_ref.dtype)


def _plain(acc):
    return acc


def _residual(acc, x_ref, g_ref):
    return x_ref[...] + g_ref[0] * acc


def _glu(acc, y_ref, b_ref):
    return y_ref[...] * jax.nn.sigmoid(acc + b_ref[...])


def _matmul(pairs, n, out_dtype, tm, tn, epilogue=_plain, extra=(), extra_specs=(), name="matmul"):
    t = pairs[0][0].shape[0]
    a_specs = [pl.BlockSpec((tm, a.shape[1]), lambda i, j: (i, 0)) for a, _ in pairs]
    b_specs = [pl.BlockSpec((b.shape[0], tn), lambda i, j: (0, j)) for _, b in pairs]
    return pl.pallas_call(
        functools.partial(_mm_kernel, n_pairs=len(pairs), epilogue=epilogue),
        grid=(t // tm, n // tn),
        in_specs=a_specs + b_specs + list(extra_specs),
        out_specs=pl.BlockSpec((tm, tn), lambda i, j: (i, j)),
        out_shape=jax.ShapeDtypeStruct((t, n), out_dtype),
        compiler_params=_cparams("parallel", "parallel"),
        name=name,
    )(*[a for a, _ in pairs], *[b for _, b in pairs], *extra)


def _matmul_residual(pairs, x, gate, tp, ls, name):
    t, d = x.shape
    tm = _tile(math.gcd(tp, ls), 1024)
    tn = _tile(d, 1024, 128)
    specs = [
        pl.BlockSpec((tm, tn), lambda i, j: (i, j)),
        pl.BlockSpec((1, 1, tn), lambda i, j: (_mod_row(i, tm, tp, ls), 0, j)),
    ]
    return _matmul(pairs, d, F32, tm, tn, _residual, (x, gate), specs, name)


def _split3(x):
    hi = x.astype(BF16)
    r1 = x - hi.astype(F32)
    mid = r1.astype(BF16)
    lo = (r1 - mid.astype(F32)).astype(BF16)
    return hi, mid, lo


def _hgrn_kernel(q_ref, f_ref, v_ref, lb_ref, s0_ref, o_ref, sfin_ref, st_ref, *, reverse, hb, c, dk):
    ci = pl.program_id(2)

    @pl.when(ci == 0)
    def _():
        for j in range(hb):
            st_ref[j] = s0_ref[0, j].T

    row = lax.broadcasted_iota(jnp.int32, (c, c), 0)
    col = lax.broadcasted_iota(jnp.int32, (c, c), 1)
    tri = jnp.where((col >= row) if reverse else (col <= row), 1.0, 0.0).astype(BF16)
    before = (col > row) if reverse else (col < row)
    diff = row ^ col
    trow = lax.broadcasted_iota(jnp.int32, (c, dk), 0)
    n_levels = int(math.log2(c))
    level = jnp.where(row == col, n_levels, -1)
    for h in range(n_levels):
        level = jnp.where(before & ((diff >> h) == 1), h, level)

    for j in range(hb):
        sl = slice(j * dk, (j + 1) * dk)
        q = _silu(q_ref[:, sl])
        lb = lb_ref[:, sl]
        f = lb + (1.0 - lb) * jax.nn.sigmoid(f_ref[:, sl])
        k = 1.0 - f
        v = v_ref[:, sl]
        vb = v.astype(BF16)
        hi, mid, lo = _split3(jnp.log(f))
        bcum = _dot(tri, hi) + _dot(tri, mid) + _dot(tri, lo)
        btot = bcum[0:1] if reverse else bcum[c - 1 : c]

        st = st_ref[j]
        o = _dot_nt((q * jnp.exp(bcum)).astype(BF16), st.astype(BF16))
        ke = (k * jnp.exp(btot - bcum)).astype(BF16)
        st_ref[j] = st * jnp.exp(btot) + _dot_tn(vb, ke)

        kb = k.astype(BF16)
        att = jnp.where(level == n_levels, _dot_nt(q.astype(BF16), kb), 0.0)
        f_prev = pltpu.roll(f, 1, 0)
        f_next = pltpu.roll(f, c - 1, 0)
        for h in range(n_levels):
            b = 1 << h
            if b == 1:
                x_l, y_l = (q * f).astype(BF16), kb
            else:
                if b == 2:
                    o4 = trow & 3
                    if reverse:
                        ex = jnp.where(o4 == 0, f * f_next, f)
                        ey = jnp.where(o4 == 3, f_prev, 1.0)
                    else:
                        ex = jnp.where(o4 == 3, f * f_prev, f)
                        ey = jnp.where(o4 == 0, f_next, 1.0)
                else:
                    b3 = bcum.reshape(c // (2 * b), 2 * b, dk)
                    edge = b if reverse else b - 1
                    ex = ey = jnp.exp(-jnp.abs(b3 - b3[:, edge : edge + 1, :])).reshape(c, dk)
                x_l, y_l = (q * ex).astype(BF16), (k * ey).astype(BF16)
            att = jnp.where(level == h, _dot_nt(x_l, y_l), att)
        o_ref[:, sl] = o + _dot(att.astype(BF16), vb)

    @pl.when(ci == pl.num_programs(2) - 1)
    def _():
        for j in range(hb):
            sfin_ref[0, j] = st_ref[j].T


def _hgrn(proj, lb, s0, row0, nb, l, d_a, f_section, reverse, prev=None):
    t = proj.shape[0]
    h, dk, dv = s0.shape[1:]
    c = min(HGRN_CHUNK, l)
    nc = l // c
    hb = next(k for k in (HGRN_HEADS_PER_STEP, 4, 2, 1) if h % k == 0)
    hw = hb * dk
    npb = d_a // hw

    def rows(b, ci):
        return row0 // c + b * nc + ((nc - 1 - ci) if reverse else ci)

    def in_spec(section):
        return pl.BlockSpec((c, hw), lambda b, g, ci: (rows(b, ci), section * npb + g))

    in_specs = [
        in_spec(0),
        in_spec(f_section),
        in_spec(3),
        pl.BlockSpec((1, hw), lambda b, g, ci: (0, g)),
        pl.BlockSpec((1, hb, dk, dv), lambda b, g, ci: (b, g, 0, 0)),
    ]
    args = [proj, proj, proj, lb.reshape(1, d_a), s0]
    kern = functools.partial(_hgrn_kernel, reverse=reverse, hb=hb, c=c, dk=dk)
    aliases = {}
    if prev is not None:
        in_specs.append(pl.BlockSpec(memory_space=pl.ANY))
        args.append(prev)
        aliases = {5: 0}
        kern = _drop_arg(kern, 5)
    return pl.pallas_call(
        kern,
        grid=(nb, npb, nc),
        in_specs=in_specs,
        out_specs=[
            pl.BlockSpec((c, hw), lambda b, g, ci: (rows(b, ci), g)),
            pl.BlockSpec((1, hb, dk, dv), lambda b, g, ci: (b, g, 0, 0)),
        ],
        out_shape=[jax.ShapeDtypeStruct((t, d_a), F32), jax.ShapeDtypeStruct((nb, h, dk, dv), F32)],
        scratch_shapes=[pltpu.VMEM((hb, dv, dk), F32)],
        input_output_aliases=aliases,
        compiler_params=_cparams("parallel", "parallel", "arbitrary"),
        name="hgrn_bwd" if reverse else "hgrn_fwd",
    )(*args)


def _hgrn_post_kernel(of_ref, ob_ref, g_ref, gn_ref, o_ref, *, nh, dv):
    for h in range(nh):
        sl = slice(h * dv, (h + 1) * dv)
        o = of_ref[:, sl] + ob_ref[:, sl]
        o = o * lax.rsqrt(jnp.mean(o * o, axis=-1, keepdims=True) + EPS) * gn_ref[:, sl]
        o_ref[:, sl] = (o * _silu(g_ref[:, sl])).astype(o_ref.dtype)


def _hgrn_post(o_fw, o_bw, proj, gnorm, nh, dv):
    t, d_a = o_fw.shape
    tm = _tile(t, 256)
    spec = pl.BlockSpec((tm, d_a), lambda i: (i, 0))
    return pl.pallas_call(
        functools.partial(_hgrn_post_kernel, nh=nh, dv=dv),
        grid=(t // tm,),
        in_specs=[spec, spec, pl.BlockSpec((tm, d_a), lambda i: (i, 4)), pl.BlockSpec((1, d_a), lambda i: (0, 0))],
        out_specs=spec,
        out_shape=jax.ShapeDtypeStruct((t, d_a), BF16),
        compiler_params=_cparams("parallel"),
        name="hgrn_post",
    )(o_fw, o_bw, proj, gnorm.reshape(1, d_a))


def _s5_tables(lam_re, lam_im, b_re, b_im, c_re, c_im, log_step, n_levels):
    hp = lax.Precision.HIGHEST
    tc = S5_CHUNK
    gb = S5_GB
    g, p, ch = b_re.shape[1:]
    nblk = g // gb
    lam = lax.complex(jnp.minimum(lam_re.astype(F32), -1e-4), lam_im.astype(F32))
    z = lam * jnp.exp(log_step.astype(F32))[..., None]
    b_bar = ((jnp.exp(z) - 1.0) / lam)[..., None] * lax.complex(b_re.astype(F32), b_im.astype(F32))
    cc = lax.complex(c_re.astype(F32), c_im.astype(F32))
    n = jnp.arange(tc + 1, dtype=F32)
    pw = jnp.exp(z[:, :, None, :] * n[None, None, :, None])

    kern = jnp.einsum("dgop,dglp,dgpi->dglio", cc, pw[:, :, :tc], b_bar, precision=hp).real
    lag_g = jnp.concatenate([kern[1][:, :0:-1], (kern[0][:, :1] + kern[1][:, :1]), kern[0][:, 1:]], axis=1)
    lagk = lag_g.reshape(nblk, gb, 2 * tc - 1, ch, ch).transpose(0, 2, 3, 1, 4).reshape(nblk, 2 * tc - 1, ch, gb * ch)

    inc_fw = (pw[0][:, tc - 1 - jnp.arange(tc), :, None] * b_bar[0][:, None]).transpose(0, 1, 3, 2)
    inc_bw = (pw[1][:, jnp.arange(tc), :, None] * b_bar[1][:, None]).transpose(0, 1, 3, 2)
    inc = jnp.concatenate([inc_fw.real, inc_fw.imag, inc_bw.real, inc_bw.imag], axis=-1)
    inct = inc.reshape(nblk, gb, tc, ch, 4 * p).transpose(0, 2, 3, 1, 4).reshape(nblk, tc, ch, gb * 4 * p)

    out_fw = cc[0][:, None] * pw[0][:, 1 + jnp.arange(tc), None, :]
    out_bw = cc[1][:, None] * pw[1][:, tc - jnp.arange(tc), None, :]
    out = jnp.concatenate([out_fw.real, -out_fw.imag, out_bw.real, -out_bw.imag], axis=-1)
    outt = out.reshape(nblk, gb, tc, ch, 4 * p).transpose(0, 2, 4, 1, 3).reshape(nblk, tc, 4 * p, gb * ch)

    steps = (tc * (2 ** jnp.arange(n_levels))).astype(F32)
    lp = jnp.exp(z[:, :, None, :] * steps[None, None, :, None])
    a, b = lp.real, lp.imag
    lev = jnp.stack(
        [jnp.concatenate([a, a], -1), jnp.concatenate([-b, b], -1), jnp.concatenate([b, -b], -1)], axis=3
    )
    lev = lev.transpose(1, 0, 2, 3, 4).reshape(g, 2 * n_levels * 3, 2 * p)
    pad = (-lev.shape[1]) % 8
    lev = jnp.pad(lev, ((0, 0), (0, pad), (0, 0)))
    return lagk, inct.astype(BF16), outt.astype(BF16), lev


def _s5_kernel(u_ref, lagk_ref, inct_ref, outt_ref, lev_ref, h0_ref, d_ref, *rest, nch, n_levels, max_levels, want_fin):
    if want_fin:
        o_ref, fin_ref, w1_ref, wt_ref, w2_ref, a_ref = rest
    else:
        o_ref, w1_ref, wt_ref, w2_ref, a_ref = rest
        fin_ref = None
    tc = S5_CHUNK
    gb = S5_GB
    lanes = u_ref.shape[1]
    ch = lanes // gb
    sw = lev_ref.shape[2]
    m = u_ref.shape[0] // tc
    nbk = m // nch
    b0 = pl.program_id(1) * nbk

    @pl.when(pl.program_id(1) == 0)
    def _():
        r1 = lax.broadcasted_iota(jnp.int32, (lanes, gb * 2 * sw), 0) // ch
        c1 = lax.broadcasted_iota(jnp.int32, (lanes, gb * 2 * sw), 1) // (2 * sw)
        for s in range(tc):
            w1_ref[s * lanes : (s + 1) * lanes, :] = jnp.where(r1 == c1, jnp.tile(inct_ref[0, s], (gb, 1)), 0.0)
        rd = lax.broadcasted_iota(jnp.int32, (lanes, lanes), 0) // ch
        cd = lax.broadcasted_iota(jnp.int32, (lanes, lanes), 1) // ch
        taps = [jnp.where(rd == cd, jnp.tile(lagk_ref[0, li], (gb, 1)), 0.0).astype(BF16) for li in range(2 * tc - 1)]
        for s in range(tc):
            for t in range(tc):
                wt_ref[s * lanes : (s + 1) * lanes, t * lanes : (t + 1) * lanes] = taps[t - s + tc - 1]
        r2 = lax.broadcasted_iota(jnp.int32, (gb * 2 * sw, lanes), 0) // (2 * sw)
        c2 = lax.broadcasted_iota(jnp.int32, (gb * 2 * sw, lanes), 1) // ch
        for t in range(tc):
            w2_ref[:, t * lanes : (t + 1) * lanes] = jnp.where(r2 == c2, jnp.tile(outt_ref[0, t], (gb, 1)), 0.0)

    xs_f32 = [u_ref[pl.ds(s, m, stride=tc), :] for s in range(tc)]
    xcat = jnp.concatenate([x.astype(BF16) for x in xs_f32], axis=1)
    inc = _dot(xcat, w1_ref[...])

    ridx = lax.broadcasted_iota(jnp.int32, (m, sw), 0)
    cidx = ridx % nch
    bidx = ridx // nch
    carry = []
    for g in range(gb):
        for d in range(2):
            fwd = d == 0
            x = inc[:, (2 * g + d) * sw : (2 * g + d + 1) * sw]
            xr = pltpu.roll(x, sw // 2, 1)
            h0 = jnp.zeros((m, sw), F32)
            h0r = jnp.zeros((m, sw), F32)
            for b in range(nbk):
                h0 = jnp.where(bidx == b, h0_ref[g, 2 * d, pl.ds(b0 + b, 1), :], h0)
                h0r = jnp.where(bidx == b, h0_ref[g, 2 * d + 1, pl.ds(b0 + b, 1), :], h0r)

            def const(j, which, d=d, g=g):
                r = (d * max_levels + j) * 3 + which
                return lev_ref[g, r : r + 1, :]

            entry = (cidx == 0) if fwd else (cidx == nch - 1)
            x = x + jnp.where(entry, h0 * const(0, 0) + h0r * const(0, 1), 0.0)
            xr = xr + jnp.where(entry, h0r * const(0, 0) + h0 * const(0, 2), 0.0)
            for j in range(n_levels):
                k = 1 << j
                keep = (cidx >= k) if fwd else (cidx < nch - k)
                shift = k if fwd else m - k
                sx = jnp.where(keep, pltpu.roll(x, shift, 0), 0.0)
                sxr = jnp.where(keep, pltpu.roll(xr, shift, 0), 0.0)
                x, xr = x + const(j, 0) * sx + const(j, 1) * sxr, xr + const(j, 0) * sxr + const(j, 2) * sx
            carry.append(jnp.where(entry, h0, pltpu.roll(x, 1 if fwd else m - 1, 0)).astype(BF16))
            if want_fin:
                a_ref[...] = x
                fin_ref[g, d] = a_ref[pl.ds(nch - 1 if fwd else 0, nbk, stride=nch), :]
    y = _dot(xcat, wt_ref[...]) + _dot(jnp.concatenate(carry, axis=1), w2_ref[...])
    for t in range(tc):
        o_ref[pl.ds(t, m, stride=tc), :] = jax.nn.gelu(y[:, t * lanes : (t + 1) * lanes] + d_ref[...] * xs_f32[t])


def _s5(proj, u_col0, tables, h0, s5_d, row0, nb, l, max_levels, want_fin, prev=None):
    lagk, inct, outt, lev = tables
    t = proj.shape[0]
    tc, gb = S5_CHUNK, S5_GB
    nblk = lagk.shape[0]
    lanes = lagk.shape[3]
    g = nblk * gb
    sw = lev.shape[2]
    nch = l // tc
    n_levels = int(math.log2(nch))
    nbk = nb
    if not want_fin:
        nbk = max(k for k in range(1, nb + 1) if nb % k == 0 and row0 % (k * l) == 0 and (k == 1 or k * nch <= 256))
    m = nbk * nch
    rows = m * tc
    in_specs = [
        pl.BlockSpec((rows, lanes), lambda i, b: (row0 // rows + b, u_col0 // lanes + i)),
        pl.BlockSpec((1,) + lagk.shape[1:], lambda i, b: (i, 0, 0, 0)),
        pl.BlockSpec((1,) + inct.shape[1:], lambda i, b: (i, 0, 0, 0)),
        pl.BlockSpec((1,) + outt.shape[1:], lambda i, b: (i, 0, 0, 0)),
        pl.BlockSpec((gb,) + lev.shape[1:], lambda i, b: (i, 0, 0)),
        pl.BlockSpec((gb, 4, nb, sw), lambda i, b: (i, 0, 0, 0)),
        pl.BlockSpec((1, lanes), lambda i, b: (0, i)),
    ]
    args = [proj, lagk, inct, outt, lev, h0, s5_d.reshape(1, g * (lanes // gb))]
    out_specs = [pl.BlockSpec((rows, lanes), lambda i, b: (row0 // rows + b, i))]
    out_shape = [jax.ShapeDtypeStruct((t, g * (lanes // gb)), F32)]
    if want_fin:
        out_specs.append(pl.BlockSpec((gb, 2, nbk, sw), lambda i, b: (i, 0, b, 0)))
        out_shape.append(jax.ShapeDtypeStruct((g, 2, nb, sw), F32))
    kern = functools.partial(_s5_kernel, nch=nch, n_levels=n_levels, max_levels=max_levels, want_fin=want_fin)
    aliases = {}
    if prev is not None:
        in_specs.append(pl.BlockSpec(memory_space=pl.ANY))
        args.append(prev)
        aliases = {7: 0}
        kern = _drop_arg(kern, 7)
    out = pl.pallas_call(
        kern,
        grid=(nblk, nb // nbk),
        in_specs=in_specs,
        out_specs=out_specs,
        out_shape=out_shape,
        scratch_shapes=[
            pltpu.VMEM((tc * lanes, gb * 2 * sw), BF16),
            pltpu.VMEM((tc * lanes, tc * lanes), BF16),
            pltpu.VMEM((gb * 2 * sw, tc * lanes), BF16),
            pltpu.VMEM((m, sw), F32),
        ],
        input_output_aliases=aliases,
        compiler_params=_cparams("parallel", "arbitrary"),
        name="s5",
    )(*args)
    return (out[0], out[1]) if want_fin else (out[0], None)


def _dft_cos_sin(num, den, scale):
    ang = (2.0 * math.pi / den) * (num % den).astype(F32)
    return (jnp.cos(ang) * scale).astype(BF16), (jnp.sin(ang) * scale).astype(BF16)


def _fnet_pos_kernel(c_ref, s_ref, uv_ref, o_ref, *, gd):
    uv = uv_ref[...]
    o_ref[...] = (_dot(c_ref[...], uv[:, :gd]) - _dot(s_ref[...], uv[:, gd:])).astype(o_ref.dtype)


def _fnet_pos(cp, sp, uv, row0, nb, l, gd, prev=None):
    t = uv.shape[0]
    ng = uv.shape[1] // (2 * gd)
    const = pl.BlockSpec((l, l), lambda b, g: (0, 0))
    in_specs = [const, const, pl.BlockSpec((l, 2 * gd), lambda b, g: (row0 // l + b, g))]
    args = [cp, sp, uv]
    kern = functools.partial(_fnet_pos_kernel, gd=gd)
    aliases = {}
    if prev is not None:
        in_specs.append(pl.BlockSpec(memory_space=pl.ANY))
        args.append(prev)
        aliases = {3: 0}
        kern = _drop_arg(kern, 3)
    return pl.pallas_call(
        kern,
        grid=(nb, ng),
        in_specs=in_specs,
        out_specs=pl.BlockSpec((l, gd), lambda b, g: (row0 // l + b, g)),
        out_shape=jax.ShapeDtypeStruct((t, ng * gd), BF16),
        input_output_aliases=aliases,
        compiler_params=_cparams("parallel", "parallel"),
        name="fnet_pos",
    )(*args)


def _fourier_mix(xn, tp, lp, ls):
    t, d = xn.shape
    gd = d // FNET_GROUPS
    kk = jnp.arange(gd)
    cc, sc = _dft_cos_sin(kk[:, None] * kk[None, :], gd, gd**-0.5)
    cs = jnp.concatenate([cc, sc], axis=1)
    tm = _tile(t, 1024)
    uv = pl.pallas_call(
        functools.partial(_mm_kernel, n_pairs=1, epilogue=_plain),
        grid=(t // tm, FNET_GROUPS),
        in_specs=[pl.BlockSpec((tm, gd), lambda i, g: (i, g)), pl.BlockSpec((gd, 2 * gd), lambda i, g: (0, 0))],
        out_specs=pl.BlockSpec((tm, 2 * gd), lambda i, g: (i, g)),
        out_shape=jax.ShapeDtypeStruct((t, 2 * d), BF16),
        compiler_params=_cparams("parallel", "parallel"),
        name="fnet_chan",
    )(xn, cs)
    pp = jnp.arange(lp)
    cp_p, sp_p = _dft_cos_sin(pp[:, None] * pp[None, :], lp, lp**-0.5)
    ps = jnp.arange(ls)
    r, c = ps // GRID_W, ps % GRID_W
    rows = ls // GRID_W
    lcm = rows * GRID_W // math.gcd(rows, GRID_W)
    num = (lcm // rows) * (r[:, None] * r[None, :]) + (lcm // GRID_W) * (c[:, None] * c[None, :])
    cp_s, sp_s = _dft_cos_sin(num, lcm, ls**-0.5)
    z = _fnet_pos(cp_p, sp_p, uv, 0, tp // lp, lp, gd)
    return _fnet_pos(cp_s, sp_s, uv, tp, (t - tp) // ls, ls, gd, prev=z)


def _router_kernel(x_ref, w_ref, b_ref, o_ref, *, ng, npg):
    x_lo, x_hi = _unpack_halves(x_ref[...])
    h = x_lo.shape[1]
    logits = _dot(x_lo.astype(BF16), w_ref[:h, :]) + _dot(x_hi.astype(BF16), w_ref[h:, :]) + b_ref[...]
    lane = lax.broadcasted_iota(jnp.int32, logits.shape, 1).astype(F32)
    big = 1e9

    def first_max(vals):
        m = jnp.max(vals, axis=-1, keepdims=True)
        return m, jnp.min(jnp.where(vals == m, lane, big), axis=-1, keepdims=True)

    gl = jnp.where(lane < ng, logits, NEG)
    gmax, gidx = first_max(gl)
    gprob = 1.0 / jnp.sum(jnp.exp(gl - gmax), axis=-1, keepdims=True)
    lo = ng + gidx * npg
    el = jnp.where((lane >= lo) & (lane < lo + npg), logits, NEG)
    m1, i1 = first_max(el)
    m2, i2 = first_max(jnp.where(lane == i1, NEG, el))
    e = jnp.exp(m2 - m1)
    w1 = gprob / (1.0 + e)
    out = jnp.where(lane == 0, i1 - ng, jnp.where(lane == 1, i2 - ng, jnp.where(lane == 2, w1, jnp.where(lane == 3, w1 * e, 0.0))))
    o_ref[...] = out


def _router(xn, wr_g, br_g, wr_e, br_e):
    t = xn.shape[0]
    d = wr_g.shape[0]
    ng, ne = wr_g.shape[1], wr_e.shape[1]
    w = jnp.zeros((d, 128), F32).at[:, :ng].set(wr_g).at[:, ng : ng + ne].set(wr_e).astype(BF16)
    b = jnp.zeros((1, 128), F32).at[0, :ng].set(br_g).at[0, ng : ng + ne].set(br_e)
    tm = _tile(t, 512)
    return pl.pallas_call(
        functools.partial(_router_kernel, ng=ng, npg=ne // ng),
        grid=(t // tm,),
        in_specs=[pl.BlockSpec((tm, d // 2), lambda i: (i, 0)), pl.BlockSpec((d, 128), lambda i: (0, 0)), pl.BlockSpec((1, 128), lambda i: (0, 0))],
        out_specs=pl.BlockSpec((tm, 128), lambda i: (i, 0)),
        out_shape=jax.ShapeDtypeStruct((t, 128), F32),
        compiler_params=_cparams("parallel"),
        name="router",
    )(xn, w, b)


def _moe_plan(rout, n_exp, tm, nt):
    t = rout.shape[0]
    flat = rout[:, :2].astype(jnp.int32).T.reshape(-1)
    oh = (flat[:, None] == jnp.arange(n_exp)[None, :]).astype(jnp.int32)
    blk = _tile(2 * t, 128)
    c1 = jnp.cumsum(oh.reshape(-1, blk, n_exp), axis=1)
    tot = c1[:, -1]
    cs = (c1 + (jnp.cumsum(tot, axis=0) - tot)[:, None]).reshape(2 * t, n_exp)
    rank = jnp.sum(oh * cs, axis=1) - 1
    counts = cs[-1]
    tiles = (counts + tm - 1) // tm
    tile_end = jnp.cumsum(tiles)
    pos = (tile_end - tiles)[flat] * tm + rank
    row_token = jnp.zeros((nt * tm,), jnp.int32).at[pos].set(jnp.tile(jnp.arange(t, dtype=jnp.int32), 2))
    tile_ids = jnp.arange(nt)
    tile_expert = jnp.minimum(jnp.searchsorted(tile_end, tile_ids, side="right"), n_exp - 1).astype(jnp.int32)
    tile_valid = (tile_ids < tile_end[-1]).astype(jnp.int32)
    prev_expert = jnp.concatenate([jnp.full((1,), -1, jnp.int32), tile_expert[:-1]])
    tile_first = (tile_valid * (tile_expert != prev_expert)).astype(jnp.int32)
    owner = jnp.where(tiles > 0, jnp.arange(n_exp), n_exp)
    nxt = jnp.concatenate([lax.cummin(owner[::-1])[::-1][1:], jnp.full((1,), n_exp)])
    tile_next = jnp.where(nxt < n_exp, nxt, -1)[tile_expert].astype(jnp.int32)
    return row_token, pos.astype(jnp.int32), tile_expert, tile_valid, tile_first, tile_next


def _moe_kernel(te_ref, tv_ref, tf_ref, tn_ref, rt_ref, x_hbm, wg_hbm, wu_hbm, wd_hbm, y_hbm,
                xbuf, stg_g, stg_u, stg_d, wb_g, wb_u, wb_d, obuf, sem_x, sem_w, sem_o, *, tm, layer):
    i = pl.program_id(0)
    nt = pl.num_programs(0)
    slot = i % 2
    d, f = stg_g.shape

    def issue_rows(tile, s):
        def body(r, carry):
            tok = rt_ref[tile * tm + r]
            pltpu.make_async_copy(x_hbm.at[pl.ds(tok, 1)], xbuf.at[s, pl.ds(r, 1)], sem_x.at[s]).start()
            return carry

        lax.fori_loop(0, tm, body, 0, unroll=8)

    def weight_copies(e):
        return (
            pltpu.make_async_copy(wg_hbm.at[layer, e], stg_g, sem_w.at[0]),
            pltpu.make_async_copy(wu_hbm.at[layer, e], stg_u, sem_w.at[1]),
            pltpu.make_async_copy(wd_hbm.at[layer, e], stg_d, sem_w.at[2]),
        )

    def out_copy(tile):
        return pltpu.make_async_copy(obuf, y_hbm.at[pl.ds(pl.multiple_of(tile * tm, tm), tm)], sem_o.at[0])

    def cast(src, dst, ck):
        def body(k, carry):
            r = pl.multiple_of(k * ck, ck)
            dst[pl.ds(r, ck), :] = src[pl.ds(r, ck), :].astype(BF16)
            return carry

        lax.fori_loop(0, src.shape[0] // ck, body, 0)

    @pl.when(i == 0)
    def _():
        issue_rows(0, 0)
        for cp in weight_copies(te_ref[0]):
            cp.start()

    nxt = jnp.minimum(i + 1, nt - 1)

    @pl.when(jnp.logical_and(i + 1 < nt, tv_ref[nxt] == 1))
    def _():
        issue_rows(i + 1, 1 - slot)

    prev_valid = jnp.logical_and(i > 0, tv_ref[jnp.maximum(i - 1, 0)] == 1)

    @pl.when(prev_valid)
    def _():
        out_copy(i - 1).wait()

    @pl.when(tv_ref[i] == 1)
    def _():
        @pl.when(tf_ref[i] == 1)
        def _():
            for cp in weight_copies(0):
                cp.wait()
            cast(stg_g, wb_g, min(256, d))
            cast(stg_u, wb_u, min(256, d))
            cast(stg_d, wb_d, min(32, f))

            @pl.when(tn_ref[i] >= 0)
            def _():
                for cp in weight_copies(tn_ref[i]):
                    cp.start()

        pltpu.make_async_copy(x_hbm.at[pl.ds(0, tm)], xbuf.at[slot], sem_x.at[slot]).wait()
        x_lo, x_hi = (v.astype(BF16) for v in _unpack_halves(xbuf[slot]))
        dh = d // 2
        hg = _dot(x_lo, wb_g[:dh, :]) + _dot(x_hi, wb_g[dh:, :])
        hu = _dot(x_lo, wb_u[:dh, :]) + _dot(x_hi, wb_u[dh:, :])
        h = (_silu(hg) * hu).astype(BF16)
        obuf[...] = _pack_pair(_dot(h, wb_d[:, :dh]), _dot(h, wb_d[:, dh:]))
        out_copy(i).start()

        @pl.when(i == nt - 1)
        def _():
            out_copy(i).wait()


def _combine_kernel(pos_ref, y_hbm, x_ref, r_ref, g_ref, o_ref, buf, sem, *, tm, t):
    i = pl.program_id(0)
    n = pl.num_programs(0)
    slot = i % 2

    def issue(tile, s):
        def body(r, carry):
            for k in range(2):
                p = pos_ref[k * t + tile * tm + r]
                pltpu.make_async_copy(y_hbm.at[pl.ds(p, 1)], buf.at[s, k, pl.ds(r, 1)], sem.at[s]).start()
            return carry

        lax.fori_loop(0, tm, body, 0, unroll=4)

    @pl.when(i == 0)
    def _():
        issue(0, 0)

    @pl.when(i + 1 < n)
    def _():
        issue(i + 1, 1 - slot)

    for k in range(2):
        pltpu.make_async_copy(y_hbm.at[pl.ds(0, tm)], buf.at[slot, k], sem.at[slot]).wait()
    w = r_ref[...]
    w1, w2 = w[:, 2:3], w[:, 3:4]
    a_lo, a_hi = _unpack_halves(buf[slot, 0])
    b_lo, b_hi = _unpack_halves(buf[slot, 1])
    dh = a_lo.shape[1]
    o_ref[:, :dh] = x_ref[:, :dh] + g_ref[0, :, :dh] * (w1 * a_lo + w2 * b_lo)
    o_ref[:, dh:] = x_ref[:, dh:] + g_ref[0, :, dh:] * (w1 * a_hi + w2 * b_hi)


def _moe(xn, x, gate, params, layer, tp, ls):
    wr_g, br_g, wr_e, br_e, w_gate, w_up, w_down = params
    t, d = x.shape
    n_exp, _, f = w_gate.shape[1:]
    rout = _router(xn, wr_g, br_g, wr_e, br_e)
    tm = 256 if t >= 4096 else 64
    nt = (2 * t + tm - 1) // tm + n_exp
    row_token, pos, tile_expert, tile_valid, tile_first, tile_next = _moe_plan(rout, n_exp, tm, nt)
    hbm = pl.BlockSpec(memory_space=pl.ANY)
    y = pl.pallas_call(
        functools.partial(_moe_kernel, tm=tm, layer=layer),
        grid_spec=pltpu.PrefetchScalarGridSpec(
            num_scalar_prefetch=5,
            grid=(nt,),
            in_specs=[hbm, hbm, hbm, hbm],
            out_specs=hbm,
            scratch_shapes=[
                pltpu.VMEM((2, tm, d // 2), jnp.uint32),
                pltpu.VMEM((d, f), F32),
                pltpu.VMEM((d, f), F32),
                pltpu.VMEM((f, d), F32),
                pltpu.VMEM((d, f), BF16),
                pltpu.VMEM((d, f), BF16),
                pltpu.VMEM((f, d), BF16),
                pltpu.VMEM((tm, d // 2), jnp.uint32),
                pltpu.SemaphoreType.DMA((2,)),
                pltpu.SemaphoreType.DMA((3,)),
                pltpu.SemaphoreType.DMA((1,)),
            ],
        ),
        out_shape=jax.ShapeDtypeStruct((nt * tm, d // 2), jnp.uint32),
        compiler_params=_cparams("arbitrary"),
        name="moe_experts",
    )(tile_expert, tile_valid, tile_first, tile_next, row_token, xn, w_gate, w_up, w_down)

    tc = _tile(math.gcd(tp, ls), 128)
    return pl.pallas_call(
        functools.partial(_combine_kernel, tm=tc, t=t),
        grid_spec=pltpu.PrefetchScalarGridSpec(
            num_scalar_prefetch=1,
            grid=(t // tc,),
            in_specs=[
                pl.BlockSpec(memory_space=pl.ANY),
                pl.BlockSpec((tc, d), lambda i, p: (i, 0)),
                pl.BlockSpec((tc, 128), lambda i, p: (i, 0)),
                pl.BlockSpec((1, 1, d), lambda i, p: (_mod_row(i, tc, tp, ls), 0, 0)),
            ],
            out_specs=pl.BlockSpec((tc, d), lambda i, p: (i, 0)),
            scratch_shapes=[pltpu.VMEM((2, 2, tc, d // 2), jnp.uint32), pltpu.SemaphoreType.DMA((2,))],
        ),
        out_shape=jax.ShapeDtypeStruct((t, d), F32),
        compiler_params=_cparams("arbitrary"),
        name="moe_combine",
    )(pos, y, x, rout, gate)


def kernel(x_prompt, x_sample, state_hgrn, state_s5_re, state_s5_im, c, c_ctx, ada_w, ada_b, norm_mix, norm_ffn, norm_final, ab_w_in, ab_w_out, hgrn_lb_theta, hgrn_gnorm, s5_lam_re, s5_lam_im, s5_b_re, s5_b_im, s5_c_re, s5_c_im, s5_log_step, s5_d, s5_glu_w, s5_glu_b, fnet_w_out, moe_wr_group, moe_br_group, moe_wr_expert, moe_br_expert, moe_w_gate, moe_w_up, moe_w_down):
    bp, lp, d = x_prompt.shape
    bs, ls, _ = x_sample.shape
    tp, ts = bp * lp, bs * ls
    depth = ada_w.shape[0]
    nh, dk, dv = state_hgrn.shape[3:]
    d_a = nh * dk
    n_grp, s5_p = state_s5_re.shape[3:]
    d_b = s5_d.shape[1]
    nmod = 1 + bs

    x = jnp.concatenate([x_prompt.reshape(tp, d), x_sample.reshape(ts, d)], axis=0)
    c8 = jnp.zeros((8, d), F32).at[0].set(c_ctx).at[1:nmod].set(c)
    mod_all = _ada(c8, ada_w, ada_b)
    lb_all = jnp.cumsum(jax.nn.softmax(hgrn_lb_theta.astype(F32), axis=1), axis=1)
    max_levels = int(math.log2(max(lp, ls) // S5_CHUNK))

    new_h, new_re, new_im = [], [], []
    i_ab = i_c = 0
    for l in range(depth):
        mod = mod_all[l, :nmod].reshape(nmod, 6, 1, d).transpose(1, 0, 2, 3)
        xn = _norm_mod(x, norm_mix[l], mod[0], mod[1], tp, ls, False)
        if l % 2 == 0:
            tm = _tile(math.gcd(tp, ls), 1024)
            n_in = ab_w_in.shape[2]
            proj = _matmul([(xn, ab_w_in[i_ab].astype(BF16))], n_in, F32, tm, _tile(n_in, 1024, 128), name="proj_in")
            tables = _s5_tables(s5_lam_re[i_ab], s5_lam_im[i_ab], s5_b_re[i_ab], s5_b_im[i_ab], s5_c_re[i_ab], s5_c_im[i_ab], s5_log_step[i_ab], max_levels)
            o_dirs, yg, h_fin, x_fin = [None, None], None, None, None
            for row0, nb, ln, h0, x0 in (
                (0, bp, lp, jnp.zeros((bp, 2, nh, dk, dv), F32), jnp.zeros((bp, 2, n_grp, 2 * s5_p), F32)),
                (tp, bs, ls, state_hgrn[:, i_ab].astype(F32), jnp.concatenate([state_s5_re[:, i_ab], state_s5_im[:, i_ab]], axis=-1).astype(F32)),
            ):
                fins = []
                for dr in range(2):
                    o_dirs[dr], s_fin = _hgrn(proj, lb_all[dr, i_ab], h0[:, dr], row0, nb, ln, d_a, 1 + dr, dr == 1, prev=o_dirs[dr])
                    fins.append(s_fin)
                x0s = jnp.concatenate([x0[..., s5_p:], x0[..., :s5_p]], axis=-1)
                h0_s5 = jnp.stack([x0[:, 0], x0s[:, 0], x0[:, 1], x0s[:, 1]], axis=0).transpose(2, 0, 1, 3)
                yg, fin_s5 = _s5(proj, 5 * d_a, tables, h0_s5, s5_d[i_ab], row0, nb, ln, max_levels, row0 == 0, prev=yg)
                if row0 == 0:
                    h_fin = jnp.stack(fins, axis=1)
                    x_fin = fin_s5.transpose(2, 1, 0, 3)
            o_a = _hgrn_post(o_dirs[0], o_dirs[1], proj, hgrn_gnorm[i_ab], nh, dv)
            tn = _tile(d_b, 1024, 128)
            y_glu = _matmul(
                [(yg, s5_glu_w[i_ab].astype(BF16))], d_b, BF16, tm, tn, _glu, (yg, s5_glu_b[i_ab].reshape(1, d_b)),
                (pl.BlockSpec((tm, tn), lambda i, j: (i, j)), pl.BlockSpec((1, tn), lambda i, j: (0, j))), "s5_glu",
            )
            w_out = ab_w_out[i_ab].astype(BF16)
            x = _matmul_residual([(o_a, w_out[:d_a]), (y_glu, w_out[d_a:])], x, mod[2], tp, ls, "mix_out")
            new_h.append(h_fin)
            new_re.append(x_fin[..., :s5_p])
            new_im.append(x_fin[..., s5_p:])
            i_ab += 1
        else:
            z = _fourier_mix(xn, tp, lp, ls)
            x = _matmul_residual([(z, fnet_w_out[i_c].astype(BF16))], x, mod[2], tp, ls, "fnet_out")
            i_c += 1
        xn = _norm_mod(x, norm_ffn[l], mod[3], mod[4], tp, ls, True)
        moe_params = (moe_wr_group[l], moe_br_group[l], moe_wr_expert[l], moe_br_expert[l], moe_w_gate, moe_w_up, moe_w_down)
        x = _moe(xn, x, mod[5], moe_params, l, tp, ls)
    return (
        _rmsnorm(x, norm_final, 0, tp).reshape(bp, lp, d),
        _rmsnorm(x, norm_final, tp, ts).reshape(bs, ls, d),
        jnp.stack(new_h, axis=1),
        jnp.stack(new_re, axis=1),
        jnp.stack(new_im, axis=1),
    )
```

```python
import functools
import math

import jax
import jax.numpy as jnp
import numpy as np
from jax import lax
from jax.experimental import pallas as pl
from jax.experimental.pallas import tpu as pltpu

F32 = jnp.float32
BF16 = jnp.bfloat16
EPS = 1e-6
NEG = -1e30
V7X_VMEM_LIMIT_BYTES = 56 * 1024 * 1024
FNET_GROUPS = 8
GRID_W = 64
HGRN_CHUNK = 128
HGRN_HEADS_PER_STEP = 8
S5_CHUNK = 16
S5_GB = 8


def _cparams(*sem):
    return pltpu.CompilerParams(dimension_semantics=sem, vmem_limit_bytes=V7X_VMEM_LIMIT_BYTES)


def _tile(n, pref, mult=8):
    if n <= pref:
        return n
    for t in range(pref - pref % mult, 0, -mult):
        if n % t == 0:
            return t
    raise ValueError(f"no tile for {n}")


def _drop_arg(body, index):
    def wrapped(*refs):
        return body(*refs[:index], *refs[index + 1 :])

    return wrapped


def _dot(a, b):
    return jnp.dot(a, b, preferred_element_type=F32)


def _dot_nt(a, b):
    return lax.dot_general(a, b, (((1,), (1,)), ((), ())), preferred_element_type=F32)


def _dot_tn(a, b):
    return lax.dot_general(a, b, (((0,), (0,)), ((), ())), preferred_element_type=F32)


def _silu(x):
    return x * jax.nn.sigmoid(x)


def _ada_kernel(c_ref, w_ref, b_ref, o_ref):
    a = _silu(c_ref[...]).astype(BF16)
    o_ref[0] = _dot(a, w_ref[0].astype(BF16)) + b_ref[0]


def _ada(c8, ada_w, ada_b):
    depth, d, n = ada_w.shape
    tn = _tile(n, 1024, 128)
    return pl.pallas_call(
        _ada_kernel,
        grid=(depth, n // tn),
        in_specs=[
            pl.BlockSpec((8, d), lambda l, j: (0, 0)),
            pl.BlockSpec((1, d, tn), lambda l, j: (l, 0, j)),
            pl.BlockSpec((1, 1, tn), lambda l, j: (l, 0, j)),
        ],
        out_specs=pl.BlockSpec((1, 8, tn), lambda l, j: (l, 0, j)),
        out_shape=jax.ShapeDtypeStruct((depth, 8, n), F32),
        compiler_params=_cparams("parallel", "parallel"),
        name="ada",
    )(c8, ada_w, ada_b.reshape(depth, 1, n))


def _mod_row(i, tm, tp, ls):
    r0 = i * tm
    return jnp.where(r0 < tp, 0, 1 + (r0 - tp) // ls)


def _pack_halves(y):
    h = y.shape[1] // 2
    return _pack_pair(y[:, :h], y[:, h:])


def _pack_pair(lo, hi):
    def bf16_bits(v):
        return lax.bitcast_convert_type(v.astype(BF16).astype(F32), jnp.uint32)

    return (bf16_bits(lo) >> 16) | (bf16_bits(hi) & jnp.uint32(0xFFFF0000))


def _unpack_halves(w):
    lo = lax.bitcast_convert_type(w << 16, F32)
    hi = lax.bitcast_convert_type(w & jnp.uint32(0xFFFF0000), F32)
    return lo, hi


LANES = 128


def _store_slabs(ref, lead, words):
    m, w = words.shape
    s = w // LANES
    for j in range(s):
        ref[(*lead, pl.ds(j, m, stride=s), slice(None))] = words[:, j * LANES : (j + 1) * LANES]


def _load_slabs(ref, lead, m, s):
    return jnp.concatenate([ref[(*lead, pl.ds(j, m, stride=s), slice(None))] for j in range(s)], axis=1)


def _norm_mod_kernel(x_ref, g_ref, sh_ref, sc_ref, o_ref, *, packed):
    x = x_ref[...]
    y = x * lax.rsqrt(jnp.mean(x * x, axis=-1, keepdims=True) + EPS) * g_ref[...]
    y = y * (1.0 + sc_ref[0]) + sh_ref[0]
    if packed:
        _store_slabs(o_ref, (), _pack_halves(y))
    else:
        o_ref[...] = y.astype(o_ref.dtype)


def _norm_mod(x, g, shift, scale, tp, ls, packed):
    t, d = x.shape
    tm = _tile(math.gcd(tp, ls), 256)
    mod_spec = pl.BlockSpec((1, 1, d), lambda i: (_mod_row(i, tm, tp, ls), 0, 0))
    s = d // 2 // LANES
    out_spec = pl.BlockSpec((tm * s, LANES), lambda i: (i, 0)) if packed else pl.BlockSpec((tm, d), lambda i: (i, 0))
    out_shape = jax.ShapeDtypeStruct((t * s, LANES), jnp.uint32) if packed else jax.ShapeDtypeStruct((t, d), BF16)
    return pl.pallas_call(
        functools.partial(_norm_mod_kernel, packed=packed),
        grid=(t // tm,),
        in_specs=[pl.BlockSpec((tm, d), lambda i: (i, 0)), pl.BlockSpec((1, d), lambda i: (0, 0)), mod_spec, mod_spec],
        out_specs=out_spec,
        out_shape=out_shape,
        compiler_params=_cparams("parallel"),
        name="norm_mod",
    )(x, g.reshape(1, d), shift, scale)


def _rmsnorm_kernel(x_ref, g_ref, o_ref):
    x = x_ref[...]
    o_ref[...] = x * lax.rsqrt(jnp.mean(x * x, axis=-1, keepdims=True) + EPS) * g_ref[...]


def _rmsnorm(x, g, row0, n):
    d = x.shape[1]
    tm = _tile(math.gcd(row0, n) if row0 else n, 256)
    return pl.pallas_call(
        _rmsnorm_kernel,
        grid=(n // tm,),
        in_specs=[pl.BlockSpec((tm, d), lambda i: (row0 // tm + i, 0)), pl.BlockSpec((1, d), lambda i: (0, 0))],
        out_specs=pl.BlockSpec((tm, d), lambda i: (i, 0)),
        out_shape=jax.ShapeDtypeStruct((n, d), F32),
        compiler_params=_cparams("parallel"),
        name="final_norm",
    )(x, g.reshape(1, d))


def _mm_kernel(*refs, n_pairs, epilogue):
    o_ref = refs[-1]
    acc = None
    for a_ref, b_ref in zip(refs[:n_pairs], refs[n_pairs : 2 * n_pairs]):
        part = _dot(a_ref[...].astype(BF16), b_ref[...])
        acc = part if acc is None else acc + part
    o_ref[...] = epilogue(acc, *refs[2 * n_pairs : -1]).astype(o_ref.dtype)


def _plain(acc):
    return acc


def _residual(acc, x_ref, g_ref):
    return x_ref[...] + g_ref[0] * acc


def _glu(acc, y_ref, b_ref):
    return y_ref[...] * jax.nn.sigmoid(acc + b_ref[...])


def _matmul(pairs, n, out_dtype, tm, tn, epilogue=_plain, extra=(), extra_specs=(), name="matmul"):
    t = pairs[0][0].shape[0]
    a_specs = [pl.BlockSpec((tm, a.shape[1]), lambda i, j: (i, 0)) for a, _ in pairs]
    b_specs = [pl.BlockSpec((b.shape[0], tn), lambda i, j: (0, j)) for _, b in pairs]
    return pl.pallas_call(
        functools.partial(_mm_kernel, n_pairs=len(pairs), epilogue=epilogue),
        grid=(t // tm, n // tn),
        in_specs=a_specs + b_specs + list(extra_specs),
        out_specs=pl.BlockSpec((tm, tn), lambda i, j: (i, j)),
        out_shape=jax.ShapeDtypeStruct((t, n), out_dtype),
        compiler_params=_cparams("parallel", "parallel"),
        name=name,
    )(*[a for a, _ in pairs], *[b for _, b in pairs], *extra)


def _matmul_residual(pairs, x, gate, tp, ls, name):
    t, d = x.shape
    tm = _tile(math.gcd(tp, ls), 1024)
    tn = _tile(d, 1024, 128)
    specs = [
        pl.BlockSpec((tm, tn), lambda i, j: (i, j)),
        pl.BlockSpec((1, 1, tn), lambda i, j: (_mod_row(i, tm, tp, ls), 0, j)),
    ]
    return _matmul(pairs, d, F32, tm, tn, _residual, (x, gate), specs, name)


def _split3(x):
    hi = x.astype(BF16)
    r1 = x - hi.astype(F32)
    mid = r1.astype(BF16)
    lo = (r1 - mid.astype(F32)).astype(BF16)
    return hi, mid, lo


def _hgrn_kernel(q_ref, f_ref, v_ref, lb_ref, s0_ref, o_ref, sfin_ref, st_ref, *, reverse, hb, c, dk):
    ci = pl.program_id(2)

    @pl.when(ci == 0)
    def _():
        for j in range(hb):
            st_ref[j] = s0_ref[0, j].T

    row = lax.broadcasted_iota(jnp.int32, (c, c), 0)
    col = lax.broadcasted_iota(jnp.int32, (c, c), 1)
    tri = jnp.where((col >= row) if reverse else (col <= row), 1.0, 0.0).astype(BF16)
    before = (col > row) if reverse else (col < row)
    diff = row ^ col
    trow = lax.broadcasted_iota(jnp.int32, (c, dk), 0)
    n_levels = int(math.log2(c))
    level = jnp.where(row == col, n_levels, -1)
    for h in range(n_levels):
        level = jnp.where(before & ((diff >> h) == 1), h, level)

    for j in range(hb):
        sl = slice(j * dk, (j + 1) * dk)
        q = _silu(q_ref[:, sl])
        lb = lb_ref[:, sl]
        f = lb + (1.0 - lb) * jax.nn.sigmoid(f_ref[:, sl])
        k = 1.0 - f
        v = v_ref[:, sl]
        vb = v.astype(BF16)
        hi, mid, lo = _split3(jnp.log(f))
        bcum = _dot(tri, hi) + _dot(tri, mid) + _dot(tri, lo)
        btot = bcum[0:1] if reverse else bcum[c - 1 : c]

        st = st_ref[j]
        o = _dot_nt((q * jnp.exp(bcum)).astype(BF16), st.astype(BF16))
        ke = (k * jnp.exp(btot - bcum)).astype(BF16)
        st_ref[j] = st * jnp.exp(btot) + _dot_tn(vb, ke)

        kb = k.astype(BF16)
        att = jnp.where(level == n_levels, _dot_nt(q.astype(BF16), kb), 0.0)
        f_prev = pltpu.roll(f, 1, 0)
        f_next = pltpu.roll(f, c - 1, 0)
        for h in range(n_levels):
            b = 1 << h
            if b == 1:
                x_l, y_l = (q * f).astype(BF16), kb
            else:
                if b == 2:
                    o4 = trow & 3
                    if reverse:
                        ex = jnp.where(o4 == 0, f * f_next, f)
                        ey = jnp.where(o4 == 3, f_prev, 1.0)
                    else:
                        ex = jnp.where(o4 == 3, f * f_prev, f)
                        ey = jnp.where(o4 == 0, f_next, 1.0)
                else:
                    b3 = bcum.reshape(c // (2 * b), 2 * b, dk)
                    edge = b if reverse else b - 1
                    ex = ey = jnp.exp(-jnp.abs(b3 - b3[:, edge : edge + 1, :])).reshape(c, dk)
                x_l, y_l = (q * ex).astype(BF16), (k * ey).astype(BF16)
            att = jnp.where(level == h, _dot_nt(x_l, y_l), att)
        o_ref[:, sl] = o + _dot(att.astype(BF16), vb)

    @pl.when(ci == pl.num_programs(2) - 1)
    def _():
        for j in range(hb):
            sfin_ref[0, j] = st_ref[j].T


def _hgrn(proj, lb, s0, row0, nb, l, d_a, f_section, reverse, prev=None):
    t = proj.shape[0]
    h, dk, dv = s0.shape[1:]
    c = min(HGRN_CHUNK, l)
    nc = l // c
    hb = next(k for k in (HGRN_HEADS_PER_STEP, 4, 2, 1) if h % k == 0)
    hw = hb * dk
    npb = d_a // hw

    def rows(b, ci):
        return row0 // c + b * nc + ((nc - 1 - ci) if reverse else ci)

    def in_spec(section):
        return pl.BlockSpec((c, hw), lambda b, g, ci: (rows(b, ci), section * npb + g))

    in_specs = [
        in_spec(0),
        in_spec(f_section),
        in_spec(3),
        pl.BlockSpec((1, hw), lambda b, g, ci: (0, g)),
        pl.BlockSpec((1, hb, dk, dv), lambda b, g, ci: (b, g, 0, 0)),
    ]
    args = [proj, proj, proj, lb.reshape(1, d_a), s0]
    kern = functools.partial(_hgrn_kernel, reverse=reverse, hb=hb, c=c, dk=dk)
    aliases = {}
    if prev is not None:
        in_specs.append(pl.BlockSpec(memory_space=pl.ANY))
        args.append(prev)
        aliases = {5: 0}
        kern = _drop_arg(kern, 5)
    return pl.pallas_call(
        kern,
        grid=(nb, npb, nc),
        in_specs=in_specs,
        out_specs=[
            pl.BlockSpec((c, hw), lambda b, g, ci: (rows(b, ci), g)),
            pl.BlockSpec((1, hb, dk, dv), lambda b, g, ci: (b, g, 0, 0)),
        ],
        out_shape=[jax.ShapeDtypeStruct((t, d_a), F32), jax.ShapeDtypeStruct((nb, h, dk, dv), F32)],
        scratch_shapes=[pltpu.VMEM((hb, dv, dk), F32)],
        input_output_aliases=aliases,
        compiler_params=_cparams("parallel", "parallel", "arbitrary"),
        name="hgrn_bwd" if reverse else "hgrn_fwd",
    )(*args)


def _hgrn_post_kernel(of_ref, ob_ref, g_ref, gn_ref, o_ref, *, nh, dv):
    for h in range(nh):
        sl = slice(h * dv, (h + 1) * dv)
        o = of_ref[:, sl] + ob_ref[:, sl]
        o = o * lax.rsqrt(jnp.mean(o * o, axis=-1, keepdims=True) + EPS) * gn_ref[:, sl]
        o_ref[:, sl] = (o * _silu(g_ref[:, sl])).astype(o_ref.dtype)


def _hgrn_post(o_fw, o_bw, proj, gnorm, nh, dv):
    t, d_a = o_fw.shape
    tm = _tile(t, 256)
    spec = pl.BlockSpec((tm, d_a), lambda i: (i, 0))
    return pl.pallas_call(
        functools.partial(_hgrn_post_kernel, nh=nh, dv=dv),
        grid=(t // tm,),
        in_specs=[spec, spec, pl.BlockSpec((tm, d_a), lambda i: (i, 4)), pl.BlockSpec((1, d_a), lambda i: (0, 0))],
        out_specs=spec,
        out_shape=jax.ShapeDtypeStruct((t, d_a), BF16),
        compiler_params=_cparams("parallel"),
        name="hgrn_post",
    )(o_fw, o_bw, proj, gnorm.reshape(1, d_a))


def _s5_tables(lam_re, lam_im, b_re, b_im, c_re, c_im, log_step, n_levels):
    hp = lax.Precision.HIGHEST
    tc = S5_CHUNK
    gb = S5_GB
    g, p, ch = b_re.shape[1:]
    nblk = g // gb
    lam = lax.complex(jnp.minimum(lam_re.astype(F32), -1e-4), lam_im.astype(F32))
    z = lam * jnp.exp(log_step.astype(F32))[..., None]
    b_bar = ((jnp.exp(z) - 1.0) / lam)[..., None] * lax.complex(b_re.astype(F32), b_im.astype(F32))
    cc = lax.complex(c_re.astype(F32), c_im.astype(F32))
    n = jnp.arange(tc + 1, dtype=F32)
    pw = jnp.exp(z[:, :, None, :] * n[None, None, :, None])

    kern = jnp.einsum("dgop,dglp,dgpi->dglio", cc, pw[:, :, :tc], b_bar, precision=hp).real
    lag_g = jnp.concatenate([kern[1][:, :0:-1], (kern[0][:, :1] + kern[1][:, :1]), kern[0][:, 1:]], axis=1)
    lagk = lag_g.reshape(nblk, gb, 2 * tc - 1, ch, ch).transpose(0, 2, 3, 1, 4).reshape(nblk, 2 * tc - 1, ch, gb * ch)

    inc_fw = (pw[0][:, tc - 1 - jnp.arange(tc), :, None] * b_bar[0][:, None]).transpose(0, 1, 3, 2)
    inc_bw = (pw[1][:, jnp.arange(tc), :, None] * b_bar[1][:, None]).transpose(0, 1, 3, 2)
    inc = jnp.concatenate([inc_fw.real, inc_fw.imag, inc_bw.real, inc_bw.imag], axis=-1)
    inct = inc.reshape(nblk, gb, tc, ch, 4 * p).transpose(0, 2, 3, 1, 4).reshape(nblk, tc, ch, gb * 4 * p)

    out_fw = cc[0][:, None] * pw[0][:, 1 + jnp.arange(tc), None, :]
    out_bw = cc[1][:, None] * pw[1][:, tc - jnp.arange(tc), None, :]
    out = jnp.concatenate([out_fw.real, -out_fw.imag, out_bw.real, -out_bw.imag], axis=-1)
    outt = out.reshape(nblk, gb, tc, ch, 4 * p).transpose(0, 2, 4, 1, 3).reshape(nblk, tc, 4 * p, gb * ch)

    steps = (tc * (2 ** jnp.arange(n_levels))).astype(F32)
    lp = jnp.exp(z[:, :, None, :] * steps[None, None, :, None])
    a, b = lp.real, lp.imag
    lev = jnp.stack(
        [jnp.concatenate([a, a], -1), jnp.concatenate([-b, b], -1), jnp.concatenate([b, -b], -1)], axis=3
    )
    lev = lev.transpose(1, 0, 2, 3, 4).reshape(g, 2 * n_levels * 3, 2 * p)
    pad = (-lev.shape[1]) % 8
    lev = jnp.pad(lev, ((0, 0), (0, pad), (0, 0)))
    return lagk, inct.astype(BF16), outt.astype(BF16), lev


def _s5_kernel(u_ref, lagk_ref, inct_ref, outt_ref, lev_ref, h0_ref, d_ref, *rest, nch, n_levels, max_levels, want_fin):
    if want_fin:
        o_ref, fin_ref, w1_ref, wt_ref, w2_ref, a_ref = rest
    else:
        o_ref, w1_ref, wt_ref, w2_ref, a_ref = rest
        fin_ref = None
    tc = S5_CHUNK
    gb = S5_GB
    lanes = u_ref.shape[1]
    ch = lanes // gb
    sw = lev_ref.shape[2]
    m = u_ref.shape[0] // tc
    nbk = m // nch
    b0 = pl.program_id(1) * nbk

    @pl.when(pl.program_id(1) == 0)
    def _():
        r1 = lax.broadcasted_iota(jnp.int32, (lanes, gb * 2 * sw), 0) // ch
        c1 = lax.broadcasted_iota(jnp.int32, (lanes, gb * 2 * sw), 1) // (2 * sw)
        for s in range(tc):
            w1_ref[s * lanes : (s + 1) * lanes, :] = jnp.where(r1 == c1, jnp.tile(inct_ref[0, s], (gb, 1)), 0.0)
        rd = lax.broadcasted_iota(jnp.int32, (lanes, lanes), 0) // ch
        cd = lax.broadcasted_iota(jnp.int32, (lanes, lanes), 1) // ch
        taps = [jnp.where(rd == cd, jnp.tile(lagk_ref[0, li], (gb, 1)), 0.0).astype(BF16) for li in range(2 * tc - 1)]
        for s in range(tc):
            for t in range(tc):
                wt_ref[s * lanes : (s + 1) * lanes, t * lanes : (t + 1) * lanes] = taps[t - s + tc - 1]
        r2 = lax.broadcasted_iota(jnp.int32, (gb * 2 * sw, lanes), 0) // (2 * sw)
        c2 = lax.broadcasted_iota(jnp.int32, (gb * 2 * sw, lanes), 1) // ch
        for t in range(tc):
            w2_ref[:, t * lanes : (t + 1) * lanes] = jnp.where(r2 == c2, jnp.tile(outt_ref[0, t], (gb, 1)), 0.0)

    xs_f32 = [u_ref[pl.ds(s, m, stride=tc), :] for s in range(tc)]
    xcat = jnp.concatenate([x.astype(BF16) for x in xs_f32], axis=1)
    inc = _dot(xcat, w1_ref[...])

    ridx = lax.broadcasted_iota(jnp.int32, (m, sw), 0)
    cidx = ridx % nch
    bidx = ridx // nch
    carry = []
    for g in range(gb):
        for d in range(2):
            fwd = d == 0
            x = inc[:, (2 * g + d) * sw : (2 * g + d + 1) * sw]
            xr = pltpu.roll(x, sw // 2, 1)
            h0 = jnp.zeros((m, sw), F32)
            h0r = jnp.zeros((m, sw), F32)
            for b in range(nbk):
                h0 = jnp.where(bidx == b, h0_ref[g, 2 * d, pl.ds(b0 + b, 1), :], h0)
                h0r = jnp.where(bidx == b, h0_ref[g, 2 * d + 1, pl.ds(b0 + b, 1), :], h0r)

            def const(j, which, d=d, g=g):
                r = (d * max_levels + j) * 3 + which
                return lev_ref[g, r : r + 1, :]

            entry = (cidx == 0) if fwd else (cidx == nch - 1)
            x = x + jnp.where(entry, h0 * const(0, 0) + h0r * const(0, 1), 0.0)
            xr = xr + jnp.where(entry, h0r * const(0, 0) + h0 * const(0, 2), 0.0)
            for j in range(n_levels):
                k = 1 << j
                keep = (cidx >= k) if fwd else (cidx < nch - k)
                shift = k if fwd else m - k
                sx = jnp.where(keep, pltpu.roll(x, shift, 0), 0.0)
                sxr = jnp.where(keep, pltpu.roll(xr, shift, 0), 0.0)
                x, xr = x + const(j, 0) * sx + const(j, 1) * sxr, xr + const(j, 0) * sxr + const(j, 2) * sx
            carry.append(jnp.where(entry, h0, pltpu.roll(x, 1 if fwd else m - 1, 0)).astype(BF16))
            if want_fin:
                a_ref[...] = x
                fin_ref[g, d] = a_ref[pl.ds(nch - 1 if fwd else 0, nbk, stride=nch), :]
    y = _dot(xcat, wt_ref[...]) + _dot(jnp.concatenate(carry, axis=1), w2_ref[...])
    for t in range(tc):
        o_ref[pl.ds(t, m, stride=tc), :] = jax.nn.gelu(y[:, t * lanes : (t + 1) * lanes] + d_ref[...] * xs_f32[t])


def _s5(proj, u_col0, tables, h0, s5_d, row0, nb, l, max_levels, want_fin, prev=None):
    lagk, inct, outt, lev = tables
    t = proj.shape[0]
    tc, gb = S5_CHUNK, S5_GB
    nblk = lagk.shape[0]
    lanes = lagk.shape[3]
    g = nblk * gb
    sw = lev.shape[2]
    nch = l // tc
    n_levels = int(math.log2(nch))
    nbk = nb
    if not want_fin:
        nbk = max(k for k in range(1, nb + 1) if nb % k == 0 and row0 % (k * l) == 0 and (k == 1 or k * nch <= 256))
    m = nbk * nch
    rows = m * tc
    in_specs = [
        pl.BlockSpec((rows, lanes), lambda i, b: (row0 // rows + b, u_col0 // lanes + i)),
        pl.BlockSpec((1,) + lagk.shape[1:], lambda i, b: (i, 0, 0, 0)),
        pl.BlockSpec((1,) + inct.shape[1:], lambda i, b: (i, 0, 0, 0)),
        pl.BlockSpec((1,) + outt.shape[1:], lambda i, b: (i, 0, 0, 0)),
        pl.BlockSpec((gb,) + lev.shape[1:], lambda i, b: (i, 0, 0)),
        pl.BlockSpec((gb, 4, nb, sw), lambda i, b: (i, 0, 0, 0)),
        pl.BlockSpec((1, lanes), lambda i, b: (0, i)),
    ]
    args = [proj, lagk, inct, outt, lev, h0, s5_d.reshape(1, g * (lanes // gb))]
    out_specs = [pl.BlockSpec((rows, lanes), lambda i, b: (row0 // rows + b, i))]
    out_shape = [jax.ShapeDtypeStruct((t, g * (lanes // gb)), F32)]
    if want_fin:
        out_specs.append(pl.BlockSpec((gb, 2, nbk, sw), lambda i, b: (i, 0, b, 0)))
        out_shape.append(jax.ShapeDtypeStruct((g, 2, nb, sw), F32))
    kern = functools.partial(_s5_kernel, nch=nch, n_levels=n_levels, max_levels=max_levels, want_fin=want_fin)
    aliases = {}
    if prev is not None:
        in_specs.append(pl.BlockSpec(memory_space=pl.ANY))
        args.append(prev)
        aliases = {7: 0}
        kern = _drop_arg(kern, 7)
    out = pl.pallas_call(
        kern,
        grid=(nblk, nb // nbk),
        in_specs=in_specs,
        out_specs=out_specs,
        out_shape=out_shape,
        scratch_shapes=[
            pltpu.VMEM((tc * lanes, gb * 2 * sw), BF16),
            pltpu.VMEM((tc * lanes, tc * lanes), BF16),
            pltpu.VMEM((gb * 2 * sw, tc * lanes), BF16),
            pltpu.VMEM((m, sw), F32),
        ],
        input_output_aliases=aliases,
        compiler_params=_cparams("parallel", "arbitrary"),
        name="s5",
    )(*args)
    return (out[0], out[1]) if want_fin else (out[0], None)


def _dft_cos_sin(num, den, scale):
    ang = (2.0 * math.pi / den) * (num % den).astype(np.float64)
    return jnp.asarray(np.cos(ang) * scale, dtype=BF16), jnp.asarray(np.sin(ang) * scale, dtype=BF16)


def _fnet_pos_kernel(c_ref, s_ref, uv_ref, o_ref, *, gd):
    uv = uv_ref[...]
    o_ref[...] = (_dot(c_ref[...], uv[:, :gd]) - _dot(s_ref[...], uv[:, gd:])).astype(o_ref.dtype)


def _fnet_pos(cp, sp, uv, row0, nb, l, gd, prev=None):
    t = uv.shape[0]
    ng = uv.shape[1] // (2 * gd)
    const = pl.BlockSpec((l, l), lambda b, g: (0, 0))
    in_specs = [const, const, pl.BlockSpec((l, 2 * gd), lambda b, g: (row0 // l + b, g))]
    args = [cp, sp, uv]
    kern = functools.partial(_fnet_pos_kernel, gd=gd)
    aliases = {}
    if prev is not None:
        in_specs.append(pl.BlockSpec(memory_space=pl.ANY))
        args.append(prev)
        aliases = {3: 0}
        kern = _drop_arg(kern, 3)
    return pl.pallas_call(
        kern,
        grid=(nb, ng),
        in_specs=in_specs,
        out_specs=pl.BlockSpec((l, gd), lambda b, g: (row0 // l + b, g)),
        out_shape=jax.ShapeDtypeStruct((t, ng * gd), BF16),
        input_output_aliases=aliases,
        compiler_params=_cparams("parallel", "parallel"),
        name="fnet_pos",
    )(*args)


def _fourier_mix(xn, tp, lp, ls):
    t, d = xn.shape
    gd = d // FNET_GROUPS
    kk = np.arange(gd)
    cc, sc = _dft_cos_sin(kk[:, None] * kk[None, :], gd, gd**-0.5)
    cs = jnp.concatenate([cc, sc], axis=1)
    tm = _tile(t, 1024)
    uv = pl.pallas_call(
        functools.partial(_mm_kernel, n_pairs=1, epilogue=_plain),
        grid=(t // tm, FNET_GROUPS),
        in_specs=[pl.BlockSpec((tm, gd), lambda i, g: (i, g)), pl.BlockSpec((gd, 2 * gd), lambda i, g: (0, 0))],
        out_specs=pl.BlockSpec((tm, 2 * gd), lambda i, g: (i, g)),
        out_shape=jax.ShapeDtypeStruct((t, 2 * d), BF16),
        compiler_params=_cparams("parallel", "parallel"),
        name="fnet_chan",
    )(xn, cs)
    pp = np.arange(lp)
    cp_p, sp_p = _dft_cos_sin(pp[:, None] * pp[None, :], lp, lp**-0.5)
    ps = np.arange(ls)
    r, c = ps // GRID_W, ps % GRID_W
    rows = ls // GRID_W
    lcm = rows * GRID_W // math.gcd(rows, GRID_W)
    num = (lcm // rows) * (r[:, None] * r[None, :]) + (lcm // GRID_W) * (c[:, None] * c[None, :])
    cp_s, sp_s = _dft_cos_sin(num, lcm, ls**-0.5)
    z = _fnet_pos(cp_p, sp_p, uv, 0, tp // lp, lp, gd)
    return _fnet_pos(cp_s, sp_s, uv, tp, (t - tp) // ls, ls, gd, prev=z)


def _router_kernel(x_ref, w_ref, b_ref, o_ref, *, ng, npg):
    h = w_ref.shape[0] // 2
    x_lo, x_hi = _unpack_halves(_load_slabs(x_ref, (), o_ref.shape[0], h // LANES))
    logits = _dot(x_lo.astype(BF16), w_ref[:h, :]) + _dot(x_hi.astype(BF16), w_ref[h:, :]) + b_ref[...]
    lane = lax.broadcasted_iota(jnp.int32, logits.shape, 1).astype(F32)
    big = 1e9

    def first_max(vals):
        m = jnp.max(vals, axis=-1, keepdims=True)
        return m, jnp.min(jnp.where(vals == m, lane, big), axis=-1, keepdims=True)

    gl = jnp.where(lane < ng, logits, NEG)
    gmax, gidx = first_max(gl)
    gprob = 1.0 / jnp.sum(jnp.exp(gl - gmax), axis=-1, keepdims=True)
    lo = ng + gidx * npg
    el = jnp.where((lane >= lo) & (lane < lo + npg), logits, NEG)
    m1, i1 = first_max(el)
    m2, i2 = first_max(jnp.where(lane == i1, NEG, el))
    e = jnp.exp(m2 - m1)
    w1 = gprob / (1.0 + e)
    out = jnp.where(lane == 0, i1 - ng, jnp.where(lane == 1, i2 - ng, jnp.where(lane == 2, w1, jnp.where(lane == 3, w1 * e, 0.0))))
    o_ref[...] = out


def _router(xn, wr_g, br_g, wr_e, br_e):
    d = wr_g.shape[0]
    t = xn.shape[0] // (d // 2 // LANES)
    ng, ne = wr_g.shape[1], wr_e.shape[1]
    w = jnp.zeros((d, 128), F32).at[:, :ng].set(wr_g).at[:, ng : ng + ne].set(wr_e).astype(BF16)
    b = jnp.zeros((1, 128), F32).at[0, :ng].set(br_g).at[0, ng : ng + ne].set(br_e)
    tm = _tile(t, 512)
    return pl.pallas_call(
        functools.partial(_router_kernel, ng=ng, npg=ne // ng),
        grid=(t // tm,),
        in_specs=[pl.BlockSpec((tm * (d // 2 // LANES), LANES), lambda i: (i, 0)), pl.BlockSpec((d, 128), lambda i: (0, 0)), pl.BlockSpec((1, 128), lambda i: (0, 0))],
        out_specs=pl.BlockSpec((tm, 128), lambda i: (i, 0)),
        out_shape=jax.ShapeDtypeStruct((t, 128), F32),
        compiler_params=_cparams("parallel"),
        name="router",
    )(xn, w, b)


def _moe_plan(rout, n_exp, tm, nt):
    t = rout.shape[0]
    flat = rout[:, :2].astype(jnp.int32).T.reshape(-1)
    oh = (flat[:, None] == jnp.arange(n_exp)[None, :]).astype(jnp.int32)
    blk = _tile(2 * t, 128)
    c1 = jnp.cumsum(oh.reshape(-1, blk, n_exp), axis=1)
    tot = c1[:, -1]
    cs = (c1 + (jnp.cumsum(tot, axis=0) - tot)[:, None]).reshape(2 * t, n_exp)
    rank = jnp.sum(oh * cs, axis=1) - 1
    counts = cs[-1]
    tiles = (counts + tm - 1) // tm
    tile_end = jnp.cumsum(tiles)
    pos = (tile_end - tiles)[flat] * tm + rank
    row_token = jnp.zeros((nt * tm,), jnp.int32).at[pos].set(jnp.tile(jnp.arange(t, dtype=jnp.int32), 2))
    tile_ids = jnp.arange(nt)
    tile_expert = jnp.minimum(jnp.searchsorted(tile_end, tile_ids, side="right"), n_exp - 1).astype(jnp.int32)
    tile_valid = (tile_ids < tile_end[-1]).astype(jnp.int32)
    prev_expert = jnp.concatenate([jnp.full((1,), -1, jnp.int32), tile_expert[:-1]])
    tile_first = (tile_valid * (tile_expert != prev_expert)).astype(jnp.int32)
    owner = jnp.where(tiles > 0, jnp.arange(n_exp), n_exp)
    nxt = jnp.concatenate([lax.cummin(owner[::-1])[::-1][1:], jnp.full((1,), n_exp)])
    tile_next = jnp.where(nxt < n_exp, nxt, -1)[tile_expert].astype(jnp.int32)
    return row_token, pos.astype(jnp.int32), tile_expert, tile_valid, tile_first, tile_next


def _moe_kernel(te_ref, tv_ref, tf_ref, tn_ref, rt_ref, x_hbm, wg_hbm, wu_hbm, wd_hbm, y_hbm,
                xbuf, stg_g, stg_u, stg_d, wb_g, wb_u, wb_d, obuf, sem_x, sem_w, sem_o, *, tm, layer):
    i = pl.program_id(0)
    nt = pl.num_programs(0)
    slot = i % 2
    d, f = stg_g.shape
    sl = d // 2 // LANES

    def issue_rows(tile, s):
        def body(r, carry):
            src = pl.multiple_of(rt_ref[tile * tm + r] * sl, sl)
            dst = pl.multiple_of(r * sl, sl)
            pltpu.make_async_copy(x_hbm.at[pl.ds(src, sl)], xbuf.at[s, pl.ds(dst, sl)], sem_x.at[s]).start()
            return carry

        lax.fori_loop(0, tm, body, 0, unroll=8)

    def weight_copies(e):
        return (
            pltpu.make_async_copy(wg_hbm.at[layer, e], stg_g, sem_w.at[0]),
            pltpu.make_async_copy(wu_hbm.at[layer, e], stg_u, sem_w.at[1]),
            pltpu.make_async_copy(wd_hbm.at[layer, e], stg_d, sem_w.at[2]),
        )

    def out_copy(tile):
        rows = tm * sl
        return pltpu.make_async_copy(obuf, y_hbm.at[pl.ds(pl.multiple_of(tile * rows, rows), rows)], sem_o.at[0])

    def cast(src, dst, ck):
        def body(k, carry):
            r = pl.multiple_of(k * ck, ck)
            dst[pl.ds(r, ck), :] = src[pl.ds(r, ck), :].astype(BF16)
            return carry

        lax.fori_loop(0, src.shape[0] // ck, body, 0)

    @pl.when(i == 0)
    def _():
        issue_rows(0, 0)
        for cp in weight_copies(te_ref[0]):
            cp.start(priority=1)

    nxt = jnp.minimum(i + 1, nt - 1)

    @pl.when(jnp.logical_and(i + 1 < nt, tv_ref[nxt] == 1))
    def _():
        issue_rows(i + 1, 1 - slot)

    prev_valid = jnp.logical_and(i > 0, tv_ref[jnp.maximum(i - 1, 0)] == 1)

    @pl.when(prev_valid)
    def _():
        out_copy(i - 1).wait()

    @pl.when(tv_ref[i] == 1)
    def _():
        @pl.when(tf_ref[i] == 1)
        def _():
            for cp in weight_copies(0):
                cp.wait()
            cast(stg_g, wb_g, min(256, d))
            cast(stg_u, wb_u, min(256, d))
            cast(stg_d, wb_d, min(32, f))

            @pl.when(tn_ref[i] >= 0)
            def _():
                for cp in weight_copies(tn_ref[i]):
                    cp.start(priority=1)

        pltpu.make_async_copy(x_hbm.at[pl.ds(0, tm * sl)], xbuf.at[slot], sem_x.at[slot]).wait()
        x_lo, x_hi = (v.astype(BF16) for v in _unpack_halves(_load_slabs(xbuf, (slot,), tm, sl)))
        dh = d // 2
        hg = _dot(x_lo, wb_g[:dh, :]) + _dot(x_hi, wb_g[dh:, :])
        hu = _dot(x_lo, wb_u[:dh, :]) + _dot(x_hi, wb_u[dh:, :])
        h = (_silu(hg) * hu).astype(BF16)
        _store_slabs(obuf, (), _pack_pair(_dot(h, wb_d[:, :dh]), _dot(h, wb_d[:, dh:])))
        out_copy(i).start()

        @pl.when(i == nt - 1)
        def _():
            out_copy(i).wait()


def _combine_kernel(pos_ref, y_hbm, x_ref, r_ref, g_ref, o_ref, buf, sem, *, tm, t):
    i = pl.program_id(0)
    n = pl.num_programs(0)
    slot = i % 2

    sl = buf.shape[2] // tm

    def issue(tile, s):
        def body(r, carry):
            dst = pl.multiple_of(r * sl, sl)
            for k in range(2):
                src = pl.multiple_of(pos_ref[k * t + tile * tm + r] * sl, sl)
                pltpu.make_async_copy(y_hbm.at[pl.ds(src, sl)], buf.at[s, k, pl.ds(dst, sl)], sem.at[s]).start()
            return carry

        lax.fori_loop(0, tm, body, 0, unroll=4)

    @pl.when(i == 0)
    def _():
        issue(0, 0)

    @pl.when(i + 1 < n)
    def _():
        issue(i + 1, 1 - slot)

    for k in range(2):
        pltpu.make_async_copy(y_hbm.at[pl.ds(0, tm * sl)], buf.at[slot, k], sem.at[slot]).wait()
    w = r_ref[...]
    w1, w2 = w[:, 2:3], w[:, 3:4]
    a_lo, a_hi = _unpack_halves(_load_slabs(buf, (slot, 0), tm, sl))
    b_lo, b_hi = _unpack_halves(_load_slabs(buf, (slot, 1), tm, sl))
    dh = a_lo.shape[1]
    o_ref[:, :dh] = x_ref[:, :dh] + g_ref[0, :, :dh] * (w1 * a_lo + w2 * b_lo)
    o_ref[:, dh:] = x_ref[:, dh:] + g_ref[0, :, dh:] * (w1 * a_hi + w2 * b_hi)


def _moe(xn, x, gate, params, layer, tp, ls):
    wr_g, br_g, wr_e, br_e, w_gate, w_up, w_down = params
    t, d = x.shape
    n_exp, _, f = w_gate.shape[1:]
    rout = _router(xn, wr_g, br_g, wr_e, br_e)
    sl = d // 2 // LANES
    tm = 256 if t >= 4096 else 64
    nt = (2 * t + tm - 1) // tm + n_exp
    row_token, pos, tile_expert, tile_valid, tile_first, tile_next = _moe_plan(rout, n_exp, tm, nt)
    hbm = pl.BlockSpec(memory_space=pl.ANY)
    y = pl.pallas_call(
        functools.partial(_moe_kernel, tm=tm, layer=layer),
        grid_spec=pltpu.PrefetchScalarGridSpec(
            num_scalar_prefetch=5,
            grid=(nt,),
            in_specs=[hbm, hbm, hbm, hbm],
            out_specs=hbm,
            scratch_shapes=[
                pltpu.VMEM((2, tm * sl, LANES), jnp.uint32),
                pltpu.VMEM((d, f), F32),
                pltpu.VMEM((d, f), F32),
                pltpu.VMEM((f, d), F32),
                pltpu.VMEM((d, f), BF16),
                pltpu.VMEM((d, f), BF16),
                pltpu.VMEM((f, d), BF16),
                pltpu.VMEM((tm * sl, LANES), jnp.uint32),
                pltpu.SemaphoreType.DMA((2,)),
                pltpu.SemaphoreType.DMA((3,)),
                pltpu.SemaphoreType.DMA((1,)),
            ],
        ),
        out_shape=jax.ShapeDtypeStruct((nt * tm * sl, LANES), jnp.uint32),
        compiler_params=_cparams("arbitrary"),
        name="moe_experts",
    )(tile_expert, tile_valid, tile_first, tile_next, row_token, xn, w_gate, w_up, w_down)

    tc = _tile(math.gcd(tp, ls), 128)
    return pl.pallas_call(
        functools.partial(_combine_kernel, tm=tc, t=t),
        grid_spec=pltpu.PrefetchScalarGridSpec(
            num_scalar_prefetch=1,
            grid=(t // tc,),
            in_specs=[
                pl.BlockSpec(memory_space=pl.ANY),
                pl.BlockSpec((tc, d), lambda i, p: (i, 0)),
                pl.BlockSpec((tc, 128), lambda i, p: (i, 0)),
                pl.BlockSpec((1, 1, d), lambda i, p: (_mod_row(i, tc, tp, ls), 0, 0)),
            ],
            out_specs=pl.BlockSpec((tc, d), lambda i, p: (i, 0)),
            scratch_shapes=[pltpu.VMEM((2, 2, tc * sl, LANES), jnp.uint32), pltpu.SemaphoreType.DMA((2,))],
        ),
        out_shape=jax.ShapeDtypeStruct((t, d), F32),
        compiler_params=_cparams("arbitrary"),
        name="moe_combine",
    )(pos, y, x, rout, gate)


def kernel(x_prompt, x_sample, state_hgrn, state_s5_re, state_s5_im, c, c_ctx, ada_w, ada_b, norm_mix, norm_ffn, norm_final, ab_w_in, ab_w_out, hgrn_lb_theta, hgrn_gnorm, s5_lam_re, s5_lam_im, s5_b_re, s5_b_im, s5_c_re, s5_c_im, s5_log_step, s5_d, s5_glu_w, s5_glu_b, fnet_w_out, moe_wr_group, moe_br_group, moe_wr_expert, moe_br_expert, moe_w_gate, moe_w_up, moe_w_down):
    bp, lp, d = x_prompt.shape
    bs, ls, _ = x_sample.shape
    tp, ts = bp * lp, bs * ls
    depth = ada_w.shape[0]
    nh, dk, dv = state_hgrn.shape[3:]
    d_a = nh * dk
    n_grp, s5_p = state_s5_re.shape[3:]
    d_b = s5_d.shape[1]
    nmod = 1 + bs

    x = jnp.concatenate([x_prompt.reshape(tp, d), x_sample.reshape(ts, d)], axis=0)
    c8 = jnp.zeros((8, d), F32).at[0].set(c_ctx).at[1:nmod].set(c)
    mod_all = _ada(c8, ada_w, ada_b)
    lb_all = jnp.cumsum(jax.nn.softmax(hgrn_lb_theta.astype(F32), axis=1), axis=1)
    max_levels = int(math.log2(max(lp, ls) // S5_CHUNK))

    new_h, new_re, new_im = [], [], []
    i_ab = i_c = 0
    for l in range(depth):
        mod = mod_all[l, :nmod].reshape(nmod, 6, 1, d).transpose(1, 0, 2, 3)
        xn = _norm_mod(x, norm_mix[l], mod[0], mod[1], tp, ls, False)
        if l % 2 == 0:
            tm = _tile(math.gcd(tp, ls), 1024)
            n_in = ab_w_in.shape[2]
            proj = _matmul([(xn, ab_w_in[i_ab].astype(BF16))], n_in, F32, tm, _tile(n_in, 1024, 128), name="proj_in")
            tables = _s5_tables(s5_lam_re[i_ab], s5_lam_im[i_ab], s5_b_re[i_ab], s5_b_im[i_ab], s5_c_re[i_ab], s5_c_im[i_ab], s5_log_step[i_ab], max_levels)
            o_dirs, yg, h_fin, x_fin = [None, None], None, None, None
            for row0, nb, ln, h0, x0 in (
                (0, bp, lp, jnp.zeros((bp, 2, nh, dk, dv), F32), jnp.zeros((bp, 2, n_grp, 2 * s5_p), F32)),
                (tp, bs, ls, state_hgrn[:, i_ab].astype(F32), jnp.concatenate([state_s5_re[:, i_ab], state_s5_im[:, i_ab]], axis=-1).astype(F32)),
            ):
                fins = []
                for dr in range(2):
                    o_dirs[dr], s_fin = _hgrn(proj, lb_all[dr, i_ab], h0[:, dr], row0, nb, ln, d_a, 1 + dr, dr == 1, prev=o_dirs[dr])
                    fins.append(s_fin)
                x0s = jnp.concatenate([x0[..., s5_p:], x0[..., :s5_p]], axis=-1)
                h0_s5 = jnp.stack([x0[:, 0], x0s[:, 0], x0[:, 1], x0s[:, 1]], axis=0).transpose(2, 0, 1, 3)
                yg, fin_s5 = _s5(proj, 5 * d_a, tables, h0_s5, s5_d[i_ab], row0, nb, ln, max_levels, row0 == 0, prev=yg)
                if row0 == 0:
                    h_fin = jnp.stack(fins, axis=1)
                    x_fin = fin_s5.transpose(2, 1, 0, 3)
            o_a = _hgrn_post(o_dirs[0], o_dirs[1], proj, hgrn_gnorm[i_ab], nh, dv)
            tn = _tile(d_b, 1024, 128)
            y_glu = _matmul(
                [(yg, s5_glu_w[i_ab].astype(BF16))], d_b, BF16, tm, tn, _glu, (yg, s5_glu_b[i_ab].reshape(1, d_b)),
                (pl.BlockSpec((tm, tn), lambda i, j: (i, j)), pl.BlockSpec((1, tn), lambda i, j: (0, j))), "s5_glu",
            )
            w_out = ab_w_out[i_ab].astype(BF16)
            x = _matmul_residual([(o_a, w_out[:d_a]), (y_glu, w_out[d_a:])], x, mod[2], tp, ls, "mix_out")
            new_h.append(h_fin)
            new_re.append(x_fin[..., :s5_p])
            new_im.append(x_fin[..., s5_p:])
            i_ab += 1
        else:
            z = _fourier_mix(xn, tp, lp, ls)
            x = _matmul_residual([(z, fnet_w_out[i_c].astype(BF16))], x, mod[2], tp, ls, "fnet_out")
            i_c += 1
        xn = _norm_mod(x, norm_ffn[l], mod[3], mod[4], tp, ls, True)
        moe_params = (moe_wr_group[l], moe_br_group[l], moe_wr_expert[l], moe_br_expert[l], moe_w_gate, moe_w_up, moe_w_down)
        x = _moe(xn, x, mod[5], moe_params, l, tp, ls)
    return (
        _rmsnorm(x, norm_final, 0, tp).reshape(bp, lp, d),
        _rmsnorm(x, norm_final, tp, ts).reshape(bs, ls, d),
        jnp.stack(new_h, axis=1),
        jnp.stack(new_re, axis=1),
        jnp.stack(new_im, axis=1),
    )
```

```python
import functools
import math

import jax
import jax.numpy as jnp
import numpy as np
from jax import lax
from jax.experimental import pallas as pl
from jax.experimental.pallas import tpu as pltpu

F32 = jnp.float32
BF16 = jnp.bfloat16
EPS = 1e-6
NEG = -1e30
V7X_VMEM_LIMIT_BYTES = 56 * 1024 * 1024
FNET_GROUPS = 8
GRID_W = 64
HGRN_CHUNK = 128
HGRN_HEADS_PER_STEP = 8
S5_CHUNK = 16
S5_GB = 8


def _cparams(*sem):
    return pltpu.CompilerParams(dimension_semantics=sem, vmem_limit_bytes=V7X_VMEM_LIMIT_BYTES)


def _tile(n, pref, mult=8):
    if n <= pref:
        return n
    for t in range(pref - pref % mult, 0, -mult):
        if n % t == 0:
            return t
    raise ValueError(f"no tile for {n}")


def _drop_arg(body, index):
    def wrapped(*refs):
        return body(*refs[:index], *refs[index + 1 :])

    return wrapped


def _dot(a, b):
    return jnp.dot(a, b, preferred_element_type=F32)


def _dot_nt(a, b):
    return lax.dot_general(a, b, (((1,), (1,)), ((), ())), preferred_element_type=F32)


def _dot_tn(a, b):
    return lax.dot_general(a, b, (((0,), (0,)), ((), ())), preferred_element_type=F32)


def _silu(x):
    return x * jax.nn.sigmoid(x)


def _ada_kernel(c_ref, w_ref, b_ref, o_ref):
    a = _silu(c_ref[...]).astype(BF16)
    o_ref[0] = _dot(a, w_ref[0].astype(BF16)) + b_ref[0]


def _ada(c8, ada_w, ada_b):
    depth, d, n = ada_w.shape
    tn = _tile(n, 1024, 128)
    return pl.pallas_call(
        _ada_kernel,
        grid=(depth, n // tn),
        in_specs=[
            pl.BlockSpec((8, d), lambda l, j: (0, 0)),
            pl.BlockSpec((1, d, tn), lambda l, j: (l, 0, j)),
            pl.BlockSpec((1, 1, tn), lambda l, j: (l, 0, j)),
        ],
        out_specs=pl.BlockSpec((1, 8, tn), lambda l, j: (l, 0, j)),
        out_shape=jax.ShapeDtypeStruct((depth, 8, n), F32),
        compiler_params=_cparams("parallel", "parallel"),
        name="ada",
    )(c8, ada_w, ada_b.reshape(depth, 1, n))


def _mod_row(i, tm, tp, ls):
    r0 = i * tm
    return jnp.where(r0 < tp, 0, 1 + (r0 - tp) // ls)


def _pack_halves(y):
    h = y.shape[1] // 2
    return _pack_pair(y[:, :h], y[:, h:])


def _pack_pair(lo, hi):
    def bf16_bits(v):
        return lax.bitcast_convert_type(v.astype(BF16).astype(F32), jnp.uint32)

    return (bf16_bits(lo) >> 16) | (bf16_bits(hi) & jnp.uint32(0xFFFF0000))


def _unpack_halves(w):
    lo = lax.bitcast_convert_type(w << 16, F32)
    hi = lax.bitcast_convert_type(w & jnp.uint32(0xFFFF0000), F32)
    return lo, hi


LANES = 128


def _store_slabs(ref, lead, words):
    m, w = words.shape
    s = w // LANES
    for j in range(s):
        ref[(*lead, pl.ds(j, m, stride=s), slice(None))] = words[:, j * LANES : (j + 1) * LANES]


def _load_slabs(ref, lead, m, s):
    return jnp.concatenate([ref[(*lead, pl.ds(j, m, stride=s), slice(None))] for j in range(s)], axis=1)


def _norm_mod_kernel(x_ref, g_ref, sh_ref, sc_ref, o_ref, *, packed):
    x = x_ref[...]
    y = x * lax.rsqrt(jnp.mean(x * x, axis=-1, keepdims=True) + EPS) * g_ref[...]
    y = y * (1.0 + sc_ref[0]) + sh_ref[0]
    if packed:
        _store_slabs(o_ref, (), _pack_halves(y))
    else:
        o_ref[...] = y.astype(o_ref.dtype)


def _norm_mod(x, g, shift, scale, tp, ls, packed):
    t, d = x.shape
    tm = _tile(math.gcd(tp, ls), 256)
    mod_spec = pl.BlockSpec((1, 1, d), lambda i: (_mod_row(i, tm, tp, ls), 0, 0))
    s = d // 2 // LANES
    out_spec = pl.BlockSpec((tm * s, LANES), lambda i: (i, 0)) if packed else pl.BlockSpec((tm, d), lambda i: (i, 0))
    out_shape = jax.ShapeDtypeStruct((t * s, LANES), jnp.uint32) if packed else jax.ShapeDtypeStruct((t, d), BF16)
    return pl.pallas_call(
        functools.partial(_norm_mod_kernel, packed=packed),
        grid=(t // tm,),
        in_specs=[pl.BlockSpec((tm, d), lambda i: (i, 0)), pl.BlockSpec((1, d), lambda i: (0, 0)), mod_spec, mod_spec],
        out_specs=out_spec,
        out_shape=out_shape,
        compiler_params=_cparams("parallel"),
        name="norm_mod",
    )(x, g.reshape(1, d), shift, scale)


def _rmsnorm_kernel(x_ref, g_ref, o_ref):
    x = x_ref[...]
    o_ref[...] = x * lax.rsqrt(jnp.mean(x * x, axis=-1, keepdims=True) + EPS) * g_ref[...]


def _rmsnorm(x, g, row0, n):
    d = x.shape[1]
    tm = _tile(math.gcd(row0, n) if row0 else n, 256)
    return pl.pallas_call(
        _rmsnorm_kernel,
        grid=(n // tm,),
        in_specs=[pl.BlockSpec((tm, d), lambda i: (row0 // tm + i, 0)), pl.BlockSpec((1, d), lambda i: (0, 0))],
        out_specs=pl.BlockSpec((tm, d), lambda i: (i, 0)),
        out_shape=jax.ShapeDtypeStruct((n, d), F32),
        compiler_params=_cparams("parallel"),
        name="final_norm",
    )(x, g.reshape(1, d))


def _mm_kernel(*refs, n_pairs, epilogue):
    o_ref = refs[-1]
    acc = None
    for a_ref, b_ref in zip(refs[:n_pairs], refs[n_pairs : 2 * n_pairs]):
        part = _dot(a_ref[...].astype(BF16), b_ref[...])
        acc = part if acc is None else acc + part
    o_ref[...] = epilogue(acc, *refs[2 * n_pairs : -1]).astype(o_ref.dtype)


def _plain(acc):
    return acc


def _residual(acc, x_ref, g_ref):
    return x_ref[...] + g_ref[0] * acc


def _glu(acc, y_ref, b_ref):
    return y_ref[...] * jax.nn.sigmoid(acc + b_ref[...])


def _matmul(pairs, n, out_dtype, tm, tn, epilogue=_plain, extra=(), extra_specs=(), name="matmul"):
    t = pairs[0][0].shape[0]
    a_specs = [pl.BlockSpec((tm, a.shape[1]), lambda i, j: (i, 0)) for a, _ in pairs]
    b_specs = [pl.BlockSpec((b.shape[0], tn), lambda i, j: (0, j)) for _, b in pairs]
    return pl.pallas_call(
        functools.partial(_mm_kernel, n_pairs=len(pairs), epilogue=epilogue),
        grid=(t // tm, n // tn),
        in_specs=a_specs + b_specs + list(extra_specs),
        out_specs=pl.BlockSpec((tm, tn), lambda i, j: (i, j)),
        out_shape=jax.ShapeDtypeStruct((t, n), out_dtype),
        compiler_params=_cparams("parallel", "parallel"),
        name=name,
    )(*[a for a, _ in pairs], *[b for _, b in pairs], *extra)


def _matmul_residual(pairs, x, gate, tp, ls, name):
    t, d = x.shape
    tm = _tile(math.gcd(tp, ls), 1024)
    tn = _tile(d, 1024, 128)
    specs = [
        pl.BlockSpec((tm, tn), lambda i, j: (i, j)),
        pl.BlockSpec((1, 1, tn), lambda i, j: (_mod_row(i, tm, tp, ls), 0, j)),
    ]
    return _matmul(pairs, d, F32, tm, tn, _residual, (x, gate), specs, name)


def _split3(x):
    hi = x.astype(BF16)
    r1 = x - hi.astype(F32)
    mid = r1.astype(BF16)
    lo = (r1 - mid.astype(F32)).astype(BF16)
    return hi, mid, lo


def _hgrn_kernel(q_ref, f_ref, v_ref, lb_ref, s0_ref, o_ref, sfin_ref, st_ref, *, reverse, hb, c, dk):
    ci = pl.program_id(2)

    @pl.when(ci == 0)
    def _():
        for j in range(hb):
            st_ref[j] = s0_ref[0, j].T

    row = lax.broadcasted_iota(jnp.int32, (c, c), 0)
    col = lax.broadcasted_iota(jnp.int32, (c, c), 1)
    tri = jnp.where((col >= row) if reverse else (col <= row), 1.0, 0.0).astype(BF16)
    before = (col > row) if reverse else (col < row)
    diff = row ^ col
    trow = lax.broadcasted_iota(jnp.int32, (c, dk), 0)
    n_levels = int(math.log2(c))
    level = jnp.where(row == col, n_levels, -1)
    for h in range(n_levels):
        level = jnp.where(before & ((diff >> h) == 1), h, level)

    for j in range(hb):
        sl = slice(j * dk, (j + 1) * dk)
        q = _silu(q_ref[:, sl])
        lb = lb_ref[:, sl]
        f = lb + (1.0 - lb) * jax.nn.sigmoid(f_ref[:, sl])
        k = 1.0 - f
        v = v_ref[:, sl]
        vb = v.astype(BF16)
        hi, mid, lo = _split3(jnp.log(f))
        bcum = _dot(tri, hi) + _dot(tri, mid) + _dot(tri, lo)
        btot = bcum[0:1] if reverse else bcum[c - 1 : c]

        st = st_ref[j]
        o = _dot_nt((q * jnp.exp(bcum)).astype(BF16), st.astype(BF16))
        ke = (k * jnp.exp(btot - bcum)).astype(BF16)
        st_ref[j] = st * jnp.exp(btot) + _dot_tn(vb, ke)

        kb = k.astype(BF16)
        att = jnp.where(level == n_levels, _dot_nt(q.astype(BF16), kb), 0.0)
        f_prev = pltpu.roll(f, 1, 0)
        f_next = pltpu.roll(f, c - 1, 0)
        for h in range(n_levels):
            b = 1 << h
            if b == 1:
                x_l, y_l = (q * f).astype(BF16), kb
            else:
                if b == 2:
                    o4 = trow & 3
                    if reverse:
                        ex = jnp.where(o4 == 0, f * f_next, f)
                        ey = jnp.where(o4 == 3, f_prev, 1.0)
                    else:
                        ex = jnp.where(o4 == 3, f * f_prev, f)
                        ey = jnp.where(o4 == 0, f_next, 1.0)
                else:
                    b3 = bcum.reshape(c // (2 * b), 2 * b, dk)
                    edge = b if reverse else b - 1
                    ex = ey = jnp.exp(-jnp.abs(b3 - b3[:, edge : edge + 1, :])).reshape(c, dk)
                x_l, y_l = (q * ex).astype(BF16), (k * ey).astype(BF16)
            att = jnp.where(level == h, _dot_nt(x_l, y_l), att)
        o_ref[:, sl] = o + _dot(att.astype(BF16), vb)

    @pl.when(ci == pl.num_programs(2) - 1)
    def _():
        for j in range(hb):
            sfin_ref[0, j] = st_ref[j].T


def _hgrn(proj, lb, s0, row0, nb, l, d_a, f_section, reverse, prev=None):
    t = proj.shape[0]
    h, dk, dv = s0.shape[1:]
    c = min(HGRN_CHUNK, l)
    nc = l // c
    hb = next(k for k in (HGRN_HEADS_PER_STEP, 4, 2, 1) if h % k == 0)
    hw = hb * dk
    npb = d_a // hw

    def rows(b, ci):
        return row0 // c + b * nc + ((nc - 1 - ci) if reverse else ci)

    def in_spec(section):
        return pl.BlockSpec((c, hw), lambda b, g, ci: (rows(b, ci), section * npb + g))

    in_specs = [
        in_spec(0),
        in_spec(f_section),
        in_spec(3),
        pl.BlockSpec((1, hw), lambda b, g, ci: (0, g)),
        pl.BlockSpec((1, hb, dk, dv), lambda b, g, ci: (b, g, 0, 0)),
    ]
    args = [proj, proj, proj, lb.reshape(1, d_a), s0]
    kern = functools.partial(_hgrn_kernel, reverse=reverse, hb=hb, c=c, dk=dk)
    aliases = {}
    if prev is not None:
        in_specs.append(pl.BlockSpec(memory_space=pl.ANY))
        args.append(prev)
        aliases = {5: 0}
        kern = _drop_arg(kern, 5)
    return pl.pallas_call(
        kern,
        grid=(nb, npb, nc),
        in_specs=in_specs,
        out_specs=[
            pl.BlockSpec((c, hw), lambda b, g, ci: (rows(b, ci), g)),
            pl.BlockSpec((1, hb, dk, dv), lambda b, g, ci: (b, g, 0, 0)),
        ],
        out_shape=[jax.ShapeDtypeStruct((t, d_a), F32), jax.ShapeDtypeStruct((nb, h, dk, dv), F32)],
        scratch_shapes=[pltpu.VMEM((hb, dv, dk), F32)],
        input_output_aliases=aliases,
        compiler_params=_cparams("parallel", "parallel", "arbitrary"),
        name="hgrn_bwd" if reverse else "hgrn_fwd",
    )(*args)


def _hgrn_post_kernel(of_ref, ob_ref, g_ref, gn_ref, o_ref, *, nh, dv):
    for h in range(nh):
        sl = slice(h * dv, (h + 1) * dv)
        o = of_ref[:, sl] + ob_ref[:, sl]
        o = o * lax.rsqrt(jnp.mean(o * o, axis=-1, keepdims=True) + EPS) * gn_ref[:, sl]
        o_ref[:, sl] = (o * _silu(g_ref[:, sl])).astype(o_ref.dtype)


def _hgrn_post(o_fw, o_bw, proj, gnorm, nh, dv):
    t, d_a = o_fw.shape
    tm = _tile(t, 256)
    spec = pl.BlockSpec((tm, d_a), lambda i: (i, 0))
    return pl.pallas_call(
        functools.partial(_hgrn_post_kernel, nh=nh, dv=dv),
        grid=(t // tm,),
        in_specs=[spec, spec, pl.BlockSpec((tm, d_a), lambda i: (i, 4)), pl.BlockSpec((1, d_a), lambda i: (0, 0))],
        out_specs=spec,
        out_shape=jax.ShapeDtypeStruct((t, d_a), BF16),
        compiler_params=_cparams("parallel"),
        name="hgrn_post",
    )(o_fw, o_bw, proj, gnorm.reshape(1, d_a))


def _s5_tables(lam_re, lam_im, b_re, b_im, c_re, c_im, log_step, n_levels):
    hp = lax.Precision.HIGHEST
    tc = S5_CHUNK
    gb = S5_GB
    g, p, ch = b_re.shape[1:]
    nblk = g // gb
    lam = lax.complex(jnp.minimum(lam_re.astype(F32), -1e-4), lam_im.astype(F32))
    z = lam * jnp.exp(log_step.astype(F32))[..., None]
    b_bar = ((jnp.exp(z) - 1.0) / lam)[..., None] * lax.complex(b_re.astype(F32), b_im.astype(F32))
    cc = lax.complex(c_re.astype(F32), c_im.astype(F32))
    n = jnp.arange(tc + 1, dtype=F32)
    pw = jnp.exp(z[:, :, None, :] * n[None, None, :, None])

    kern = jnp.einsum("dgop,dglp,dgpi->dglio", cc, pw[:, :, :tc], b_bar, precision=hp).real
    lag_g = jnp.concatenate([kern[1][:, :0:-1], (kern[0][:, :1] + kern[1][:, :1]), kern[0][:, 1:]], axis=1)
    lagk = lag_g.reshape(nblk, gb, 2 * tc - 1, ch, ch).transpose(0, 2, 3, 1, 4).reshape(nblk, 2 * tc - 1, ch, gb * ch)

    inc_fw = (pw[0][:, tc - 1 - jnp.arange(tc), :, None] * b_bar[0][:, None]).transpose(0, 1, 3, 2)
    inc_bw = (pw[1][:, jnp.arange(tc), :, None] * b_bar[1][:, None]).transpose(0, 1, 3, 2)
    inc = jnp.concatenate([inc_fw.real, inc_fw.imag, inc_bw.real, inc_bw.imag], axis=-1)
    inct = inc.reshape(nblk, gb, tc, ch, 4 * p).transpose(0, 2, 3, 1, 4).reshape(nblk, tc, ch, gb * 4 * p)

    out_fw = cc[0][:, None] * pw[0][:, 1 + jnp.arange(tc), None, :]
    out_bw = cc[1][:, None] * pw[1][:, tc - jnp.arange(tc), None, :]
    out = jnp.concatenate([out_fw.real, -out_fw.imag, out_bw.real, -out_bw.imag], axis=-1)
    outt = out.reshape(nblk, gb, tc, ch, 4 * p).transpose(0, 2, 4, 1, 3).reshape(nblk, tc, 4 * p, gb * ch)

    steps = (tc * (2 ** jnp.arange(n_levels))).astype(F32)
    lp = jnp.exp(z[:, :, None, :] * steps[None, None, :, None])
    a, b = lp.real, lp.imag
    lev = jnp.stack(
        [jnp.concatenate([a, a], -1), jnp.concatenate([-b, b], -1), jnp.concatenate([b, -b], -1)], axis=3
    )
    lev = lev.transpose(1, 0, 2, 3, 4).reshape(g, 2 * n_levels * 3, 2 * p)
    pad = (-lev.shape[1]) % 8
    lev = jnp.pad(lev, ((0, 0), (0, pad), (0, 0)))
    return lagk, inct.astype(BF16), outt.astype(BF16), lev


def _s5_kernel(u_ref, lagk_ref, inct_ref, outt_ref, lev_ref, h0_ref, d_ref, *rest, nch, n_levels, max_levels, want_fin):
    if want_fin:
        o_ref, fin_ref, w1_ref, wt_ref, w2_ref, a_ref = rest
    else:
        o_ref, w1_ref, wt_ref, w2_ref, a_ref = rest
        fin_ref = None
    tc = S5_CHUNK
    gb = S5_GB
    lanes = u_ref.shape[1]
    ch = lanes // gb
    sw = lev_ref.shape[2]
    m = u_ref.shape[0] // tc
    nbk = m // nch
    b0 = pl.program_id(1) * nbk

    @pl.when(pl.program_id(1) == 0)
    def _():
        r1 = lax.broadcasted_iota(jnp.int32, (lanes, gb * 2 * sw), 0) // ch
        c1 = lax.broadcasted_iota(jnp.int32, (lanes, gb * 2 * sw), 1) // (2 * sw)
        for s in range(tc):
            w1_ref[s * lanes : (s + 1) * lanes, :] = jnp.where(r1 == c1, jnp.tile(inct_ref[0, s], (gb, 1)), 0.0)
        rd = lax.broadcasted_iota(jnp.int32, (lanes, lanes), 0) // ch
        cd = lax.broadcasted_iota(jnp.int32, (lanes, lanes), 1) // ch
        taps = [jnp.where(rd == cd, jnp.tile(lagk_ref[0, li], (gb, 1)), 0.0).astype(BF16) for li in range(2 * tc - 1)]
        for s in range(tc):
            for t in range(tc):
                wt_ref[s * lanes : (s + 1) * lanes, t * lanes : (t + 1) * lanes] = taps[t - s + tc - 1]
        r2 = lax.broadcasted_iota(jnp.int32, (gb * 2 * sw, lanes), 0) // (2 * sw)
        c2 = lax.broadcasted_iota(jnp.int32, (gb * 2 * sw, lanes), 1) // ch
        for t in range(tc):
            w2_ref[:, t * lanes : (t + 1) * lanes] = jnp.where(r2 == c2, jnp.tile(outt_ref[0, t], (gb, 1)), 0.0)

    xs_f32 = [u_ref[pl.ds(s, m, stride=tc), :] for s in range(tc)]
    xcat = jnp.concatenate([x.astype(BF16) for x in xs_f32], axis=1)
    inc = _dot(xcat, w1_ref[...])

    ridx = lax.broadcasted_iota(jnp.int32, (m, sw), 0)
    cidx = ridx % nch
    bidx = ridx // nch
    carry = []
    for g in range(gb):
        for d in range(2):
            fwd = d == 0
            x = inc[:, (2 * g + d) * sw : (2 * g + d + 1) * sw]
            xr = pltpu.roll(x, sw // 2, 1)
            h0 = jnp.zeros((m, sw), F32)
            h0r = jnp.zeros((m, sw), F32)
            for b in range(nbk):
                h0 = jnp.where(bidx == b, h0_ref[g, 2 * d, pl.ds(b0 + b, 1), :], h0)
                h0r = jnp.where(bidx == b, h0_ref[g, 2 * d + 1, pl.ds(b0 + b, 1), :], h0r)

            def const(j, which, d=d, g=g):
                r = (d * max_levels + j) * 3 + which
                return lev_ref[g, r : r + 1, :]

            entry = (cidx == 0) if fwd else (cidx == nch - 1)
            x = x + jnp.where(entry, h0 * const(0, 0) + h0r * const(0, 1), 0.0)
            xr = xr + jnp.where(entry, h0r * const(0, 0) + h0 * const(0, 2), 0.0)
            for j in range(n_levels):
                k = 1 << j
                keep = (cidx >= k) if fwd else (cidx < nch - k)
                shift = k if fwd else m - k
                sx = jnp.where(keep, pltpu.roll(x, shift, 0), 0.0)
                sxr = jnp.where(keep, pltpu.roll(xr, shift, 0), 0.0)
                x, xr = x + const(j, 0) * sx + const(j, 1) * sxr, xr + const(j, 0) * sxr + const(j, 2) * sx
            carry.append(jnp.where(entry, h0, pltpu.roll(x, 1 if fwd else m - 1, 0)).astype(BF16))
            if want_fin:
                a_ref[...] = x
                fin_ref[g, d] = a_ref[pl.ds(nch - 1 if fwd else 0, nbk, stride=nch), :]
    y = _dot(xcat, wt_ref[...]) + _dot(jnp.concatenate(carry, axis=1), w2_ref[...])
    for t in range(tc):
        o_ref[pl.ds(t, m, stride=tc), :] = jax.nn.gelu(y[:, t * lanes : (t + 1) * lanes] + d_ref[...] * xs_f32[t])


def _s5(proj, u_col0, tables, h0, s5_d, row0, nb, l, max_levels, want_fin, prev=None):
    lagk, inct, outt, lev = tables
    t = proj.shape[0]
    tc, gb = S5_CHUNK, S5_GB
    nblk = lagk.shape[0]
    lanes = lagk.shape[3]
    g = nblk * gb
    sw = lev.shape[2]
    nch = l // tc
    n_levels = int(math.log2(nch))
    nbk = nb
    if not want_fin:
        nbk = max(k for k in range(1, nb + 1) if nb % k == 0 and row0 % (k * l) == 0 and (k == 1 or k * nch <= 256))
    m = nbk * nch
    rows = m * tc
    in_specs = [
        pl.BlockSpec((rows, lanes), lambda i, b: (row0 // rows + b, u_col0 // lanes + i)),
        pl.BlockSpec((1,) + lagk.shape[1:], lambda i, b: (i, 0, 0, 0)),
        pl.BlockSpec((1,) + inct.shape[1:], lambda i, b: (i, 0, 0, 0)),
        pl.BlockSpec((1,) + outt.shape[1:], lambda i, b: (i, 0, 0, 0)),
        pl.BlockSpec((gb,) + lev.shape[1:], lambda i, b: (i, 0, 0)),
        pl.BlockSpec((gb, 4, nb, sw), lambda i, b: (i, 0, 0, 0)),
        pl.BlockSpec((1, lanes), lambda i, b: (0, i)),
    ]
    args = [proj, lagk, inct, outt, lev, h0, s5_d.reshape(1, g * (lanes // gb))]
    out_specs = [pl.BlockSpec((rows, lanes), lambda i, b: (row0 // rows + b, i))]
    out_shape = [jax.ShapeDtypeStruct((t, g * (lanes // gb)), F32)]
    if want_fin:
        out_specs.append(pl.BlockSpec((gb, 2, nbk, sw), lambda i, b: (i, 0, b, 0)))
        out_shape.append(jax.ShapeDtypeStruct((g, 2, nb, sw), F32))
    kern = functools.partial(_s5_kernel, nch=nch, n_levels=n_levels, max_levels=max_levels, want_fin=want_fin)
    aliases = {}
    if prev is not None:
        in_specs.append(pl.BlockSpec(memory_space=pl.ANY))
        args.append(prev)
        aliases = {7: 0}
        kern = _drop_arg(kern, 7)
    out = pl.pallas_call(
        kern,
        grid=(nblk, nb // nbk),
        in_specs=in_specs,
        out_specs=out_specs,
        out_shape=out_shape,
        scratch_shapes=[
            pltpu.VMEM((tc * lanes, gb * 2 * sw), BF16),
            pltpu.VMEM((tc * lanes, tc * lanes), BF16),
            pltpu.VMEM((gb * 2 * sw, tc * lanes), BF16),
            pltpu.VMEM((m, sw), F32),
        ],
        input_output_aliases=aliases,
        compiler_params=_cparams("parallel", "arbitrary"),
        name="s5",
    )(*args)
    return (out[0], out[1]) if want_fin else (out[0], None)


def _dft_cos_sin(num, den, scale):
    ang = (2.0 * math.pi / den) * (num % den).astype(np.float64)
    return jnp.asarray(np.cos(ang) * scale, dtype=BF16), jnp.asarray(np.sin(ang) * scale, dtype=BF16)


def _fnet_pos_kernel(c_ref, s_ref, uv_ref, o_ref, *, gd):
    uv = uv_ref[...]
    o_ref[...] = (_dot(c_ref[...], uv[:, :gd]) - _dot(s_ref[...], uv[:, gd:])).astype(o_ref.dtype)


def _fnet_pos(cp, sp, uv, row0, nb, l, gd, prev=None):
    t = uv.shape[0]
    ng = uv.shape[1] // (2 * gd)
    const = pl.BlockSpec((l, l), lambda b, g: (0, 0))
    in_specs = [const, const, pl.BlockSpec((l, 2 * gd), lambda b, g: (row0 // l + b, g))]
    args = [cp, sp, uv]
    kern = functools.partial(_fnet_pos_kernel, gd=gd)
    aliases = {}
    if prev is not None:
        in_specs.append(pl.BlockSpec(memory_space=pl.ANY))
        args.append(prev)
        aliases = {3: 0}
        kern = _drop_arg(kern, 3)
    return pl.pallas_call(
        kern,
        grid=(nb, ng),
        in_specs=in_specs,
        out_specs=pl.BlockSpec((l, gd), lambda b, g: (row0 // l + b, g)),
        out_shape=jax.ShapeDtypeStruct((t, ng * gd), BF16),
        input_output_aliases=aliases,
        compiler_params=_cparams("parallel", "parallel"),
        name="fnet_pos",
    )(*args)


def _fourier_mix(xn, tp, lp, ls):
    t, d = xn.shape
    gd = d // FNET_GROUPS
    kk = np.arange(gd)
    cc, sc = _dft_cos_sin(kk[:, None] * kk[None, :], gd, gd**-0.5)
    cs = jnp.concatenate([cc, sc], axis=1)
    tm = _tile(t, 1024)
    uv = pl.pallas_call(
        functools.partial(_mm_kernel, n_pairs=1, epilogue=_plain),
        grid=(t // tm, FNET_GROUPS),
        in_specs=[pl.BlockSpec((tm, gd), lambda i, g: (i, g)), pl.BlockSpec((gd, 2 * gd), lambda i, g: (0, 0))],
        out_specs=pl.BlockSpec((tm, 2 * gd), lambda i, g: (i, g)),
        out_shape=jax.ShapeDtypeStruct((t, 2 * d), BF16),
        compiler_params=_cparams("parallel", "parallel"),
        name="fnet_chan",
    )(xn, cs)
    pp = np.arange(lp)
    cp_p, sp_p = _dft_cos_sin(pp[:, None] * pp[None, :], lp, lp**-0.5)
    ps = np.arange(ls)
    r, c = ps // GRID_W, ps % GRID_W
    rows = ls // GRID_W
    lcm = rows * GRID_W // math.gcd(rows, GRID_W)
    num = (lcm // rows) * (r[:, None] * r[None, :]) + (lcm // GRID_W) * (c[:, None] * c[None, :])
    cp_s, sp_s = _dft_cos_sin(num, lcm, ls**-0.5)
    z = _fnet_pos(cp_p, sp_p, uv, 0, tp // lp, lp, gd)
    return _fnet_pos(cp_s, sp_s, uv, tp, (t - tp) // ls, ls, gd, prev=z)


def _router_kernel(x_ref, w_ref, b_ref, o_ref, *, ng, npg):
    h = w_ref.shape[0] // 2
    x_lo, x_hi = _unpack_halves(_load_slabs(x_ref, (), o_ref.shape[0], h // LANES))
    logits = _dot(x_lo.astype(BF16), w_ref[:h, :]) + _dot(x_hi.astype(BF16), w_ref[h:, :]) + b_ref[...]
    lane = lax.broadcasted_iota(jnp.int32, logits.shape, 1).astype(F32)
    big = 1e9

    def first_max(vals):
        m = jnp.max(vals, axis=-1, keepdims=True)
        return m, jnp.min(jnp.where(vals == m, lane, big), axis=-1, keepdims=True)

    gl = jnp.where(lane < ng, logits, NEG)
    gmax, gidx = first_max(gl)
    gprob = 1.0 / jnp.sum(jnp.exp(gl - gmax), axis=-1, keepdims=True)
    lo = ng + gidx * npg
    el = jnp.where((lane >= lo) & (lane < lo + npg), logits, NEG)
    m1, i1 = first_max(el)
    m2, i2 = first_max(jnp.where(lane == i1, NEG, el))
    e = jnp.exp(m2 - m1)
    w1 = gprob / (1.0 + e)
    out = jnp.where(lane == 0, i1 - ng, jnp.where(lane == 1, i2 - ng, jnp.where(lane == 2, w1, jnp.where(lane == 3, w1 * e, 0.0))))
    o_ref[...] = out


def _router(xn, wr_g, br_g, wr_e, br_e):
    d = wr_g.shape[0]
    t = xn.shape[0] // (d // 2 // LANES)
    ng, ne = wr_g.shape[1], wr_e.shape[1]
    w = jnp.zeros((d, 128), F32).at[:, :ng].set(wr_g).at[:, ng : ng + ne].set(wr_e).astype(BF16)
    b = jnp.zeros((1, 128), F32).at[0, :ng].set(br_g).at[0, ng : ng + ne].set(br_e)
    tm = _tile(t, 512)
    return pl.pallas_call(
        functools.partial(_router_kernel, ng=ng, npg=ne // ng),
        grid=(t // tm,),
        in_specs=[pl.BlockSpec((tm * (d // 2 // LANES), LANES), lambda i: (i, 0)), pl.BlockSpec((d, 128), lambda i: (0, 0)), pl.BlockSpec((1, 128), lambda i: (0, 0))],
        out_specs=pl.BlockSpec((tm, 128), lambda i: (i, 0)),
        out_shape=jax.ShapeDtypeStruct((t, 128), F32),
        compiler_params=_cparams("parallel"),
        name="router",
    )(xn, w, b)


def _moe_plan(rout, n_exp, tm, nt):
    t = rout.shape[0]
    flat = rout[:, :2].astype(jnp.int32).T.reshape(-1)
    oh = (flat[:, None] == jnp.arange(n_exp)[None, :]).astype(jnp.int32)
    blk = _tile(2 * t, 128)
    c1 = jnp.cumsum(oh.reshape(-1, blk, n_exp), axis=1)
    tot = c1[:, -1]
    cs = (c1 + (jnp.cumsum(tot, axis=0) - tot)[:, None]).reshape(2 * t, n_exp)
    rank = jnp.sum(oh * cs, axis=1) - 1
    counts = cs[-1]
    tiles = (counts + tm - 1) // tm
    tile_end = jnp.cumsum(tiles)
    pos = (tile_end - tiles)[flat] * tm + rank
    row_token = jnp.zeros((nt * tm,), jnp.int32).at[pos].set(jnp.tile(jnp.arange(t, dtype=jnp.int32), 2))
    tile_ids = jnp.arange(nt)
    tile_expert = jnp.minimum(jnp.searchsorted(tile_end, tile_ids, side="right"), n_exp - 1).astype(jnp.int32)
    tile_valid = (tile_ids < tile_end[-1]).astype(jnp.int32)
    prev_expert = jnp.concatenate([jnp.full((1,), -1, jnp.int32), tile_expert[:-1]])
    tile_first = (tile_valid * (tile_expert != prev_expert)).astype(jnp.int32)
    owner = jnp.where(tiles > 0, jnp.arange(n_exp), n_exp)
    nxt = jnp.concatenate([lax.cummin(owner[::-1])[::-1][1:], jnp.full((1,), n_exp)])
    tile_next = jnp.where(nxt < n_exp, nxt, -1)[tile_expert].astype(jnp.int32)
    return row_token, pos.astype(jnp.int32), tile_expert, tile_valid, tile_first, tile_next


def _moe_kernel(te_ref, tv_ref, tf_ref, tn_ref, rt_ref, x_hbm, wg_hbm, wu_hbm, wd_hbm, y_hbm,
                xbuf, stg_g, stg_u, stg_d, wb_g, wb_u, wb_d, obuf, xs, acc, hs, sem_x, sem_w, sem_o, *, tm, layer):
    i = pl.program_id(0)
    nt = pl.num_programs(0)
    slot = i % 2
    d, f = stg_g.shape
    sl = d // 2 // LANES

    def issue_rows(tile, s, r0, r1):
        def body(r, carry):
            src = pl.multiple_of(rt_ref[tile * tm + r] * sl, sl)
            dst = pl.multiple_of(r * sl, sl)
            pltpu.make_async_copy(x_hbm.at[pl.ds(src, sl)], xbuf.at[s, pl.ds(dst, sl)], sem_x.at[s]).start()
            return carry

        lax.fori_loop(r0, r1, body, 0, unroll=8)

    def weight_copies(e):
        return (
            pltpu.make_async_copy(wg_hbm.at[layer, e], stg_g, sem_w.at[0]),
            pltpu.make_async_copy(wu_hbm.at[layer, e], stg_u, sem_w.at[1]),
            pltpu.make_async_copy(wd_hbm.at[layer, e], stg_d, sem_w.at[2]),
        )

    def out_copy(tile):
        rows = tm * sl
        return pltpu.make_async_copy(obuf, y_hbm.at[pl.ds(pl.multiple_of(tile * rows, rows), rows)], sem_o.at[0])

    def cast(src, dst, ck):
        def body(k, carry):
            r = pl.multiple_of(k * ck, ck)
            dst[pl.ds(r, ck), :] = src[pl.ds(r, ck), :].astype(BF16)
            return carry

        lax.fori_loop(0, src.shape[0] // ck, body, 0)

    @pl.when(i == 0)
    def _():
        issue_rows(0, 0, 0, tm)
        for cp in weight_copies(te_ref[0]):
            cp.start(priority=1)

    nxt = jnp.minimum(i + 1, nt - 1)
    gather_next = jnp.logical_and(i + 1 < nt, tv_ref[nxt] == 1)
    n_parts = 8
    part = tm // n_parts

    def issue_part(k):
        @pl.when(gather_next)
        def _():
            issue_rows(i + 1, 1 - slot, k * part, (k + 1) * part)

    prev_valid = jnp.logical_and(i > 0, tv_ref[jnp.maximum(i - 1, 0)] == 1)

    @pl.when(jnp.logical_and(prev_valid, tv_ref[i] == 0))
    def _():
        out_copy(i - 1).wait()

    @pl.when(tv_ref[i] == 1)
    def _():
        @pl.when(tf_ref[i] == 1)
        def _():
            for cp in weight_copies(0):
                cp.wait()
            cast(stg_g, wb_g, min(256, d))
            cast(stg_u, wb_u, min(256, d))
            cast(stg_d, wb_d, min(32, f))

            @pl.when(tn_ref[i] >= 0)
            def _():
                for cp in weight_copies(tn_ref[i]):
                    cp.start(priority=1)

        pltpu.make_async_copy(x_hbm.at[pl.ds(0, tm * sl)], xbuf.at[slot], sem_x.at[slot]).wait()
        x_lo, x_hi = _unpack_halves(_load_slabs(xbuf, (slot,), tm, sl))
        xs[0] = x_lo.astype(BF16)
        xs[1] = x_hi.astype(BF16)
        dh = d // 2
        issue_part(0)
        acc[0] = _dot(xs[0], wb_g[:dh, :])
        issue_part(1)
        acc[0] += _dot(xs[1], wb_g[dh:, :])
        issue_part(2)
        acc[1] = _dot(xs[0], wb_u[:dh, :])
        issue_part(3)
        hs[...] = (_silu(acc[0]) * (acc[1] + _dot(xs[1], wb_u[dh:, :]))).astype(BF16)

        @pl.when(prev_valid)
        def _():
            out_copy(i - 1).wait()

        n_out = n_parts - 4
        wc = dh // n_out
        for c in range(n_out):
            issue_part(4 + c)
            lo = _dot(hs[...], wb_d[:, c * wc : (c + 1) * wc])
            hi = _dot(hs[...], wb_d[:, dh + c * wc : dh + (c + 1) * wc])
            words = _pack_pair(lo, hi)
            for j in range(wc // LANES):
                obuf[pl.ds(c * (wc // LANES) + j, tm, stride=sl), :] = words[:, j * LANES : (j + 1) * LANES]
        out_copy(i).start()

        @pl.when(i == nt - 1)
        def _():
            out_copy(i).wait()


def _combine_kernel(pos_ref, y_hbm, x_ref, r_ref, g_ref, o_ref, buf, sem, *, tm, t):
    i = pl.program_id(0)
    n = pl.num_programs(0)
    slot = i % 2

    sl = buf.shape[2] // tm

    def issue(tile, s):
        def body(r, carry):
            dst = pl.multiple_of(r * sl, sl)
            for k in range(2):
                src = pl.multiple_of(pos_ref[k * t + tile * tm + r] * sl, sl)
                pltpu.make_async_copy(y_hbm.at[pl.ds(src, sl)], buf.at[s, k, pl.ds(dst, sl)], sem.at[s]).start(priority=k)
            return carry

        lax.fori_loop(0, tm, body, 0, unroll=4)

    @pl.when(i == 0)
    def _():
        issue(0, 0)

    @pl.when(i + 1 < n)
    def _():
        issue(i + 1, 1 - slot)

    for k in range(2):
        pltpu.make_async_copy(y_hbm.at[pl.ds(0, tm * sl)], buf.at[slot, k], sem.at[slot]).wait()
    w = r_ref[...]
    w1, w2 = w[:, 2:3], w[:, 3:4]
    a_lo, a_hi = _unpack_halves(_load_slabs(buf, (slot, 0), tm, sl))
    b_lo, b_hi = _unpack_halves(_load_slabs(buf, (slot, 1), tm, sl))
    dh = a_lo.shape[1]
    o_ref[:, :dh] = x_ref[:, :dh] + g_ref[0, :, :dh] * (w1 * a_lo + w2 * b_lo)
    o_ref[:, dh:] = x_ref[:, dh:] + g_ref[0, :, dh:] * (w1 * a_hi + w2 * b_hi)


def _moe(xn, x, gate, params, layer, tp, ls):
    wr_g, br_g, wr_e, br_e, w_gate, w_up, w_down = params
    t, d = x.shape
    n_exp, _, f = w_gate.shape[1:]
    rout = _router(xn, wr_g, br_g, wr_e, br_e)
    sl = d // 2 // LANES
    tm = 256 if t >= 4096 else 64
    nt = (2 * t + tm - 1) // tm + n_exp
    row_token, pos, tile_expert, tile_valid, tile_first, tile_next = _moe_plan(rout, n_exp, tm, nt)
    hbm = pl.BlockSpec(memory_space=pl.ANY)
    y = pl.pallas_call(
        functools.partial(_moe_kernel, tm=tm, layer=layer),
        grid_spec=pltpu.PrefetchScalarGridSpec(
            num_scalar_prefetch=5,
            grid=(nt,),
            in_specs=[hbm, hbm, hbm, hbm],
            out_specs=hbm,
            scratch_shapes=[
                pltpu.VMEM((2, tm * sl, LANES), jnp.uint32),
                pltpu.VMEM((d, f), F32),
                pltpu.VMEM((d, f), F32),
                pltpu.VMEM((f, d), F32),
                pltpu.VMEM((d, f), BF16),
                pltpu.VMEM((d, f), BF16),
                pltpu.VMEM((f, d), BF16),
                pltpu.VMEM((tm * sl, LANES), jnp.uint32),
                pltpu.VMEM((2, tm, d // 2), BF16),
                pltpu.VMEM((2, tm, f), F32),
                pltpu.VMEM((tm, f), BF16),
                pltpu.SemaphoreType.DMA((2,)),
                pltpu.SemaphoreType.DMA((3,)),
                pltpu.SemaphoreType.DMA((1,)),
            ],
        ),
        out_shape=jax.ShapeDtypeStruct((nt * tm * sl, LANES), jnp.uint32),
        compiler_params=_cparams("arbitrary"),
        name="moe_experts",
    )(tile_expert, tile_valid, tile_first, tile_next, row_token, xn, w_gate, w_up, w_down)

    tc = _tile(math.gcd(tp, ls), 128)
    return pl.pallas_call(
        functools.partial(_combine_kernel, tm=tc, t=t),
        grid_spec=pltpu.PrefetchScalarGridSpec(
            num_scalar_prefetch=1,
            grid=(t // tc,),
            in_specs=[
                pl.BlockSpec(memory_space=pl.ANY),
                pl.BlockSpec((tc, d), lambda i, p: (i, 0)),
                pl.BlockSpec((tc, 128), lambda i, p: (i, 0)),
                pl.BlockSpec((1, 1, d), lambda i, p: (_mod_row(i, tc, tp, ls), 0, 0)),
            ],
            out_specs=pl.BlockSpec((tc, d), lambda i, p: (i, 0)),
            scratch_shapes=[pltpu.VMEM((2, 2, tc * sl, LANES), jnp.uint32), pltpu.SemaphoreType.DMA((2,))],
        ),
        out_shape=jax.ShapeDtypeStruct((t, d), F32),
        compiler_params=_cparams("arbitrary"),
        name="moe_combine",
    )(pos, y, x, rout, gate)


def kernel(x_prompt, x_sample, state_hgrn, state_s5_re, state_s5_im, c, c_ctx, ada_w, ada_b, norm_mix, norm_ffn, norm_final, ab_w_in, ab_w_out, hgrn_lb_theta, hgrn_gnorm, s5_lam_re, s5_lam_im, s5_b_re, s5_b_im, s5_c_re, s5_c_im, s5_log_step, s5_d, s5_glu_w, s5_glu_b, fnet_w_out, moe_wr_group, moe_br_group, moe_wr_expert, moe_br_expert, moe_w_gate, moe_w_up, moe_w_down):
    bp, lp, d = x_prompt.shape
    bs, ls, _ = x_sample.shape
    tp, ts = bp * lp, bs * ls
    depth = ada_w.shape[0]
    nh, dk, dv = state_hgrn.shape[3:]
    d_a = nh * dk
    n_grp, s5_p = state_s5_re.shape[3:]
    d_b = s5_d.shape[1]
    nmod = 1 + bs

    x = jnp.concatenate([x_prompt.reshape(tp, d), x_sample.reshape(ts, d)], axis=0)
    c8 = jnp.zeros((8, d), F32).at[0].set(c_ctx).at[1:nmod].set(c)
    mod_all = _ada(c8, ada_w, ada_b)
    lb_all = jnp.cumsum(jax.nn.softmax(hgrn_lb_theta.astype(F32), axis=1), axis=1)
    max_levels = int(math.log2(max(lp, ls) // S5_CHUNK))

    new_h, new_re, new_im = [], [], []
    i_ab = i_c = 0
    for l in range(depth):
        mod = mod_all[l, :nmod].reshape(nmod, 6, 1, d).transpose(1, 0, 2, 3)
        xn = _norm_mod(x, norm_mix[l], mod[0], mod[1], tp, ls, False)
        if l % 2 == 0:
            tm = _tile(math.gcd(tp, ls), 1024)
            n_in = ab_w_in.shape[2]
            proj = _matmul([(xn, ab_w_in[i_ab].astype(BF16))], n_in, F32, tm, _tile(n_in, 1024, 128), name="proj_in")
            tables = _s5_tables(s5_lam_re[i_ab], s5_lam_im[i_ab], s5_b_re[i_ab], s5_b_im[i_ab], s5_c_re[i_ab], s5_c_im[i_ab], s5_log_step[i_ab], max_levels)
            o_dirs, yg, h_fin, x_fin = [None, None], None, None, None
            for row0, nb, ln, h0, x0 in (
                (0, bp, lp, jnp.zeros((bp, 2, nh, dk, dv), F32), jnp.zeros((bp, 2, n_grp, 2 * s5_p), F32)),
                (tp, bs, ls, state_hgrn[:, i_ab].astype(F32), jnp.concatenate([state_s5_re[:, i_ab], state_s5_im[:, i_ab]], axis=-1).astype(F32)),
            ):
                fins = []
                for dr in range(2):
                    o_dirs[dr], s_fin = _hgrn(proj, lb_all[dr, i_ab], h0[:, dr], row0, nb, ln, d_a, 1 + dr, dr == 1, prev=o_dirs[dr])
                    fins.append(s_fin)
                x0s = jnp.concatenate([x0[..., s5_p:], x0[..., :s5_p]], axis=-1)
                h0_s5 = jnp.stack([x0[:, 0], x0s[:, 0], x0[:, 1], x0s[:, 1]], axis=0).transpose(2, 0, 1, 3)
                yg, fin_s5 = _s5(proj, 5 * d_a, tables, h0_s5, s5_d[i_ab], row0, nb, ln, max_levels, row0 == 0, prev=yg)
                if row0 == 0:
                    h_fin = jnp.stack(fins, axis=1)
                    x_fin = fin_s5.transpose(2, 1, 0, 3)
            o_a = _hgrn_post(o_dirs[0], o_dirs[1], proj, hgrn_gnorm[i_ab], nh, dv)
            tn = _tile(d_b, 1024, 128)
            y_glu = _matmul(
                [(yg, s5_glu_w[i_ab].astype(BF16))], d_b, BF16, tm, tn, _glu, (yg, s5_glu_b[i_ab].reshape(1, d_b)),
                (pl.BlockSpec((tm, tn), lambda i, j: (i, j)), pl.BlockSpec((1, tn), lambda i, j: (0, j))), "s5_glu",
            )
            w_out = ab_w_out[i_ab].astype(BF16)
            x = _matmul_residual([(o_a, w_out[:d_a]), (y_glu, w_out[d_a:])], x, mod[2], tp, ls, "mix_out")
            new_h.append(h_fin)
            new_re.append(x_fin[..., :s5_p])
            new_im.append(x_fin[..., s5_p:])
            i_ab += 1
        else:
            z = _fourier_mix(xn, tp, lp, ls)
            x = _matmul_residual([(z, fnet_w_out[i_c].astype(BF16))], x, mod[2], tp, ls, "fnet_out")
            i_c += 1
        xn = _norm_mod(x, norm_ffn[l], mod[3], mod[4], tp, ls, True)
        moe_params = (moe_wr_group[l], moe_br_group[l], moe_wr_expert[l], moe_br_expert[l], moe_w_gate, moe_w_up, moe_w_down)
        x = _moe(xn, x, mod[5], moe_params, l, tp, ls)
    return (
        _rmsnorm(x, norm_final, 0, tp).reshape(bp, lp, d),
        _rmsnorm(x, norm_final, tp, ts).reshape(bs, ls, d),
        jnp.stack(new_h, axis=1),
        jnp.stack(new_re, axis=1),
        jnp.stack(new_im, axis=1),
    )
```

```python
import functools
import math

import jax
import jax.numpy as jnp
import numpy as np
from jax import lax
from jax.experimental import pallas as pl
from jax.experimental.pallas import tpu as pltpu

F32 = jnp.float32
BF16 = jnp.bfloat16
EPS = 1e-6
NEG = -1e30
V7X_VMEM_LIMIT_BYTES = 56 * 1024 * 1024
FNET_GROUPS = 8
GRID_W = 64
HGRN_CHUNK = 128
HGRN_HEADS_PER_STEP = 8
S5_CHUNK = 16
S5_GB = 8


def _cparams(*sem):
    return pltpu.CompilerParams(dimension_semantics=sem, vmem_limit_bytes=V7X_VMEM_LIMIT_BYTES)


def _tile(n, pref, mult=8):
    if n <= pref:
        return n
    for t in range(pref - pref % mult, 0, -mult):
        if n % t == 0:
            return t
    raise ValueError(f"no tile for {n}")


def _drop_arg(body, index):
    def wrapped(*refs):
        return body(*refs[:index], *refs[index + 1 :])

    return wrapped


def _dot(a, b):
    return jnp.dot(a, b, preferred_element_type=F32)


def _dot_nt(a, b):
    return lax.dot_general(a, b, (((1,), (1,)), ((), ())), preferred_element_type=F32)


def _dot_tn(a, b):
    return lax.dot_general(a, b, (((0,), (0,)), ((), ())), preferred_element_type=F32)


def _silu(x):
    return x * jax.nn.sigmoid(x)


def _ada_kernel(c_ref, w_ref, b_ref, o_ref):
    a = _silu(c_ref[...]).astype(BF16)
    o_ref[0] = _dot(a, w_ref[0].astype(BF16)) + b_ref[0]


def _ada(c8, ada_w, ada_b):
    depth, d, n = ada_w.shape
    tn = _tile(n, 1024, 128)
    return pl.pallas_call(
        _ada_kernel,
        grid=(depth, n // tn),
        in_specs=[
            pl.BlockSpec((8, d), lambda l, j: (0, 0)),
            pl.BlockSpec((1, d, tn), lambda l, j: (l, 0, j)),
            pl.BlockSpec((1, 1, tn), lambda l, j: (l, 0, j)),
        ],
        out_specs=pl.BlockSpec((1, 8, tn), lambda l, j: (l, 0, j)),
        out_shape=jax.ShapeDtypeStruct((depth, 8, n), F32),
        compiler_params=_cparams("parallel", "parallel"),
        name="ada",
    )(c8, ada_w, ada_b.reshape(depth, 1, n))


def _mod_row(i, tm, tp, ls):
    r0 = i * tm
    return jnp.where(r0 < tp, 0, 1 + (r0 - tp) // ls)


def _pack_halves(y):
    h = y.shape[1] // 2
    return _pack_pair(y[:, :h], y[:, h:])


def _pack_pair(lo, hi):
    def bf16_bits(v):
        return lax.bitcast_convert_type(v.astype(BF16).astype(F32), jnp.uint32)

    return (bf16_bits(lo) >> 16) | (bf16_bits(hi) & jnp.uint32(0xFFFF0000))


def _unpack_halves(w):
    lo = lax.bitcast_convert_type(w << 16, F32)
    hi = lax.bitcast_convert_type(w & jnp.uint32(0xFFFF0000), F32)
    return lo, hi


LANES = 128


def _store_slabs(ref, lead, words):
    m, w = words.shape
    s = w // LANES
    for j in range(s):
        ref[(*lead, pl.ds(j, m, stride=s), slice(None))] = words[:, j * LANES : (j + 1) * LANES]


def _load_slabs(ref, lead, m, s):
    return jnp.concatenate([ref[(*lead, pl.ds(j, m, stride=s), slice(None))] for j in range(s)], axis=1)


def _norm_mod_kernel(x_ref, g_ref, sh_ref, sc_ref, o_ref, *, packed):
    x = x_ref[...]
    y = x * lax.rsqrt(jnp.mean(x * x, axis=-1, keepdims=True) + EPS) * g_ref[...]
    y = y * (1.0 + sc_ref[0]) + sh_ref[0]
    if packed:
        _store_slabs(o_ref, (), _pack_halves(y))
    else:
        o_ref[...] = y.astype(o_ref.dtype)


def _norm_mod(x, g, shift, scale, tp, ls, packed):
    t, d = x.shape
    tm = _tile(math.gcd(tp, ls), 256)
    mod_spec = pl.BlockSpec((1, 1, d), lambda i: (_mod_row(i, tm, tp, ls), 0, 0))
    s = d // 2 // LANES
    out_spec = pl.BlockSpec((tm * s, LANES), lambda i: (i, 0)) if packed else pl.BlockSpec((tm, d), lambda i: (i, 0))
    out_shape = jax.ShapeDtypeStruct((t * s, LANES), jnp.uint32) if packed else jax.ShapeDtypeStruct((t, d), BF16)
    return pl.pallas_call(
        functools.partial(_norm_mod_kernel, packed=packed),
        grid=(t // tm,),
        in_specs=[pl.BlockSpec((tm, d), lambda i: (i, 0)), pl.BlockSpec((1, d), lambda i: (0, 0)), mod_spec, mod_spec],
        out_specs=out_spec,
        out_shape=out_shape,
        compiler_params=_cparams("parallel"),
        name="norm_mod",
    )(x, g.reshape(1, d), shift, scale)


def _rmsnorm_kernel(x_ref, g_ref, o_ref):
    x = x_ref[...]
    o_ref[...] = x * lax.rsqrt(jnp.mean(x * x, axis=-1, keepdims=True) + EPS) * g_ref[...]


def _rmsnorm(x, g, row0, n):
    d = x.shape[1]
    tm = _tile(math.gcd(row0, n) if row0 else n, 256)
    return pl.pallas_call(
        _rmsnorm_kernel,
        grid=(n // tm,),
        in_specs=[pl.BlockSpec((tm, d), lambda i: (row0 // tm + i, 0)), pl.BlockSpec((1, d), lambda i: (0, 0))],
        out_specs=pl.BlockSpec((tm, d), lambda i: (i, 0)),
        out_shape=jax.ShapeDtypeStruct((n, d), F32),
        compiler_params=_cparams("parallel"),
        name="final_norm",
    )(x, g.reshape(1, d))


def _mm_kernel(*refs, n_pairs, epilogue):
    o_ref = refs[-1]
    acc = None
    for a_ref, b_ref in zip(refs[:n_pairs], refs[n_pairs : 2 * n_pairs]):
        part = _dot(a_ref[...].astype(BF16), b_ref[...])
        acc = part if acc is None else acc + part
    o_ref[...] = epilogue(acc, *refs[2 * n_pairs : -1]).astype(o_ref.dtype)


def _plain(acc):
    return acc


def _residual(acc, x_ref, g_ref):
    return x_ref[...] + g_ref[0] * acc


def _glu(acc, y_ref, b_ref):
    return y_ref[...] * jax.nn.sigmoid(acc + b_ref[...])


def _matmul(pairs, n, out_dtype, tm, tn, epilogue=_plain, extra=(), extra_specs=(), name="matmul"):
    t = pairs[0][0].shape[0]
    a_specs = [pl.BlockSpec((tm, a.shape[1]), lambda i, j: (i, 0)) for a, _ in pairs]
    b_specs = [pl.BlockSpec((b.shape[0], tn), lambda i, j: (0, j)) for _, b in pairs]
    return pl.pallas_call(
        functools.partial(_mm_kernel, n_pairs=len(pairs), epilogue=epilogue),
        grid=(t // tm, n // tn),
        in_specs=a_specs + b_specs + list(extra_specs),
        out_specs=pl.BlockSpec((tm, tn), lambda i, j: (i, j)),
        out_shape=jax.ShapeDtypeStruct((t, n), out_dtype),
        compiler_params=_cparams("parallel", "parallel"),
        name=name,
    )(*[a for a, _ in pairs], *[b for _, b in pairs], *extra)


def _matmul_residual(pairs, x, gate, tp, ls, name):
    t, d = x.shape
    tm = _tile(math.gcd(tp, ls), 1024)
    tn = _tile(d, 1024, 128)
    specs = [
        pl.BlockSpec((tm, tn), lambda i, j: (i, j)),
        pl.BlockSpec((1, 1, tn), lambda i, j: (_mod_row(i, tm, tp, ls), 0, j)),
    ]
    return _matmul(pairs, d, F32, tm, tn, _residual, (x, gate), specs, name)


def _split3(x):
    hi = x.astype(BF16)
    r1 = x - hi.astype(F32)
    mid = r1.astype(BF16)
    lo = (r1 - mid.astype(F32)).astype(BF16)
    return hi, mid, lo


def _hgrn_kernel(q_ref, f_ref, v_ref, lb_ref, s0_ref, o_ref, sfin_ref, st_ref, *, reverse, hb, c, dk):
    ci = pl.program_id(2)

    @pl.when(ci == 0)
    def _():
        for j in range(hb):
            st_ref[j] = s0_ref[0, j].T

    row = lax.broadcasted_iota(jnp.int32, (c, c), 0)
    col = lax.broadcasted_iota(jnp.int32, (c, c), 1)
    tri = jnp.where((col >= row) if reverse else (col <= row), 1.0, 0.0).astype(BF16)
    before = (col > row) if reverse else (col < row)
    diff = row ^ col
    trow = lax.broadcasted_iota(jnp.int32, (c, dk), 0)
    n_levels = int(math.log2(c))
    level = jnp.where(row == col, n_levels, -1)
    for h in range(n_levels):
        level = jnp.where(before & ((diff >> h) == 1), h, level)

    for j in range(hb):
        sl = slice(j * dk, (j + 1) * dk)
        q = _silu(q_ref[:, sl])
        lb = lb_ref[:, sl]
        f = lb + (1.0 - lb) * jax.nn.sigmoid(f_ref[:, sl])
        k = 1.0 - f
        v = v_ref[:, sl]
        vb = v.astype(BF16)
        hi, mid, lo = _split3(jnp.log(f))
        bcum = _dot(tri, hi) + _dot(tri, mid) + _dot(tri, lo)
        btot = bcum[0:1] if reverse else bcum[c - 1 : c]

        st = st_ref[j]
        o = _dot_nt((q * jnp.exp(bcum)).astype(BF16), st.astype(BF16))
        ke = (k * jnp.exp(btot - bcum)).astype(BF16)
        st_ref[j] = st * jnp.exp(btot) + _dot_tn(vb, ke)

        kb = k.astype(BF16)
        att = jnp.where(level == n_levels, _dot_nt(q.astype(BF16), kb), 0.0)
        f_prev = pltpu.roll(f, 1, 0)
        f_next = pltpu.roll(f, c - 1, 0)
        for h in range(n_levels):
            b = 1 << h
            if b == 1:
                x_l, y_l = (q * f).astype(BF16), kb
            else:
                if b == 2:
                    o4 = trow & 3
                    if reverse:
                        ex = jnp.where(o4 == 0, f * f_next, f)
                        ey = jnp.where(o4 == 3, f_prev, 1.0)
                    else:
                        ex = jnp.where(o4 == 3, f * f_prev, f)
                        ey = jnp.where(o4 == 0, f_next, 1.0)
                else:
                    b3 = bcum.reshape(c // (2 * b), 2 * b, dk)
                    edge = b if reverse else b - 1
                    ex = ey = jnp.exp(-jnp.abs(b3 - b3[:, edge : edge + 1, :])).reshape(c, dk)
                x_l, y_l = (q * ex).astype(BF16), (k * ey).astype(BF16)
            att = jnp.where(level == h, _dot_nt(x_l, y_l), att)
        o_ref[:, sl] = o + _dot(att.astype(BF16), vb)

    @pl.when(ci == pl.num_programs(2) - 1)
    def _():
        for j in range(hb):
            sfin_ref[0, j] = st_ref[j].T


def _hgrn(proj, lb, s0, row0, nb, l, d_a, f_section, reverse, prev=None):
    t = proj.shape[0]
    h, dk, dv = s0.shape[1:]
    c = min(HGRN_CHUNK, l)
    nc = l // c
    hb = next(k for k in (HGRN_HEADS_PER_STEP, 4, 2, 1) if h % k == 0)
    hw = hb * dk
    npb = d_a // hw

    def rows(b, ci):
        return row0 // c + b * nc + ((nc - 1 - ci) if reverse else ci)

    def in_spec(section):
        return pl.BlockSpec((c, hw), lambda b, g, ci: (rows(b, ci), section * npb + g))

    in_specs = [
        in_spec(0),
        in_spec(f_section),
        in_spec(3),
        pl.BlockSpec((1, hw), lambda b, g, ci: (0, g)),
        pl.BlockSpec((1, hb, dk, dv), lambda b, g, ci: (b, g, 0, 0)),
    ]
    args = [proj, proj, proj, lb.reshape(1, d_a), s0]
    kern = functools.partial(_hgrn_kernel, reverse=reverse, hb=hb, c=c, dk=dk)
    aliases = {}
    if prev is not None:
        in_specs.append(pl.BlockSpec(memory_space=pl.ANY))
        args.append(prev)
        aliases = {5: 0}
        kern = _drop_arg(kern, 5)
    return pl.pallas_call(
        kern,
        grid=(nb, npb, nc),
        in_specs=in_specs,
        out_specs=[
            pl.BlockSpec((c, hw), lambda b, g, ci: (rows(b, ci), g)),
            pl.BlockSpec((1, hb, dk, dv), lambda b, g, ci: (b, g, 0, 0)),
        ],
        out_shape=[jax.ShapeDtypeStruct((t, d_a), F32), jax.ShapeDtypeStruct((nb, h, dk, dv), F32)],
        scratch_shapes=[pltpu.VMEM((hb, dv, dk), F32)],
        input_output_aliases=aliases,
        compiler_params=_cparams("parallel", "parallel", "arbitrary"),
        name="hgrn_bwd" if reverse else "hgrn_fwd",
    )(*args)


def _hgrn_post_kernel(of_ref, ob_ref, g_ref, gn_ref, o_ref, *, nh, dv):
    for h in range(nh):
        sl = slice(h * dv, (h + 1) * dv)
        o = of_ref[:, sl] + ob_ref[:, sl]
        o = o * lax.rsqrt(jnp.mean(o * o, axis=-1, keepdims=True) + EPS) * gn_ref[:, sl]
        o_ref[:, sl] = (o * _silu(g_ref[:, sl])).astype(o_ref.dtype)


def _hgrn_post(o_fw, o_bw, proj, gnorm, nh, dv):
    t, d_a = o_fw.shape
    tm = _tile(t, 256)
    spec = pl.BlockSpec((tm, d_a), lambda i: (i, 0))
    return pl.pallas_call(
        functools.partial(_hgrn_post_kernel, nh=nh, dv=dv),
        grid=(t // tm,),
        in_specs=[spec, spec, pl.BlockSpec((tm, d_a), lambda i: (i, 4)), pl.BlockSpec((1, d_a), lambda i: (0, 0))],
        out_specs=spec,
        out_shape=jax.ShapeDtypeStruct((t, d_a), BF16),
        compiler_params=_cparams("parallel"),
        name="hgrn_post",
    )(o_fw, o_bw, proj, gnorm.reshape(1, d_a))


def _s5_tables(lam_re, lam_im, b_re, b_im, c_re, c_im, log_step, n_levels):
    hp = lax.Precision.HIGHEST
    tc = S5_CHUNK
    gb = S5_GB
    g, p, ch = b_re.shape[1:]
    nblk = g // gb
    lam = lax.complex(jnp.minimum(lam_re.astype(F32), -1e-4), lam_im.astype(F32))
    z = lam * jnp.exp(log_step.astype(F32))[..., None]
    b_bar = ((jnp.exp(z) - 1.0) / lam)[..., None] * lax.complex(b_re.astype(F32), b_im.astype(F32))
    cc = lax.complex(c_re.astype(F32), c_im.astype(F32))
    n = jnp.arange(tc + 1, dtype=F32)
    pw = jnp.exp(z[:, :, None, :] * n[None, None, :, None])

    kern = jnp.einsum("dgop,dglp,dgpi->dglio", cc, pw[:, :, :tc], b_bar, precision=hp).real
    lag_g = jnp.concatenate([kern[1][:, :0:-1], (kern[0][:, :1] + kern[1][:, :1]), kern[0][:, 1:]], axis=1)
    lagk = lag_g.reshape(nblk, gb, 2 * tc - 1, ch, ch).transpose(0, 2, 3, 1, 4).reshape(nblk, 2 * tc - 1, ch, gb * ch)

    inc_fw = (pw[0][:, tc - 1 - jnp.arange(tc), :, None] * b_bar[0][:, None]).transpose(0, 1, 3, 2)
    inc_bw = (pw[1][:, jnp.arange(tc), :, None] * b_bar[1][:, None]).transpose(0, 1, 3, 2)
    inc = jnp.concatenate([inc_fw.real, inc_fw.imag, inc_bw.real, inc_bw.imag], axis=-1)
    inct = inc.reshape(nblk, gb, tc, ch, 4 * p).transpose(0, 2, 3, 1, 4).reshape(nblk, tc, ch, gb * 4 * p)

    out_fw = cc[0][:, None] * pw[0][:, 1 + jnp.arange(tc), None, :]
    out_bw = cc[1][:, None] * pw[1][:, tc - jnp.arange(tc), None, :]
    out = jnp.concatenate([out_fw.real, -out_fw.imag, out_bw.real, -out_bw.imag], axis=-1)
    outt = out.reshape(nblk, gb, tc, ch, 4 * p).transpose(0, 2, 4, 1, 3).reshape(nblk, tc, 4 * p, gb * ch)

    steps = (tc * (2 ** jnp.arange(n_levels))).astype(F32)
    lp = jnp.exp(z[:, :, None, :] * steps[None, None, :, None])
    a, b = lp.real, lp.imag
    lev = jnp.stack(
        [jnp.concatenate([a, a], -1), jnp.concatenate([-b, b], -1), jnp.concatenate([b, -b], -1)], axis=3
    )
    lev = lev.transpose(1, 0, 2, 3, 4).reshape(g, 2 * n_levels * 3, 2 * p)
    pad = (-lev.shape[1]) % 8
    lev = jnp.pad(lev, ((0, 0), (0, pad), (0, 0)))
    return lagk, inct.astype(BF16), outt.astype(BF16), lev


def _s5_kernel(u_ref, lagk_ref, inct_ref, outt_ref, lev_ref, h0_ref, d_ref, *rest, nch, n_levels, max_levels, want_fin):
    if want_fin:
        o_ref, fin_ref, w1_ref, wt_ref, w2_ref, a_ref = rest
    else:
        o_ref, w1_ref, wt_ref, w2_ref, a_ref = rest
        fin_ref = None
    tc = S5_CHUNK
    gb = S5_GB
    lanes = u_ref.shape[1]
    ch = lanes // gb
    sw = lev_ref.shape[2]
    m = u_ref.shape[0] // tc
    nbk = m // nch
    b0 = pl.program_id(1) * nbk

    @pl.when(pl.program_id(1) == 0)
    def _():
        r1 = lax.broadcasted_iota(jnp.int32, (lanes, gb * 2 * sw), 0) // ch
        c1 = lax.broadcasted_iota(jnp.int32, (lanes, gb * 2 * sw), 1) // (2 * sw)
        for s in range(tc):
            w1_ref[s * lanes : (s + 1) * lanes, :] = jnp.where(r1 == c1, jnp.tile(inct_ref[0, s], (gb, 1)), 0.0)
        rd = lax.broadcasted_iota(jnp.int32, (lanes, lanes), 0) // ch
        cd = lax.broadcasted_iota(jnp.int32, (lanes, lanes), 1) // ch
        taps = [jnp.where(rd == cd, jnp.tile(lagk_ref[0, li], (gb, 1)), 0.0).astype(BF16) for li in range(2 * tc - 1)]
        for s in range(tc):
            for t in range(tc):
                wt_ref[s * lanes : (s + 1) * lanes, t * lanes : (t + 1) * lanes] = taps[t - s + tc - 1]
        r2 = lax.broadcasted_iota(jnp.int32, (gb * 2 * sw, lanes), 0) // (2 * sw)
        c2 = lax.broadcasted_iota(jnp.int32, (gb * 2 * sw, lanes), 1) // ch
        for t in range(tc):
            w2_ref[:, t * lanes : (t + 1) * lanes] = jnp.where(r2 == c2, jnp.tile(outt_ref[0, t], (gb, 1)), 0.0)

    xs_f32 = [u_ref[pl.ds(s, m, stride=tc), :] for s in range(tc)]
    xcat = jnp.concatenate([x.astype(BF16) for x in xs_f32], axis=1)
    inc = _dot(xcat, w1_ref[...])

    ridx = lax.broadcasted_iota(jnp.int32, (m, sw), 0)
    cidx = ridx % nch
    bidx = ridx // nch
    carry = []
    for g in range(gb):
        for d in range(2):
            fwd = d == 0
            x = inc[:, (2 * g + d) * sw : (2 * g + d + 1) * sw]
            xr = pltpu.roll(x, sw // 2, 1)
            h0 = jnp.zeros((m, sw), F32)
            h0r = jnp.zeros((m, sw), F32)
            for b in range(nbk):
                h0 = jnp.where(bidx == b, h0_ref[g, 2 * d, pl.ds(b0 + b, 1), :], h0)
                h0r = jnp.where(bidx == b, h0_ref[g, 2 * d + 1, pl.ds(b0 + b, 1), :], h0r)

            def const(j, which, d=d, g=g):
                r = (d * max_levels + j) * 3 + which
                return lev_ref[g, r : r + 1, :]

            entry = (cidx == 0) if fwd else (cidx == nch - 1)
            x = x + jnp.where(entry, h0 * const(0, 0) + h0r * const(0, 1), 0.0)
            xr = xr + jnp.where(entry, h0r * const(0, 0) + h0 * const(0, 2), 0.0)
            for j in range(n_levels):
                k = 1 << j
                keep = (cidx >= k) if fwd else (cidx < nch - k)
                shift = k if fwd else m - k
                sx = jnp.where(keep, pltpu.roll(x, shift, 0), 0.0)
                sxr = jnp.where(keep, pltpu.roll(xr, shift, 0), 0.0)
                x, xr = x + const(j, 0) * sx + const(j, 1) * sxr, xr + const(j, 0) * sxr + const(j, 2) * sx
            carry.append(jnp.where(entry, h0, pltpu.roll(x, 1 if fwd else m - 1, 0)).astype(BF16))
            if want_fin:
                a_ref[...] = x
                fin_ref[g, d] = a_ref[pl.ds(nch - 1 if fwd else 0, nbk, stride=nch), :]
    y = _dot(xcat, wt_ref[...]) + _dot(jnp.concatenate(carry, axis=1), w2_ref[...])
    for t in range(tc):
        o_ref[pl.ds(t, m, stride=tc), :] = jax.nn.gelu(y[:, t * lanes : (t + 1) * lanes] + d_ref[...] * xs_f32[t])


def _s5(proj, u_col0, tables, h0, s5_d, row0, nb, l, max_levels, want_fin, prev=None):
    lagk, inct, outt, lev = tables
    t = proj.shape[0]
    tc, gb = S5_CHUNK, S5_GB
    nblk = lagk.shape[0]
    lanes = lagk.shape[3]
    g = nblk * gb
    sw = lev.shape[2]
    nch = l // tc
    n_levels = int(math.log2(nch))
    nbk = nb
    if not want_fin:
        nbk = max(k for k in range(1, nb + 1) if nb % k == 0 and row0 % (k * l) == 0 and (k == 1 or k * nch <= 256))
    m = nbk * nch
    rows = m * tc
    in_specs = [
        pl.BlockSpec((rows, lanes), lambda i, b: (row0 // rows + b, u_col0 // lanes + i)),
        pl.BlockSpec((1,) + lagk.shape[1:], lambda i, b: (i, 0, 0, 0)),
        pl.BlockSpec((1,) + inct.shape[1:], lambda i, b: (i, 0, 0, 0)),
        pl.BlockSpec((1,) + outt.shape[1:], lambda i, b: (i, 0, 0, 0)),
        pl.BlockSpec((gb,) + lev.shape[1:], lambda i, b: (i, 0, 0)),
        pl.BlockSpec((gb, 4, nb, sw), lambda i, b: (i, 0, 0, 0)),
        pl.BlockSpec((1, lanes), lambda i, b: (0, i)),
    ]
    args = [proj, lagk, inct, outt, lev, h0, s5_d.reshape(1, g * (lanes // gb))]
    out_specs = [pl.BlockSpec((rows, lanes), lambda i, b: (row0 // rows + b, i))]
    out_shape = [jax.ShapeDtypeStruct((t, g * (lanes // gb)), F32)]
    if want_fin:
        out_specs.append(pl.BlockSpec((gb, 2, nbk, sw), lambda i, b: (i, 0, b, 0)))
        out_shape.append(jax.ShapeDtypeStruct((g, 2, nb, sw), F32))
    kern = functools.partial(_s5_kernel, nch=nch, n_levels=n_levels, max_levels=max_levels, want_fin=want_fin)
    aliases = {}
    if prev is not None:
        in_specs.append(pl.BlockSpec(memory_space=pl.ANY))
        args.append(prev)
        aliases = {7: 0}
        kern = _drop_arg(kern, 7)
    out = pl.pallas_call(
        kern,
        grid=(nblk, nb // nbk),
        in_specs=in_specs,
        out_specs=out_specs,
        out_shape=out_shape,
        scratch_shapes=[
            pltpu.VMEM((tc * lanes, gb * 2 * sw), BF16),
            pltpu.VMEM((tc * lanes, tc * lanes), BF16),
            pltpu.VMEM((gb * 2 * sw, tc * lanes), BF16),
            pltpu.VMEM((m, sw), F32),
        ],
        input_output_aliases=aliases,
        compiler_params=_cparams("parallel", "arbitrary"),
        name="s5",
    )(*args)
    return (out[0], out[1]) if want_fin else (out[0], None)


def _dft_cos_sin(num, den, scale):
    ang = (2.0 * math.pi / den) * (num % den).astype(np.float64)
    return jnp.asarray(np.cos(ang) * scale, dtype=BF16), jnp.asarray(np.sin(ang) * scale, dtype=BF16)


def _fnet_pos_kernel(c_ref, s_ref, uv_ref, o_ref, *, gd):
    uv = uv_ref[...]
    o_ref[...] = (_dot(c_ref[...], uv[:, :gd]) - _dot(s_ref[...], uv[:, gd:])).astype(o_ref.dtype)


def _fnet_pos(cp, sp, uv, row0, nb, l, gd, prev=None):
    t = uv.shape[0]
    ng = uv.shape[1] // (2 * gd)
    const = pl.BlockSpec((l, l), lambda b, g: (0, 0))
    in_specs = [const, const, pl.BlockSpec((l, 2 * gd), lambda b, g: (row0 // l + b, g))]
    args = [cp, sp, uv]
    kern = functools.partial(_fnet_pos_kernel, gd=gd)
    aliases = {}
    if prev is not None:
        in_specs.append(pl.BlockSpec(memory_space=pl.ANY))
        args.append(prev)
        aliases = {3: 0}
        kern = _drop_arg(kern, 3)
    return pl.pallas_call(
        kern,
        grid=(nb, ng),
        in_specs=in_specs,
        out_specs=pl.BlockSpec((l, gd), lambda b, g: (row0 // l + b, g)),
        out_shape=jax.ShapeDtypeStruct((t, ng * gd), BF16),
        input_output_aliases=aliases,
        compiler_params=_cparams("parallel", "parallel"),
        name="fnet_pos",
    )(*args)


def _fourier_mix(xn, tp, lp, ls):
    t, d = xn.shape
    gd = d // FNET_GROUPS
    kk = np.arange(gd)
    cc, sc = _dft_cos_sin(kk[:, None] * kk[None, :], gd, gd**-0.5)
    cs = jnp.concatenate([cc, sc], axis=1)
    tm = _tile(t, 1024)
    uv = pl.pallas_call(
        functools.partial(_mm_kernel, n_pairs=1, epilogue=_plain),
        grid=(t // tm, FNET_GROUPS),
        in_specs=[pl.BlockSpec((tm, gd), lambda i, g: (i, g)), pl.BlockSpec((gd, 2 * gd), lambda i, g: (0, 0))],
        out_specs=pl.BlockSpec((tm, 2 * gd), lambda i, g: (i, g)),
        out_shape=jax.ShapeDtypeStruct((t, 2 * d), BF16),
        compiler_params=_cparams("parallel", "parallel"),
        name="fnet_chan",
    )(xn, cs)
    pp = np.arange(lp)
    cp_p, sp_p = _dft_cos_sin(pp[:, None] * pp[None, :], lp, lp**-0.5)
    ps = np.arange(ls)
    r, c = ps // GRID_W, ps % GRID_W
    rows = ls // GRID_W
    lcm = rows * GRID_W // math.gcd(rows, GRID_W)
    num = (lcm // rows) * (r[:, None] * r[None, :]) + (lcm // GRID_W) * (c[:, None] * c[None, :])
    cp_s, sp_s = _dft_cos_sin(num, lcm, ls**-0.5)
    z = _fnet_pos(cp_p, sp_p, uv, 0, tp // lp, lp, gd)
    return _fnet_pos(cp_s, sp_s, uv, tp, (t - tp) // ls, ls, gd, prev=z)


def _router_kernel(x_ref, w_ref, b_ref, o_ref, *, ng, npg):
    h = w_ref.shape[0] // 2
    x_lo, x_hi = _unpack_halves(_load_slabs(x_ref, (), o_ref.shape[0], h // LANES))
    logits = _dot(x_lo.astype(BF16), w_ref[:h, :]) + _dot(x_hi.astype(BF16), w_ref[h:, :]) + b_ref[...]
    lane = lax.broadcasted_iota(jnp.int32, logits.shape, 1).astype(F32)
    big = 1e9

    def first_max(vals):
        m = jnp.max(vals, axis=-1, keepdims=True)
        return m, jnp.min(jnp.where(vals == m, lane, big), axis=-1, keepdims=True)

    gl = jnp.where(lane < ng, logits, NEG)
    gmax, gidx = first_max(gl)
    gprob = 1.0 / jnp.sum(jnp.exp(gl - gmax), axis=-1, keepdims=True)
    lo = ng + gidx * npg
    el = jnp.where((lane >= lo) & (lane < lo + npg), logits, NEG)
    m1, i1 = first_max(el)
    m2, i2 = first_max(jnp.where(lane == i1, NEG, el))
    e = jnp.exp(m2 - m1)
    w1 = gprob / (1.0 + e)
    out = jnp.where(lane == 0, i1 - ng, jnp.where(lane == 1, i2 - ng, jnp.where(lane == 2, w1, jnp.where(lane == 3, w1 * e, 0.0))))
    o_ref[...] = out


def _router(xn, wr_g, br_g, wr_e, br_e):
    d = wr_g.shape[0]
    t = xn.shape[0] // (d // 2 // LANES)
    ng, ne = wr_g.shape[1], wr_e.shape[1]
    w = jnp.zeros((d, 128), F32).at[:, :ng].set(wr_g).at[:, ng : ng + ne].set(wr_e).astype(BF16)
    b = jnp.zeros((1, 128), F32).at[0, :ng].set(br_g).at[0, ng : ng + ne].set(br_e)
    tm = _tile(t, 512)
    return pl.pallas_call(
        functools.partial(_router_kernel, ng=ng, npg=ne // ng),
        grid=(t // tm,),
        in_specs=[pl.BlockSpec((tm * (d // 2 // LANES), LANES), lambda i: (i, 0)), pl.BlockSpec((d, 128), lambda i: (0, 0)), pl.BlockSpec((1, 128), lambda i: (0, 0))],
        out_specs=pl.BlockSpec((tm, 128), lambda i: (i, 0)),
        out_shape=jax.ShapeDtypeStruct((t, 128), F32),
        compiler_params=_cparams("parallel"),
        name="router",
    )(xn, w, b)


def _moe_plan(rout, n_exp, tm, nt):
    t = rout.shape[0]
    flat = rout[:, :2].astype(jnp.int32).T.reshape(-1)
    oh = (flat[:, None] == jnp.arange(n_exp)[None, :]).astype(jnp.int32)
    blk = _tile(2 * t, 128)
    c1 = jnp.cumsum(oh.reshape(-1, blk, n_exp), axis=1)
    tot = c1[:, -1]
    cs = (c1 + (jnp.cumsum(tot, axis=0) - tot)[:, None]).reshape(2 * t, n_exp)
    rank = jnp.sum(oh * cs, axis=1) - 1
    counts = cs[-1]
    tiles = (counts + tm - 1) // tm
    tile_end = jnp.cumsum(tiles)
    pos = (tile_end - tiles)[flat] * tm + rank
    row_token = jnp.zeros((nt * tm,), jnp.int32).at[pos].set(jnp.tile(jnp.arange(t, dtype=jnp.int32), 2))
    tile_ids = jnp.arange(nt)
    tile_expert = jnp.minimum(jnp.searchsorted(tile_end, tile_ids, side="right"), n_exp - 1).astype(jnp.int32)
    tile_valid = (tile_ids < tile_end[-1]).astype(jnp.int32)
    prev_expert = jnp.concatenate([jnp.full((1,), -1, jnp.int32), tile_expert[:-1]])
    tile_first = (tile_valid * (tile_expert != prev_expert)).astype(jnp.int32)
    owner = jnp.where(tiles > 0, jnp.arange(n_exp), n_exp)
    nxt = jnp.concatenate([lax.cummin(owner[::-1])[::-1][1:], jnp.full((1,), n_exp)])
    tile_next = jnp.where(nxt < n_exp, nxt, -1)[tile_expert].astype(jnp.int32)
    return row_token, pos.astype(jnp.int32), tile_expert, tile_valid, tile_first, tile_next


def _moe_kernel(te_ref, tv_ref, tf_ref, tn_ref, rt_ref, x_hbm, wg_hbm, wu_hbm, wd_hbm, y_hbm,
                xbuf, stg_g, stg_u, stg_d, wb_g, wb_u, wb_d, obuf, sem_x, sem_w, sem_o, *, tm, layer):
    i = pl.program_id(0)
    nt = pl.num_programs(0)
    slot = i % 2
    d, f = stg_g.shape
    sl = d // 2 // LANES

    def issue_rows(tile, s, r0, r1):
        def body(r, carry):
            src = pl.multiple_of(rt_ref[tile * tm + r] * sl, sl)
            dst = pl.multiple_of(r * sl, sl)
            pltpu.make_async_copy(x_hbm.at[pl.ds(src, sl)], xbuf.at[s, pl.ds(dst, sl)], sem_x.at[s]).start()
            return carry

        lax.fori_loop(r0, r1, body, 0, unroll=8)

    def weight_copies(e):
        return (
            pltpu.make_async_copy(wg_hbm.at[layer, e], stg_g, sem_w.at[0]),
            pltpu.make_async_copy(wu_hbm.at[layer, e], stg_u, sem_w.at[1]),
            pltpu.make_async_copy(wd_hbm.at[layer, e], stg_d, sem_w.at[2]),
        )

    def out_copy(tile):
        rows = tm * sl
        dst = y_hbm.at[pl.ds(pl.multiple_of(tile * rows, rows), rows)]
        return pltpu.make_async_copy(obuf.at[tile % 2], dst, sem_o.at[tile % 2])

    def cast(src, dst, ck):
        def body(k, carry):
            r = pl.multiple_of(k * ck, ck)
            dst[pl.ds(r, ck), :] = src[pl.ds(r, ck), :].astype(BF16)
            return carry

        lax.fori_loop(0, src.shape[0] // ck, body, 0)

    @pl.when(i == 0)
    def _():
        issue_rows(0, 0, 0, tm)
        for cp in weight_copies(te_ref[0]):
            cp.start(priority=1)

    nxt = jnp.minimum(i + 1, nt - 1)

    @pl.when(jnp.logical_and(i + 1 < nt, tv_ref[nxt] == 1))
    def _():
        issue_rows(i + 1, 1 - slot, 0, tm)

    @pl.when(jnp.logical_and(i >= 2, tv_ref[i] == 1))
    def _():
        out_copy(i - 2).wait()

    @pl.when(jnp.logical_and(tv_ref[i] == 0, jnp.logical_and(i >= 1, tv_ref[jnp.maximum(i - 1, 0)] == 1)))
    def _():
        out_copy(i - 1).wait()

        @pl.when(i >= 2)
        def _():
            out_copy(i - 2).wait()

    @pl.when(tv_ref[i] == 1)
    def _():
        @pl.when(tf_ref[i] == 1)
        def _():
            for cp in weight_copies(0):
                cp.wait()
            cast(stg_g, wb_g, min(256, d))
            cast(stg_u, wb_u, min(256, d))
            cast(stg_d, wb_d, min(32, f))

            @pl.when(tn_ref[i] >= 0)
            def _():
                for cp in weight_copies(tn_ref[i]):
                    cp.start(priority=1)

        pltpu.make_async_copy(x_hbm.at[pl.ds(0, tm * sl)], xbuf.at[slot], sem_x.at[slot]).wait()
        x_lo, x_hi = (v.astype(BF16) for v in _unpack_halves(_load_slabs(xbuf, (slot,), tm, sl)))
        dh = d // 2
        hg = _dot(x_lo, wb_g[:dh, :]) + _dot(x_hi, wb_g[dh:, :])
        hu = _dot(x_lo, wb_u[:dh, :]) + _dot(x_hi, wb_u[dh:, :])
        h = (_silu(hg) * hu).astype(BF16)
        _store_slabs(obuf, (slot,), _pack_pair(_dot(h, wb_d[:, :dh]), _dot(h, wb_d[:, dh:])))
        out_copy(i).start()

        @pl.when(i == nt - 1)
        def _():

            @pl.when(i >= 1)
            def _():
                out_copy(i - 1).wait()

            out_copy(i).wait()


def _combine_kernel(pos_ref, y_hbm, x_ref, r_ref, g_ref, o_ref, buf, sem, *, tm, t):
    i = pl.program_id(0)
    n = pl.num_programs(0)
    slot = i % 2

    sl = buf.shape[2] // tm

    def issue(tile, s):
        def body(r, carry):
            dst = pl.multiple_of(r * sl, sl)
            for k in range(2):
                src = pl.multiple_of(pos_ref[k * t + tile * tm + r] * sl, sl)
                pltpu.make_async_copy(y_hbm.at[pl.ds(src, sl)], buf.at[s, k, pl.ds(dst, sl)], sem.at[s]).start()
            return carry

        lax.fori_loop(0, tm, body, 0, unroll=4)

    @pl.when(i == 0)
    def _():
        issue(0, 0)

    @pl.when(i + 1 < n)
    def _():
        issue(i + 1, 1 - slot)

    for k in range(2):
        pltpu.make_async_copy(y_hbm.at[pl.ds(0, tm * sl)], buf.at[slot, k], sem.at[slot]).wait()
    w = r_ref[...]
    w1, w2 = w[:, 2:3], w[:, 3:4]
    a_lo, a_hi = _unpack_halves(_load_slabs(buf, (slot, 0), tm, sl))
    b_lo, b_hi = _unpack_halves(_load_slabs(buf, (slot, 1), tm, sl))
    dh = a_lo.shape[1]
    o_ref[:, :dh] = x_ref[:, :dh] + g_ref[0, :, :dh] * (w1 * a_lo + w2 * b_lo)
    o_ref[:, dh:] = x_ref[:, dh:] + g_ref[0, :, dh:] * (w1 * a_hi + w2 * b_hi)


def _moe(xn, x, gate, params, layer, tp, ls):
    wr_g, br_g, wr_e, br_e, w_gate, w_up, w_down = params
    t, d = x.shape
    n_exp, _, f = w_gate.shape[1:]
    rout = _router(xn, wr_g, br_g, wr_e, br_e)
    sl = d // 2 // LANES
    tm = 256 if t >= 4096 else 64
    nt = (2 * t + tm - 1) // tm + n_exp
    row_token, pos, tile_expert, tile_valid, tile_first, tile_next = _moe_plan(rout, n_exp, tm, nt)
    hbm = pl.BlockSpec(memory_space=pl.ANY)
    y = pl.pallas_call(
        functools.partial(_moe_kernel, tm=tm, layer=layer),
        grid_spec=pltpu.PrefetchScalarGridSpec(
            num_scalar_prefetch=5,
            grid=(nt,),
            in_specs=[hbm, hbm, hbm, hbm],
            out_specs=hbm,
            scratch_shapes=[
                pltpu.VMEM((2, tm * sl, LANES), jnp.uint32),
                pltpu.VMEM((d, f), F32),
                pltpu.VMEM((d, f), F32),
                pltpu.VMEM((f, d), F32),
                pltpu.VMEM((d, f), BF16),
                pltpu.VMEM((d, f), BF16),
                pltpu.VMEM((f, d), BF16),
                pltpu.VMEM((2, tm * sl, LANES), jnp.uint32),
                pltpu.SemaphoreType.DMA((2,)),
                pltpu.SemaphoreType.DMA((3,)),
                pltpu.SemaphoreType.DMA((2,)),
            ],
        ),
        out_shape=jax.ShapeDtypeStruct((nt * tm * sl, LANES), jnp.uint32),
        compiler_params=_cparams("arbitrary"),
        name="moe_experts",
    )(tile_expert, tile_valid, tile_first, tile_next, row_token, xn, w_gate, w_up, w_down)

    tc = _tile(math.gcd(tp, ls), 128)
    return pl.pallas_call(
        functools.partial(_combine_kernel, tm=tc, t=t),
        grid_spec=pltpu.PrefetchScalarGridSpec(
            num_scalar_prefetch=1,
            grid=(t // tc,),
            in_specs=[
                pl.BlockSpec(memory_space=pl.ANY),
                pl.BlockSpec((tc, d), lambda i, p: (i, 0)),
                pl.BlockSpec((tc, 128), lambda i, p: (i, 0)),
                pl.BlockSpec((1, 1, d), lambda i, p: (_mod_row(i, tc, tp, ls), 0, 0)),
            ],
            out_specs=pl.BlockSpec((tc, d), lambda i, p: (i, 0)),
            scratch_shapes=[pltpu.VMEM((2, 2, tc * sl, LANES), jnp.uint32), pltpu.SemaphoreType.DMA((2,))],
        ),
        out_shape=jax.ShapeDtypeStruct((t, d), F32),
        compiler_params=_cparams("arbitrary"),
        name="moe_combine",
    )(pos, y, x, rout, gate)


def kernel(x_prompt, x_sample, state_hgrn, state_s5_re, state_s5_im, c, c_ctx, ada_w, ada_b, norm_mix, norm_ffn, norm_final, ab_w_in, ab_w_out, hgrn_lb_theta, hgrn_gnorm, s5_lam_re, s5_lam_im, s5_b_re, s5_b_im, s5_c_re, s5_c_im, s5_log_step, s5_d, s5_glu_w, s5_glu_b, fnet_w_out, moe_wr_group, moe_br_group, moe_wr_expert, moe_br_expert, moe_w_gate, moe_w_up, moe_w_down):
    bp, lp, d = x_prompt.shape
    bs, ls, _ = x_sample.shape
    tp, ts = bp * lp, bs * ls
    depth = ada_w.shape[0]
    nh, dk, dv = state_hgrn.shape[3:]
    d_a = nh * dk
    n_grp, s5_p = state_s5_re.shape[3:]
    d_b = s5_d.shape[1]
    nmod = 1 + bs

    x = jnp.concatenate([x_prompt.reshape(tp, d), x_sample.reshape(ts, d)], axis=0)
    c8 = jnp.zeros((8, d), F32).at[0].set(c_ctx).at[1:nmod].set(c)
    mod_all = _ada(c8, ada_w, ada_b)
    lb_all = jnp.cumsum(jax.nn.softmax(hgrn_lb_theta.astype(F32), axis=1), axis=1)
    max_levels = int(math.log2(max(lp, ls) // S5_CHUNK))

    new_h, new_re, new_im = [], [], []
    i_ab = i_c = 0
    for l in range(depth):
        mod = mod_all[l, :nmod].reshape(nmod, 6, 1, d).transpose(1, 0, 2, 3)
        xn = _norm_mod(x, norm_mix[l], mod[0], mod[1], tp, ls, False)
        if l % 2 == 0:
            tm = _tile(math.gcd(tp, ls), 1024)
            n_in = ab_w_in.shape[2]
            proj = _matmul([(xn, ab_w_in[i_ab].astype(BF16))], n_in, F32, tm, _tile(n_in, 1024, 128), name="proj_in")
            tables = _s5_tables(s5_lam_re[i_ab], s5_lam_im[i_ab], s5_b_re[i_ab], s5_b_im[i_ab], s5_c_re[i_ab], s5_c_im[i_ab], s5_log_step[i_ab], max_levels)
            o_dirs, yg, h_fin, x_fin = [None, None], None, None, None
            for row0, nb, ln, h0, x0 in (
                (0, bp, lp, jnp.zeros((bp, 2, nh, dk, dv), F32), jnp.zeros((bp, 2, n_grp, 2 * s5_p), F32)),
                (tp, bs, ls, state_hgrn[:, i_ab].astype(F32), jnp.concatenate([state_s5_re[:, i_ab], state_s5_im[:, i_ab]], axis=-1).astype(F32)),
            ):
                fins = []
                for dr in range(2):
                    o_dirs[dr], s_fin = _hgrn(proj, lb_all[dr, i_ab], h0[:, dr], row0, nb, ln, d_a, 1 + dr, dr == 1, prev=o_dirs[dr])
                    fins.append(s_fin)
                x0s = jnp.concatenate([x0[..., s5_p:], x0[..., :s5_p]], axis=-1)
                h0_s5 = jnp.stack([x0[:, 0], x0s[:, 0], x0[:, 1], x0s[:, 1]], axis=0).transpose(2, 0, 1, 3)
                yg, fin_s5 = _s5(proj, 5 * d_a, tables, h0_s5, s5_d[i_ab], row0, nb, ln, max_levels, row0 == 0, prev=yg)
                if row0 == 0:
                    h_fin = jnp.stack(fins, axis=1)
                    x_fin = fin_s5.transpose(2, 1, 0, 3)
            o_a = _hgrn_post(o_dirs[0], o_dirs[1], proj, hgrn_gnorm[i_ab], nh, dv)
            tn = _tile(d_b, 1024, 128)
            y_glu = _matmul(
                [(yg, s5_glu_w[i_ab].astype(BF16))], d_b, BF16, tm, tn, _glu, (yg, s5_glu_b[i_ab].reshape(1, d_b)),
                (pl.BlockSpec((tm, tn), lambda i, j: (i, j)), pl.BlockSpec((1, tn), lambda i, j: (0, j))), "s5_glu",
            )
            w_out = ab_w_out[i_ab].astype(BF16)
            x = _matmul_residual([(o_a, w_out[:d_a]), (y_glu, w_out[d_a:])], x, mod[2], tp, ls, "mix_out")
            new_h.append(h_fin)
            new_re.append(x_fin[..., :s5_p])
            new_im.append(x_fin[..., s5_p:])
            i_ab += 1
        else:
            z = _fourier_mix(xn, tp, lp, ls)
            x = _matmul_residual([(z, fnet_w_out[i_c].astype(BF16))], x, mod[2], tp, ls, "fnet_out")
            i_c += 1
        xn = _norm_mod(x, norm_ffn[l], mod[3], mod[4], tp, ls, True)
        moe_params = (moe_wr_group[l], moe_br_group[l], moe_wr_expert[l], moe_br_expert[l], moe_w_gate, moe_w_up, moe_w_down)
        x = _moe(xn, x, mod[5], moe_params, l, tp, ls)
    return (
        _rmsnorm(x, norm_final, 0, tp).reshape(bp, lp, d),
        _rmsnorm(x, norm_final, tp, ts).reshape(bs, ls, d),
        jnp.stack(new_h, axis=1),
        jnp.stack(new_re, axis=1),
        jnp.stack(new_im, axis=1),
    )
```

```python
import functools
import math

import jax
import jax.numpy as jnp
import numpy as np
from jax import lax
from jax.experimental import pallas as pl
from jax.experimental.pallas import tpu as pltpu

F32 = jnp.float32
BF16 = jnp.bfloat16
EPS = 1e-6
NEG = -1e30
V7X_VMEM_LIMIT_BYTES = 56 * 1024 * 1024
FNET_GROUPS = 8
GRID_W = 64
HGRN_CHUNK = 128
HGRN_HEADS_PER_STEP = 8
S5_CHUNK = 16
S5_GB = 8


def _cparams(*sem):
    return pltpu.CompilerParams(dimension_semantics=sem, vmem_limit_bytes=V7X_VMEM_LIMIT_BYTES)


def _tile(n, pref, mult=8):
    if n <= pref:
        return n
    for t in range(pref - pref % mult, 0, -mult):
        if n % t == 0:
            return t
    raise ValueError(f"no tile for {n}")


def _drop_arg(body, index):
    def wrapped(*refs):
        return body(*refs[:index], *refs[index + 1 :])

    return wrapped


def _dot(a, b):
    return jnp.dot(a, b, preferred_element_type=F32)


def _dot_nt(a, b):
    return lax.dot_general(a, b, (((1,), (1,)), ((), ())), preferred_element_type=F32)


def _dot_tn(a, b):
    return lax.dot_general(a, b, (((0,), (0,)), ((), ())), preferred_element_type=F32)


def _silu(x):
    return x * jax.nn.sigmoid(x)


def _ada_kernel(c_ref, w_ref, b_ref, o_ref):
    a = _silu(c_ref[...]).astype(BF16)
    o_ref[0] = _dot(a, w_ref[0].astype(BF16)) + b_ref[0]


def _ada(c8, ada_w, ada_b):
    depth, d, n = ada_w.shape
    tn = _tile(n, 1024, 128)
    return pl.pallas_call(
        _ada_kernel,
        grid=(depth, n // tn),
        in_specs=[
            pl.BlockSpec((8, d), lambda l, j: (0, 0)),
            pl.BlockSpec((1, d, tn), lambda l, j: (l, 0, j)),
            pl.BlockSpec((1, 1, tn), lambda l, j: (l, 0, j)),
        ],
        out_specs=pl.BlockSpec((1, 8, tn), lambda l, j: (l, 0, j)),
        out_shape=jax.ShapeDtypeStruct((depth, 8, n), F32),
        compiler_params=_cparams("parallel", "parallel"),
        name="ada",
    )(c8, ada_w, ada_b.reshape(depth, 1, n))


def _mod_row(i, tm, tp, ls):
    r0 = i * tm
    return jnp.where(r0 < tp, 0, 1 + (r0 - tp) // ls)


def _pack_halves(y):
    h = y.shape[1] // 2
    return _pack_pair(y[:, :h], y[:, h:])


def _pack_pair(lo, hi):
    def bf16_bits(v):
        return lax.bitcast_convert_type(v.astype(BF16).astype(F32), jnp.uint32)

    return (bf16_bits(lo) >> 16) | (bf16_bits(hi) & jnp.uint32(0xFFFF0000))


def _unpack_halves(w):
    lo = lax.bitcast_convert_type(w << 16, F32)
    hi = lax.bitcast_convert_type(w & jnp.uint32(0xFFFF0000), F32)
    return lo, hi


LANES = 128


def _store_slabs(ref, lead, words):
    m, w = words.shape
    s = w // LANES
    for j in range(s):
        ref[(*lead, pl.ds(j, m, stride=s), slice(None))] = words[:, j * LANES : (j + 1) * LANES]


def _load_slabs(ref, lead, m, s):
    return jnp.concatenate([ref[(*lead, pl.ds(j, m, stride=s), slice(None))] for j in range(s)], axis=1)


def _norm_mod_kernel(x_ref, g_ref, sh_ref, sc_ref, o_ref, *, packed):
    x = x_ref[...]
    y = x * lax.rsqrt(jnp.mean(x * x, axis=-1, keepdims=True) + EPS) * g_ref[...]
    y = y * (1.0 + sc_ref[0]) + sh_ref[0]
    if packed:
        _store_slabs(o_ref, (), _pack_halves(y))
    else:
        o_ref[...] = y.astype(o_ref.dtype)


def _norm_mod(x, g, shift, scale, tp, ls, packed):
    t, d = x.shape
    tm = _tile(math.gcd(tp, ls), 256)
    mod_spec = pl.BlockSpec((1, 1, d), lambda i: (_mod_row(i, tm, tp, ls), 0, 0))
    s = d // 2 // LANES
    out_spec = pl.BlockSpec((tm * s, LANES), lambda i: (i, 0)) if packed else pl.BlockSpec((tm, d), lambda i: (i, 0))
    out_shape = jax.ShapeDtypeStruct((t * s, LANES), jnp.uint32) if packed else jax.ShapeDtypeStruct((t, d), BF16)
    return pl.pallas_call(
        functools.partial(_norm_mod_kernel, packed=packed),
        grid=(t // tm,),
        in_specs=[pl.BlockSpec((tm, d), lambda i: (i, 0)), pl.BlockSpec((1, d), lambda i: (0, 0)), mod_spec, mod_spec],
        out_specs=out_spec,
        out_shape=out_shape,
        compiler_params=_cparams("parallel"),
        name="norm_mod",
    )(x, g.reshape(1, d), shift, scale)


def _rmsnorm_kernel(x_ref, g_ref, o_ref):
    x = x_ref[...]
    o_ref[...] = x * lax.rsqrt(jnp.mean(x * x, axis=-1, keepdims=True) + EPS) * g_ref[...]


def _rmsnorm(x, g, row0, n):
    d = x.shape[1]
    tm = _tile(math.gcd(row0, n) if row0 else n, 256)
    return pl.pallas_call(
        _rmsnorm_kernel,
        grid=(n // tm,),
        in_specs=[pl.BlockSpec((tm, d), lambda i: (row0 // tm + i, 0)), pl.BlockSpec((1, d), lambda i: (0, 0))],
        out_specs=pl.BlockSpec((tm, d), lambda i: (i, 0)),
        out_shape=jax.ShapeDtypeStruct((n, d), F32),
        compiler_params=_cparams("parallel"),
        name="final_norm",
    )(x, g.reshape(1, d))


def _mm_kernel(*refs, n_pairs, epilogue):
    o_ref = refs[-1]
    acc = None
    for a_ref, b_ref in zip(refs[:n_pairs], refs[n_pairs : 2 * n_pairs]):
        part = _dot(a_ref[...].astype(BF16), b_ref[...])
        acc = part if acc is None else acc + part
    o_ref[...] = epilogue(acc, *refs[2 * n_pairs : -1]).astype(o_ref.dtype)


def _plain(acc):
    return acc


def _residual(acc, x_ref, g_ref):
    return x_ref[...] + g_ref[0] * acc


def _glu(acc, y_ref, b_ref):
    return y_ref[...] * jax.nn.sigmoid(acc + b_ref[...])


def _matmul(pairs, n, out_dtype, tm, tn, epilogue=_plain, extra=(), extra_specs=(), name="matmul"):
    t = pairs[0][0].shape[0]
    a_specs = [pl.BlockSpec((tm, a.shape[1]), lambda i, j: (i, 0)) for a, _ in pairs]
    b_specs = [pl.BlockSpec((b.shape[0], tn), lambda i, j: (0, j)) for _, b in pairs]
    return pl.pallas_call(
        functools.partial(_mm_kernel, n_pairs=len(pairs), epilogue=epilogue),
        grid=(t // tm, n // tn),
        in_specs=a_specs + b_specs + list(extra_specs),
        out_specs=pl.BlockSpec((tm, tn), lambda i, j: (i, j)),
        out_shape=jax.ShapeDtypeStruct((t, n), out_dtype),
        compiler_params=_cparams("parallel", "parallel"),
        name=name,
    )(*[a for a, _ in pairs], *[b for _, b in pairs], *extra)


def _matmul_residual(pairs, x, gate, tp, ls, name):
    t, d = x.shape
    tm = _tile(math.gcd(tp, ls), 1024)
    tn = _tile(d, 1024, 128)
    specs = [
        pl.BlockSpec((tm, tn), lambda i, j: (i, j)),
        pl.BlockSpec((1, 1, tn), lambda i, j: (_mod_row(i, tm, tp, ls), 0, j)),
    ]
    return _matmul(pairs, d, F32, tm, tn, _residual, (x, gate), specs, name)


def _split3(x):
    hi = x.astype(BF16)
    r1 = x - hi.astype(F32)
    mid = r1.astype(BF16)
    lo = (r1 - mid.astype(F32)).astype(BF16)
    return hi, mid, lo


def _hgrn_kernel(q_ref, f_ref, v_ref, lb_ref, s0_ref, o_ref, sfin_ref, st_ref, *, reverse, hb, c, dk):
    ci = pl.program_id(2)

    @pl.when(ci == 0)
    def _():
        for j in range(hb):
            st_ref[j] = s0_ref[0, j].T

    row = lax.broadcasted_iota(jnp.int32, (c, c), 0)
    col = lax.broadcasted_iota(jnp.int32, (c, c), 1)
    tri = jnp.where((col >= row) if reverse else (col <= row), 1.0, 0.0).astype(BF16)
    before = (col > row) if reverse else (col < row)
    diff = row ^ col
    trow = lax.broadcasted_iota(jnp.int32, (c, dk), 0)
    n_levels = int(math.log2(c))
    level = jnp.where(row == col, n_levels, -1)
    for h in range(n_levels):
        level = jnp.where(before & ((diff >> h) == 1), h, level)

    for j in range(hb):
        sl = slice(j * dk, (j + 1) * dk)
        q = _silu(q_ref[:, sl])
        lb = lb_ref[:, sl]
        f = lb + (1.0 - lb) * jax.nn.sigmoid(f_ref[:, sl])
        k = 1.0 - f
        v = v_ref[:, sl]
        vb = v.astype(BF16)
        hi, mid, lo = _split3(jnp.log(f))
        bcum = _dot(tri, hi) + _dot(tri, mid) + _dot(tri, lo)
        btot = bcum[0:1] if reverse else bcum[c - 1 : c]

        st = st_ref[j]
        o = _dot_nt((q * jnp.exp(bcum)).astype(BF16), st.astype(BF16))
        ke = (k * jnp.exp(btot - bcum)).astype(BF16)
        st_ref[j] = st * jnp.exp(btot) + _dot_tn(vb, ke)

        kb = k.astype(BF16)
        att = jnp.where(level == n_levels, _dot_nt(q.astype(BF16), kb), 0.0)
        f_prev = pltpu.roll(f, 1, 0)
        f_next = pltpu.roll(f, c - 1, 0)
        for h in range(n_levels):
            b = 1 << h
            if b == 1:
                x_l, y_l = (q * f).astype(BF16), kb
            else:
                if b == 2:
                    o4 = trow & 3
                    if reverse:
                        ex = jnp.where(o4 == 0, f * f_next, f)
                        ey = jnp.where(o4 == 3, f_prev, 1.0)
                    else:
                        ex = jnp.where(o4 == 3, f * f_prev, f)
                        ey = jnp.where(o4 == 0, f_next, 1.0)
                else:
                    b3 = bcum.reshape(c // (2 * b), 2 * b, dk)
                    edge = b if reverse else b - 1
                    ex = ey = jnp.exp(-jnp.abs(b3 - b3[:, edge : edge + 1, :])).reshape(c, dk)
                x_l, y_l = (q * ex).astype(BF16), (k * ey).astype(BF16)
            att = jnp.where(level == h, _dot_nt(x_l, y_l), att)
        o_ref[:, sl] = o + _dot(att.astype(BF16), vb)

    @pl.when(ci == pl.num_programs(2) - 1)
    def _():
        for j in range(hb):
            sfin_ref[0, j] = st_ref[j].T


def _hgrn(proj, lb, s0, row0, nb, l, d_a, f_section, reverse, prev=None):
    t = proj.shape[0]
    h, dk, dv = s0.shape[1:]
    c = min(HGRN_CHUNK, l)
    nc = l // c
    hb = next(k for k in (HGRN_HEADS_PER_STEP, 4, 2, 1) if h % k == 0)
    hw = hb * dk
    npb = d_a // hw

    def rows(b, ci):
        return row0 // c + b * nc + ((nc - 1 - ci) if reverse else ci)

    def in_spec(section):
        return pl.BlockSpec((c, hw), lambda b, g, ci: (rows(b, ci), section * npb + g))

    in_specs = [
        in_spec(0),
        in_spec(f_section),
        in_spec(3),
        pl.BlockSpec((1, hw), lambda b, g, ci: (0, g)),
        pl.BlockSpec((1, hb, dk, dv), lambda b, g, ci: (b, g, 0, 0)),
    ]
    args = [proj, proj, proj, lb.reshape(1, d_a), s0]
    kern = functools.partial(_hgrn_kernel, reverse=reverse, hb=hb, c=c, dk=dk)
    aliases = {}
    if prev is not None:
        in_specs.append(pl.BlockSpec(memory_space=pl.ANY))
        args.append(prev)
        aliases = {5: 0}
        kern = _drop_arg(kern, 5)
    return pl.pallas_call(
        kern,
        grid=(nb, npb, nc),
        in_specs=in_specs,
        out_specs=[
            pl.BlockSpec((c, hw), lambda b, g, ci: (rows(b, ci), g)),
            pl.BlockSpec((1, hb, dk, dv), lambda b, g, ci: (b, g, 0, 0)),
        ],
        out_shape=[jax.ShapeDtypeStruct((t, d_a), F32), jax.ShapeDtypeStruct((nb, h, dk, dv), F32)],
        scratch_shapes=[pltpu.VMEM((hb, dv, dk), F32)],
        input_output_aliases=aliases,
        compiler_params=_cparams("parallel", "parallel", "arbitrary"),
        name="hgrn_bwd" if reverse else "hgrn_fwd",
    )(*args)


def _hgrn_post_kernel(of_ref, ob_ref, g_ref, gn_ref, o_ref, *, nh, dv):
    for h in range(nh):
        sl = slice(h * dv, (h + 1) * dv)
        o = of_ref[:, sl] + ob_ref[:, sl]
        o = o * lax.rsqrt(jnp.mean(o * o, axis=-1, keepdims=True) + EPS) * gn_ref[:, sl]
        o_ref[:, sl] = (o * _silu(g_ref[:, sl])).astype(o_ref.dtype)


def _hgrn_post(o_fw, o_bw, proj, gnorm, nh, dv):
    t, d_a = o_fw.shape
    tm = _tile(t, 256)
    spec = pl.BlockSpec((tm, d_a), lambda i: (i, 0))
    return pl.pallas_call(
        functools.partial(_hgrn_post_kernel, nh=nh, dv=dv),
        grid=(t // tm,),
        in_specs=[spec, spec, pl.BlockSpec((tm, d_a), lambda i: (i, 4)), pl.BlockSpec((1, d_a), lambda i: (0, 0))],
        out_specs=spec,
        out_shape=jax.ShapeDtypeStruct((t, d_a), BF16),
        compiler_params=_cparams("parallel"),
        name="hgrn_post",
    )(o_fw, o_bw, proj, gnorm.reshape(1, d_a))


def _s5_tables(lam_re, lam_im, b_re, b_im, c_re, c_im, log_step, n_levels):
    hp = lax.Precision.HIGHEST
    tc = S5_CHUNK
    gb = S5_GB
    g, p, ch = b_re.shape[1:]
    nblk = g // gb
    lam = lax.complex(jnp.minimum(lam_re.astype(F32), -1e-4), lam_im.astype(F32))
    z = lam * jnp.exp(log_step.astype(F32))[..., None]
    b_bar = ((jnp.exp(z) - 1.0) / lam)[..., None] * lax.complex(b_re.astype(F32), b_im.astype(F32))
    cc = lax.complex(c_re.astype(F32), c_im.astype(F32))
    n = jnp.arange(tc + 1, dtype=F32)
    pw = jnp.exp(z[:, :, None, :] * n[None, None, :, None])

    kern = jnp.einsum("dgop,dglp,dgpi->dglio", cc, pw[:, :, :tc], b_bar, precision=hp).real
    lag_g = jnp.concatenate([kern[1][:, :0:-1], (kern[0][:, :1] + kern[1][:, :1]), kern[0][:, 1:]], axis=1)
    lagk = lag_g.reshape(nblk, gb, 2 * tc - 1, ch, ch).transpose(0, 2, 3, 1, 4).reshape(nblk, 2 * tc - 1, ch, gb * ch)

    inc_fw = (pw[0][:, tc - 1 - jnp.arange(tc), :, None] * b_bar[0][:, None]).transpose(0, 1, 3, 2)
    inc_bw = (pw[1][:, jnp.arange(tc), :, None] * b_bar[1][:, None]).transpose(0, 1, 3, 2)
    inc = jnp.concatenate([inc_fw.real, inc_fw.imag, inc_bw.real, inc_bw.imag], axis=-1)
    inct = inc.reshape(nblk, gb, tc, ch, 4 * p).transpose(0, 2, 3, 1, 4).reshape(nblk, tc, ch, gb * 4 * p)

    out_fw = cc[0][:, None] * pw[0][:, 1 + jnp.arange(tc), None, :]
    out_bw = cc[1][:, None] * pw[1][:, tc - jnp.arange(tc), None, :]
    out = jnp.concatenate([out_fw.real, -out_fw.imag, out_bw.real, -out_bw.imag], axis=-1)
    outt = out.reshape(nblk, gb, tc, ch, 4 * p).transpose(0, 2, 4, 1, 3).reshape(nblk, tc, 4 * p, gb * ch)

    steps = (tc * (2 ** jnp.arange(n_levels))).astype(F32)
    lp = jnp.exp(z[:, :, None, :] * steps[None, None, :, None])
    a, b = lp.real, lp.imag
    lev = jnp.stack(
        [jnp.concatenate([a, a], -1), jnp.concatenate([-b, b], -1), jnp.concatenate([b, -b], -1)], axis=3
    )
    lev = lev.transpose(1, 0, 2, 3, 4).reshape(g, 2 * n_levels * 3, 2 * p)
    pad = (-lev.shape[1]) % 8
    lev = jnp.pad(lev, ((0, 0), (0, pad), (0, 0)))
    return lagk, inct.astype(BF16), outt.astype(BF16), lev


def _s5_kernel(u_ref, lagk_ref, inct_ref, outt_ref, lev_ref, h0_ref, d_ref, *rest, nch, n_levels, max_levels, want_fin):
    if want_fin:
        o_ref, fin_ref, w1_ref, wt_ref, w2_ref, a_ref = rest
    else:
        o_ref, w1_ref, wt_ref, w2_ref, a_ref = rest
        fin_ref = None
    tc = S5_CHUNK
    gb = S5_GB
    lanes = u_ref.shape[1]
    ch = lanes // gb
    sw = lev_ref.shape[2]
    m = u_ref.shape[0] // tc
    nbk = m // nch
    b0 = pl.program_id(1) * nbk

    @pl.when(pl.program_id(1) == 0)
    def _():
        r1 = lax.broadcasted_iota(jnp.int32, (lanes, gb * 2 * sw), 0) // ch
        c1 = lax.broadcasted_iota(jnp.int32, (lanes, gb * 2 * sw), 1) // (2 * sw)
        for s in range(tc):
            w1_ref[s * lanes : (s + 1) * lanes, :] = jnp.where(r1 == c1, jnp.tile(inct_ref[0, s], (gb, 1)), 0.0)
        rd = lax.broadcasted_iota(jnp.int32, (lanes, lanes), 0) // ch
        cd = lax.broadcasted_iota(jnp.int32, (lanes, lanes), 1) // ch
        taps = [jnp.where(rd == cd, jnp.tile(lagk_ref[0, li], (gb, 1)), 0.0).astype(BF16) for li in range(2 * tc - 1)]
        for s in range(tc):
            for t in range(tc):
                wt_ref[s * lanes : (s + 1) * lanes, t * lanes : (t + 1) * lanes] = taps[t - s + tc - 1]
        r2 = lax.broadcasted_iota(jnp.int32, (gb * 2 * sw, lanes), 0) // (2 * sw)
        c2 = lax.broadcasted_iota(jnp.int32, (gb * 2 * sw, lanes), 1) // ch
        for t in range(tc):
            w2_ref[:, t * lanes : (t + 1) * lanes] = jnp.where(r2 == c2, jnp.tile(outt_ref[0, t], (gb, 1)), 0.0)

    xs_f32 = [u_ref[pl.ds(s, m, stride=tc), :] for s in range(tc)]
    xcat = jnp.concatenate([x.astype(BF16) for x in xs_f32], axis=1)
    inc = _dot(xcat, w1_ref[...])

    ridx = lax.broadcasted_iota(jnp.int32, (m, sw), 0)
    cidx = ridx % nch
    bidx = ridx // nch
    carry = []
    for g in range(gb):
        for d in range(2):
            fwd = d == 0
            x = inc[:, (2 * g + d) * sw : (2 * g + d + 1) * sw]
            xr = pltpu.roll(x, sw // 2, 1)
            h0 = jnp.zeros((m, sw), F32)
            h0r = jnp.zeros((m, sw), F32)
            for b in range(nbk):
                h0 = jnp.where(bidx == b, h0_ref[g, 2 * d, pl.ds(b0 + b, 1), :], h0)
                h0r = jnp.where(bidx == b, h0_ref[g, 2 * d + 1, pl.ds(b0 + b, 1), :], h0r)

            def const(j, which, d=d, g=g):
                r = (d * max_levels + j) * 3 + which
                return lev_ref[g, r : r + 1, :]

            entry = (cidx == 0) if fwd else (cidx == nch - 1)
            x = x + jnp.where(entry, h0 * const(0, 0) + h0r * const(0, 1), 0.0)
            xr = xr + jnp.where(entry, h0r * const(0, 0) + h0 * const(0, 2), 0.0)
            for j in range(n_levels):
                k = 1 << j
                keep = (cidx >= k) if fwd else (cidx < nch - k)
                shift = k if fwd else m - k
                sx = jnp.where(keep, pltpu.roll(x, shift, 0), 0.0)
                sxr = jnp.where(keep, pltpu.roll(xr, shift, 0), 0.0)
                x, xr = x + const(j, 0) * sx + const(j, 1) * sxr, xr + const(j, 0) * sxr + const(j, 2) * sx
            carry.append(jnp.where(entry, h0, pltpu.roll(x, 1 if fwd else m - 1, 0)).astype(BF16))
            if want_fin:
                a_ref[...] = x
                fin_ref[g, d] = a_ref[pl.ds(nch - 1 if fwd else 0, nbk, stride=nch), :]
    y = _dot(xcat, wt_ref[...]) + _dot(jnp.concatenate(carry, axis=1), w2_ref[...])
    for t in range(tc):
        o_ref[pl.ds(t, m, stride=tc), :] = jax.nn.gelu(y[:, t * lanes : (t + 1) * lanes] + d_ref[...] * xs_f32[t])


def _s5(proj, u_col0, tables, h0, s5_d, row0, nb, l, max_levels, want_fin, prev=None):
    lagk, inct, outt, lev = tables
    t = proj.shape[0]
    tc, gb = S5_CHUNK, S5_GB
    nblk = lagk.shape[0]
    lanes = lagk.shape[3]
    g = nblk * gb
    sw = lev.shape[2]
    nch = l // tc
    n_levels = int(math.log2(nch))
    nbk = nb
    if not want_fin:
        nbk = max(k for k in range(1, nb + 1) if nb % k == 0 and row0 % (k * l) == 0 and (k == 1 or k * nch <= 256))
    m = nbk * nch
    rows = m * tc
    in_specs = [
        pl.BlockSpec((rows, lanes), lambda i, b: (row0 // rows + b, u_col0 // lanes + i)),
        pl.BlockSpec((1,) + lagk.shape[1:], lambda i, b: (i, 0, 0, 0)),
        pl.BlockSpec((1,) + inct.shape[1:], lambda i, b: (i, 0, 0, 0)),
        pl.BlockSpec((1,) + outt.shape[1:], lambda i, b: (i, 0, 0, 0)),
        pl.BlockSpec((gb,) + lev.shape[1:], lambda i, b: (i, 0, 0)),
        pl.BlockSpec((gb, 4, nb, sw), lambda i, b: (i, 0, 0, 0)),
        pl.BlockSpec((1, lanes), lambda i, b: (0, i)),
    ]
    args = [proj, lagk, inct, outt, lev, h0, s5_d.reshape(1, g * (lanes // gb))]
    out_specs = [pl.BlockSpec((rows, lanes), lambda i, b: (row0 // rows + b, i))]
    out_shape = [jax.ShapeDtypeStruct((t, g * (lanes // gb)), F32)]
    if want_fin:
        out_specs.append(pl.BlockSpec((gb, 2, nbk, sw), lambda i, b: (i, 0, b, 0)))
        out_shape.append(jax.ShapeDtypeStruct((g, 2, nb, sw), F32))
    kern = functools.partial(_s5_kernel, nch=nch, n_levels=n_levels, max_levels=max_levels, want_fin=want_fin)
    aliases = {}
    if prev is not None:
        in_specs.append(pl.BlockSpec(memory_space=pl.ANY))
        args.append(prev)
        aliases = {7: 0}
        kern = _drop_arg(kern, 7)
    out = pl.pallas_call(
        kern,
        grid=(nblk, nb // nbk),
        in_specs=in_specs,
        out_specs=out_specs,
        out_shape=out_shape,
        scratch_shapes=[
            pltpu.VMEM((tc * lanes, gb * 2 * sw), BF16),
            pltpu.VMEM((tc * lanes, tc * lanes), BF16),
            pltpu.VMEM((gb * 2 * sw, tc * lanes), BF16),
            pltpu.VMEM((m, sw), F32),
        ],
        input_output_aliases=aliases,
        compiler_params=_cparams("parallel", "arbitrary"),
        name="s5",
    )(*args)
    return (out[0], out[1]) if want_fin else (out[0], None)


def _dft_cos_sin(num, den, scale):
    ang = (2.0 * math.pi / den) * (num % den).astype(np.float64)
    return jnp.asarray(np.cos(ang) * scale, dtype=BF16), jnp.asarray(np.sin(ang) * scale, dtype=BF16)


def _fnet_pos_kernel(c_ref, s_ref, uv_ref, o_ref, *, gd):
    uv = uv_ref[...]
    o_ref[...] = (_dot(c_ref[...], uv[:, :gd]) - _dot(s_ref[...], uv[:, gd:])).astype(o_ref.dtype)


def _fnet_pos(cp, sp, uv, row0, nb, l, gd, prev=None):
    t = uv.shape[0]
    ng = uv.shape[1] // (2 * gd)
    const = pl.BlockSpec((l, l), lambda b, g: (0, 0))
    in_specs = [const, const, pl.BlockSpec((l, 2 * gd), lambda b, g: (row0 // l + b, g))]
    args = [cp, sp, uv]
    kern = functools.partial(_fnet_pos_kernel, gd=gd)
    aliases = {}
    if prev is not None:
        in_specs.append(pl.BlockSpec(memory_space=pl.ANY))
        args.append(prev)
        aliases = {3: 0}
        kern = _drop_arg(kern, 3)
    return pl.pallas_call(
        kern,
        grid=(nb, ng),
        in_specs=in_specs,
        out_specs=pl.BlockSpec((l, gd), lambda b, g: (row0 // l + b, g)),
        out_shape=jax.ShapeDtypeStruct((t, ng * gd), BF16),
        input_output_aliases=aliases,
        compiler_params=_cparams("parallel", "parallel"),
        name="fnet_pos",
    )(*args)


def _fourier_mix(xn, tp, lp, ls):
    t, d = xn.shape
    gd = d // FNET_GROUPS
    kk = np.arange(gd)
    cc, sc = _dft_cos_sin(kk[:, None] * kk[None, :], gd, gd**-0.5)
    cs = jnp.concatenate([cc, sc], axis=1)
    tm = _tile(t, 1024)
    uv = pl.pallas_call(
        functools.partial(_mm_kernel, n_pairs=1, epilogue=_plain),
        grid=(t // tm, FNET_GROUPS),
        in_specs=[pl.BlockSpec((tm, gd), lambda i, g: (i, g)), pl.BlockSpec((gd, 2 * gd), lambda i, g: (0, 0))],
        out_specs=pl.BlockSpec((tm, 2 * gd), lambda i, g: (i, g)),
        out_shape=jax.ShapeDtypeStruct((t, 2 * d), BF16),
        compiler_params=_cparams("parallel", "parallel"),
        name="fnet_chan",
    )(xn, cs)
    pp = np.arange(lp)
    cp_p, sp_p = _dft_cos_sin(pp[:, None] * pp[None, :], lp, lp**-0.5)
    ps = np.arange(ls)
    r, c = ps // GRID_W, ps % GRID_W
    rows = ls // GRID_W
    lcm = rows * GRID_W // math.gcd(rows, GRID_W)
    num = (lcm // rows) * (r[:, None] * r[None, :]) + (lcm // GRID_W) * (c[:, None] * c[None, :])
    cp_s, sp_s = _dft_cos_sin(num, lcm, ls**-0.5)
    z = _fnet_pos(cp_p, sp_p, uv, 0, tp // lp, lp, gd)
    return _fnet_pos(cp_s, sp_s, uv, tp, (t - tp) // ls, ls, gd, prev=z)


def _router_kernel(x_ref, w_ref, b_ref, o_ref, *, ng, npg):
    h = w_ref.shape[0] // 2
    x_lo, x_hi = _unpack_halves(_load_slabs(x_ref, (), o_ref.shape[0], h // LANES))
    logits = _dot(x_lo.astype(BF16), w_ref[:h, :]) + _dot(x_hi.astype(BF16), w_ref[h:, :]) + b_ref[...]
    lane = lax.broadcasted_iota(jnp.int32, logits.shape, 1).astype(F32)
    big = 1e9

    def first_max(vals):
        m = jnp.max(vals, axis=-1, keepdims=True)
        return m, jnp.min(jnp.where(vals == m, lane, big), axis=-1, keepdims=True)

    gl = jnp.where(lane < ng, logits, NEG)
    gmax, gidx = first_max(gl)
    gprob = 1.0 / jnp.sum(jnp.exp(gl - gmax), axis=-1, keepdims=True)
    lo = ng + gidx * npg
    el = jnp.where((lane >= lo) & (lane < lo + npg), logits, NEG)
    m1, i1 = first_max(el)
    m2, i2 = first_max(jnp.where(lane == i1, NEG, el))
    e = jnp.exp(m2 - m1)
    w1 = gprob / (1.0 + e)
    out = jnp.where(lane == 0, i1 - ng, jnp.where(lane == 1, i2 - ng, jnp.where(lane == 2, w1, jnp.where(lane == 3, w1 * e, 0.0))))
    o_ref[...] = out


def _router(xn, wr_g, br_g, wr_e, br_e):
    d = wr_g.shape[0]
    t = xn.shape[0] // (d // 2 // LANES)
    ng, ne = wr_g.shape[1], wr_e.shape[1]
    w = jnp.zeros((d, 128), F32).at[:, :ng].set(wr_g).at[:, ng : ng + ne].set(wr_e).astype(BF16)
    b = jnp.zeros((1, 128), F32).at[0, :ng].set(br_g).at[0, ng : ng + ne].set(br_e)
    tm = _tile(t, 512)
    return pl.pallas_call(
        functools.partial(_router_kernel, ng=ng, npg=ne // ng),
        grid=(t // tm,),
        in_specs=[pl.BlockSpec((tm * (d // 2 // LANES), LANES), lambda i: (i, 0)), pl.BlockSpec((d, 128), lambda i: (0, 0)), pl.BlockSpec((1, 128), lambda i: (0, 0))],
        out_specs=pl.BlockSpec((tm, 128), lambda i: (i, 0)),
        out_shape=jax.ShapeDtypeStruct((t, 128), F32),
        compiler_params=_cparams("parallel"),
        name="router",
    )(xn, w, b)


def _moe_plan(rout, n_exp, tm, nt):
    t = rout.shape[0]
    flat = rout[:, :2].astype(jnp.int32).T.reshape(-1)
    oh = (flat[:, None] == jnp.arange(n_exp)[None, :]).astype(jnp.int32)
    blk = _tile(2 * t, 128)
    c1 = jnp.cumsum(oh.reshape(-1, blk, n_exp), axis=1)
    tot = c1[:, -1]
    cs = (c1 + (jnp.cumsum(tot, axis=0) - tot)[:, None]).reshape(2 * t, n_exp)
    rank = jnp.sum(oh * cs, axis=1) - 1
    counts = cs[-1]
    tiles = (counts + tm - 1) // tm
    tile_end = jnp.cumsum(tiles)
    pos = (tile_end - tiles)[flat] * tm + rank
    row_token = jnp.zeros((nt * tm,), jnp.int32).at[pos].set(jnp.tile(jnp.arange(t, dtype=jnp.int32), 2))
    tile_ids = jnp.arange(nt)
    tile_expert = jnp.minimum(jnp.searchsorted(tile_end, tile_ids, side="right"), n_exp - 1).astype(jnp.int32)
    tile_valid = (tile_ids < tile_end[-1]).astype(jnp.int32)
    prev_expert = jnp.concatenate([jnp.full((1,), -1, jnp.int32), tile_expert[:-1]])
    tile_first = (tile_valid * (tile_expert != prev_expert)).astype(jnp.int32)
    owner = jnp.where(tiles > 0, jnp.arange(n_exp), n_exp)
    nxt = jnp.concatenate([lax.cummin(owner[::-1])[::-1][1:], jnp.full((1,), n_exp)])
    tile_next = jnp.where(nxt < n_exp, nxt, -1)[tile_expert].astype(jnp.int32)
    return row_token, pos.astype(jnp.int32), tile_expert, tile_valid, tile_first, tile_next


def _moe_kernel(te_ref, tv_ref, tf_ref, tn_ref, rt_ref, x_hbm, wg_hbm, wu_hbm, wd_hbm, y_hbm,
                xbuf, stg_g, stg_u, stg_d, wb_g, wb_u, wb_d, obuf, sem_x, sem_w, sem_o, *, tm, layer):
    i = pl.program_id(0)
    nt = pl.num_programs(0)
    slot = i % 2
    xslot = i % 3
    d, f = stg_g.shape
    sl = d // 2 // LANES

    def issue_rows(tile, s, r0, r1):
        def body(r, carry):
            src = pl.multiple_of(rt_ref[tile * tm + r] * sl, sl)
            dst = pl.multiple_of(r * sl, sl)
            pltpu.make_async_copy(x_hbm.at[pl.ds(src, sl)], xbuf.at[s, pl.ds(dst, sl)], sem_x.at[s]).start()
            return carry

        lax.fori_loop(r0, r1, body, 0, unroll=8)

    def weight_copies(e):
        return (
            pltpu.make_async_copy(wg_hbm.at[layer, e], stg_g, sem_w.at[0]),
            pltpu.make_async_copy(wu_hbm.at[layer, e], stg_u, sem_w.at[1]),
            pltpu.make_async_copy(wd_hbm.at[layer, e], stg_d, sem_w.at[2]),
        )

    def out_copy(tile):
        rows = tm * sl
        dst = y_hbm.at[pl.ds(pl.multiple_of(tile * rows, rows), rows)]
        return pltpu.make_async_copy(obuf.at[tile % 2], dst, sem_o.at[tile % 2])

    def cast(src, dst, ck):
        def body(k, carry):
            r = pl.multiple_of(k * ck, ck)
            dst[pl.ds(r, ck), :] = src[pl.ds(r, ck), :].astype(BF16)
            return carry

        lax.fori_loop(0, src.shape[0] // ck, body, 0)

    @pl.when(i == 0)
    def _():
        issue_rows(0, 0, 0, tm)
        for cp in weight_copies(te_ref[0]):
            cp.start(priority=1)

        @pl.when(jnp.logical_and(nt > 1, tv_ref[jnp.minimum(1, nt - 1)] == 1))
        def _():
            issue_rows(1, 1, 0, tm)

    ahead = jnp.minimum(i + 2, nt - 1)

    @pl.when(jnp.logical_and(i + 2 < nt, tv_ref[ahead] == 1))
    def _():
        issue_rows(i + 2, (i + 2) % 3, 0, tm)

    @pl.when(jnp.logical_and(i >= 2, tv_ref[i] == 1))
    def _():
        out_copy(i - 2).wait()

    @pl.when(jnp.logical_and(tv_ref[i] == 0, jnp.logical_and(i >= 1, tv_ref[jnp.maximum(i - 1, 0)] == 1)))
    def _():
        out_copy(i - 1).wait()

        @pl.when(i >= 2)
        def _():
            out_copy(i - 2).wait()

    @pl.when(tv_ref[i] == 1)
    def _():
        @pl.when(tf_ref[i] == 1)
        def _():
            for cp in weight_copies(0):
                cp.wait()
            cast(stg_g, wb_g, min(256, d))
            cast(stg_u, wb_u, min(256, d))
            cast(stg_d, wb_d, min(32, f))

            @pl.when(tn_ref[i] >= 0)
            def _():
                for cp in weight_copies(tn_ref[i]):
                    cp.start(priority=1)

        pltpu.make_async_copy(x_hbm.at[pl.ds(0, tm * sl)], xbuf.at[xslot], sem_x.at[xslot]).wait()
        x_lo, x_hi = (v.astype(BF16) for v in _unpack_halves(_load_slabs(xbuf, (xslot,), tm, sl)))
        dh = d // 2
        hg = _dot(x_lo, wb_g[:dh, :]) + _dot(x_hi, wb_g[dh:, :])
        hu = _dot(x_lo, wb_u[:dh, :]) + _dot(x_hi, wb_u[dh:, :])
        h = (_silu(hg) * hu).astype(BF16)
        _store_slabs(obuf, (slot,), _pack_pair(_dot(h, wb_d[:, :dh]), _dot(h, wb_d[:, dh:])))
        out_copy(i).start()

        @pl.when(i == nt - 1)
        def _():

            @pl.when(i >= 1)
            def _():
                out_copy(i - 1).wait()

            out_copy(i).wait()


def _combine_kernel(pos_ref, y_hbm, x_ref, r_ref, g_ref, o_ref, buf, sem, *, tm, t):
    i = pl.program_id(0)
    n = pl.num_programs(0)
    slot = i % 2

    sl = buf.shape[2] // tm

    def issue(tile, s):
        def body(r, carry):
            dst = pl.multiple_of(r * sl, sl)
            for k in range(2):
                src = pl.multiple_of(pos_ref[k * t + tile * tm + r] * sl, sl)
                pltpu.make_async_copy(y_hbm.at[pl.ds(src, sl)], buf.at[s, k, pl.ds(dst, sl)], sem.at[s]).start()
            return carry

        lax.fori_loop(0, tm, body, 0, unroll=4)

    @pl.when(i == 0)
    def _():
        issue(0, 0)

    @pl.when(i + 1 < n)
    def _():
        issue(i + 1, 1 - slot)

    for k in range(2):
        pltpu.make_async_copy(y_hbm.at[pl.ds(0, tm * sl)], buf.at[slot, k], sem.at[slot]).wait()
    w = r_ref[...]
    w1, w2 = w[:, 2:3], w[:, 3:4]
    a_lo, a_hi = _unpack_halves(_load_slabs(buf, (slot, 0), tm, sl))
    b_lo, b_hi = _unpack_halves(_load_slabs(buf, (slot, 1), tm, sl))
    dh = a_lo.shape[1]
    o_ref[:, :dh] = x_ref[:, :dh] + g_ref[0, :, :dh] * (w1 * a_lo + w2 * b_lo)
    o_ref[:, dh:] = x_ref[:, dh:] + g_ref[0, :, dh:] * (w1 * a_hi + w2 * b_hi)


def _moe(xn, x, gate, params, layer, tp, ls):
    wr_g, br_g, wr_e, br_e, w_gate, w_up, w_down = params
    t, d = x.shape
    n_exp, _, f = w_gate.shape[1:]
    rout = _router(xn, wr_g, br_g, wr_e, br_e)
    sl = d // 2 // LANES
    tm = 256 if t >= 4096 else 64
    nt = (2 * t + tm - 1) // tm + n_exp
    row_token, pos, tile_expert, tile_valid, tile_first, tile_next = _moe_plan(rout, n_exp, tm, nt)
    hbm = pl.BlockSpec(memory_space=pl.ANY)
    y = pl.pallas_call(
        functools.partial(_moe_kernel, tm=tm, layer=layer),
        grid_spec=pltpu.PrefetchScalarGridSpec(
            num_scalar_prefetch=5,
            grid=(nt,),
            in_specs=[hbm, hbm, hbm, hbm],
            out_specs=hbm,
            scratch_shapes=[
                pltpu.VMEM((3, tm * sl, LANES), jnp.uint32),
                pltpu.VMEM((d, f), F32),
                pltpu.VMEM((d, f), F32),
                pltpu.VMEM((f, d), F32),
                pltpu.VMEM((d, f), BF16),
                pltpu.VMEM((d, f), BF16),
                pltpu.VMEM((f, d), BF16),
                pltpu.VMEM((2, tm * sl, LANES), jnp.uint32),
                pltpu.SemaphoreType.DMA((3,)),
                pltpu.SemaphoreType.DMA((3,)),
                pltpu.SemaphoreType.DMA((2,)),
            ],
        ),
        out_shape=jax.ShapeDtypeStruct((nt * tm * sl, LANES), jnp.uint32),
        compiler_params=_cparams("arbitrary"),
        name="moe_experts",
    )(tile_expert, tile_valid, tile_first, tile_next, row_token, xn, w_gate, w_up, w_down)

    tc = _tile(math.gcd(tp, ls), 128)
    return pl.pallas_call(
        functools.partial(_combine_kernel, tm=tc, t=t),
        grid_spec=pltpu.PrefetchScalarGridSpec(
            num_scalar_prefetch=1,
            grid=(t // tc,),
            in_specs=[
                pl.BlockSpec(memory_space=pl.ANY),
                pl.BlockSpec((tc, d), lambda i, p: (i, 0)),
                pl.BlockSpec((tc, 128), lambda i, p: (i, 0)),
                pl.BlockSpec((1, 1, d), lambda i, p: (_mod_row(i, tc, tp, ls), 0, 0)),
            ],
            out_specs=pl.BlockSpec((tc, d), lambda i, p: (i, 0)),
            scratch_shapes=[pltpu.VMEM((2, 2, tc * sl, LANES), jnp.uint32), pltpu.SemaphoreType.DMA((2,))],
        ),
        out_shape=jax.ShapeDtypeStruct((t, d), F32),
        compiler_params=_cparams("arbitrary"),
        name="moe_combine",
    )(pos, y, x, rout, gate)


def kernel(x_prompt, x_sample, state_hgrn, state_s5_re, state_s5_im, c, c_ctx, ada_w, ada_b, norm_mix, norm_ffn, norm_final, ab_w_in, ab_w_out, hgrn_lb_theta, hgrn_gnorm, s5_lam_re, s5_lam_im, s5_b_re, s5_b_im, s5_c_re, s5_c_im, s5_log_step, s5_d, s5_glu_w, s5_glu_b, fnet_w_out, moe_wr_group, moe_br_group, moe_wr_expert, moe_br_expert, moe_w_gate, moe_w_up, moe_w_down):
    bp, lp, d = x_prompt.shape
    bs, ls, _ = x_sample.shape
    tp, ts = bp * lp, bs * ls
    depth = ada_w.shape[0]
    nh, dk, dv = state_hgrn.shape[3:]
    d_a = nh * dk
    n_grp, s5_p = state_s5_re.shape[3:]
    d_b = s5_d.shape[1]
    nmod = 1 + bs

    x = jnp.concatenate([x_prompt.reshape(tp, d), x_sample.reshape(ts, d)], axis=0)
    c8 = jnp.zeros((8, d), F32).at[0].set(c_ctx).at[1:nmod].set(c)
    mod_all = _ada(c8, ada_w, ada_b)
    lb_all = jnp.cumsum(jax.nn.softmax(hgrn_lb_theta.astype(F32), axis=1), axis=1)
    max_levels = int(math.log2(max(lp, ls) // S5_CHUNK))

    new_h, new_re, new_im = [], [], []
    i_ab = i_c = 0
    for l in range(depth):
        mod = mod_all[l, :nmod].reshape(nmod, 6, 1, d).transpose(1, 0, 2, 3)
        xn = _norm_mod(x, norm_mix[l], mod[0], mod[1], tp, ls, False)
        if l % 2 == 0:
            tm = _tile(math.gcd(tp, ls), 1024)
            n_in = ab_w_in.shape[2]
            proj = _matmul([(xn, ab_w_in[i_ab].astype(BF16))], n_in, F32, tm, _tile(n_in, 1024, 128), name="proj_in")
            tables = _s5_tables(s5_lam_re[i_ab], s5_lam_im[i_ab], s5_b_re[i_ab], s5_b_im[i_ab], s5_c_re[i_ab], s5_c_im[i_ab], s5_log_step[i_ab], max_levels)
            o_dirs, yg, h_fin, x_fin = [None, None], None, None, None
            for row0, nb, ln, h0, x0 in (
                (0, bp, lp, jnp.zeros((bp, 2, nh, dk, dv), F32), jnp.zeros((bp, 2, n_grp, 2 * s5_p), F32)),
                (tp, bs, ls, state_hgrn[:, i_ab].astype(F32), jnp.concatenate([state_s5_re[:, i_ab], state_s5_im[:, i_ab]], axis=-1).astype(F32)),
            ):
                fins = []
                for dr in range(2):
                    o_dirs[dr], s_fin = _hgrn(proj, lb_all[dr, i_ab], h0[:, dr], row0, nb, ln, d_a, 1 + dr, dr == 1, prev=o_dirs[dr])
                    fins.append(s_fin)
                x0s = jnp.concatenate([x0[..., s5_p:], x0[..., :s5_p]], axis=-1)
                h0_s5 = jnp.stack([x0[:, 0], x0s[:, 0], x0[:, 1], x0s[:, 1]], axis=0).transpose(2, 0, 1, 3)
                yg, fin_s5 = _s5(proj, 5 * d_a, tables, h0_s5, s5_d[i_ab], row0, nb, ln, max_levels, row0 == 0, prev=yg)
                if row0 == 0:
                    h_fin = jnp.stack(fins, axis=1)
                    x_fin = fin_s5.transpose(2, 1, 0, 3)
            o_a = _hgrn_post(o_dirs[0], o_dirs[1], proj, hgrn_gnorm[i_ab], nh, dv)
            tn = _tile(d_b, 1024, 128)
            y_glu = _matmul(
                [(yg, s5_glu_w[i_ab].astype(BF16))], d_b, BF16, tm, tn, _glu, (yg, s5_glu_b[i_ab].reshape(1, d_b)),
                (pl.BlockSpec((tm, tn), lambda i, j: (i, j)), pl.BlockSpec((1, tn), lambda i, j: (0, j))), "s5_glu",
            )
            w_out = ab_w_out[i_ab].astype(BF16)
            x = _matmul_residual([(o_a, w_out[:d_a]), (y_glu, w_out[d_a:])], x, mod[2], tp, ls, "mix_out")
            new_h.append(h_fin)
            new_re.append(x_fin[..., :s5_p])
            new_im.append(x_fin[..., s5_p:])
            i_ab += 1
        else:
            z = _fourier_mix(xn, tp, lp, ls)
            x = _matmul_residual([(z, fnet_w_out[i_c].astype(BF16))], x, mod[2], tp, ls, "fnet_out")
            i_c += 1
        xn = _norm_mod(x, norm_ffn[l], mod[3], mod[4], tp, ls, True)
        moe_params = (moe_wr_group[l], moe_br_group[l], moe_wr_expert[l], moe_br_expert[l], moe_w_gate, moe_w_up, moe_w_down)
        x = _moe(xn, x, mod[5], moe_params, l, tp, ls)
    return (
        _rmsnorm(x, norm_final, 0, tp).reshape(bp, lp, d),
        _rmsnorm(x, norm_final, tp, ts).reshape(bs, ls, d),
        jnp.stack(new_h, axis=1),
        jnp.stack(new_re, axis=1),
        jnp.stack(new_im, axis=1),
    )
```

```python
import functools
import math

import jax
import jax.numpy as jnp
import numpy as np
from jax import lax
from jax.experimental import pallas as pl
from jax.experimental.pallas import tpu as pltpu

F32 = jnp.float32
BF16 = jnp.bfloat16
EPS = 1e-6
NEG = -1e30
V7X_VMEM_LIMIT_BYTES = 56 * 1024 * 1024
FNET_GROUPS = 8
GRID_W = 64
HGRN_CHUNK = 128
HGRN_HEADS_PER_STEP = 8
S5_CHUNK = 16
S5_GB = 8
MOE_GATHER_AHEAD = 3


def _cparams(*sem):
    return pltpu.CompilerParams(dimension_semantics=sem, vmem_limit_bytes=V7X_VMEM_LIMIT_BYTES)


def _tile(n, pref, mult=8):
    if n <= pref:
        return n
    for t in range(pref - pref % mult, 0, -mult):
        if n % t == 0:
            return t
    raise ValueError(f"no tile for {n}")


def _drop_arg(body, index):
    def wrapped(*refs):
        return body(*refs[:index], *refs[index + 1 :])

    return wrapped


def _dot(a, b):
    return jnp.dot(a, b, preferred_element_type=F32)


def _dot_nt(a, b):
    return lax.dot_general(a, b, (((1,), (1,)), ((), ())), preferred_element_type=F32)


def _dot_tn(a, b):
    return lax.dot_general(a, b, (((0,), (0,)), ((), ())), preferred_element_type=F32)


def _silu(x):
    return x * jax.nn.sigmoid(x)


def _ada_kernel(c_ref, w_ref, b_ref, o_ref):
    a = _silu(c_ref[...]).astype(BF16)
    o_ref[0] = _dot(a, w_ref[0].astype(BF16)) + b_ref[0]


def _ada(c8, ada_w, ada_b):
    depth, d, n = ada_w.shape
    tn = _tile(n, 1024, 128)
    return pl.pallas_call(
        _ada_kernel,
        grid=(depth, n // tn),
        in_specs=[
            pl.BlockSpec((8, d), lambda l, j: (0, 0)),
            pl.BlockSpec((1, d, tn), lambda l, j: (l, 0, j)),
            pl.BlockSpec((1, 1, tn), lambda l, j: (l, 0, j)),
        ],
        out_specs=pl.BlockSpec((1, 8, tn), lambda l, j: (l, 0, j)),
        out_shape=jax.ShapeDtypeStruct((depth, 8, n), F32),
        compiler_params=_cparams("parallel", "parallel"),
        name="ada",
    )(c8, ada_w, ada_b.reshape(depth, 1, n))


def _mod_row(i, tm, tp, ls):
    r0 = i * tm
    return jnp.where(r0 < tp, 0, 1 + (r0 - tp) // ls)


def _pack_halves(y):
    h = y.shape[1] // 2
    return _pack_pair(y[:, :h], y[:, h:])


def _pack_pair(lo, hi):
    def bf16_bits(v):
        return lax.bitcast_convert_type(v.astype(BF16).astype(F32), jnp.uint32)

    return (bf16_bits(lo) >> 16) | (bf16_bits(hi) & jnp.uint32(0xFFFF0000))


def _unpack_halves(w):
    lo = lax.bitcast_convert_type(w << 16, F32)
    hi = lax.bitcast_convert_type(w & jnp.uint32(0xFFFF0000), F32)
    return lo, hi


LANES = 128


def _store_slabs(ref, lead, words):
    m, w = words.shape
    s = w // LANES
    for j in range(s):
        ref[(*lead, pl.ds(j, m, stride=s), slice(None))] = words[:, j * LANES : (j + 1) * LANES]


def _load_slabs(ref, lead, m, s):
    return jnp.concatenate([ref[(*lead, pl.ds(j, m, stride=s), slice(None))] for j in range(s)], axis=1)


def _norm_mod_kernel(x_ref, g_ref, sh_ref, sc_ref, o_ref, *, packed):
    x = x_ref[...]
    y = x * lax.rsqrt(jnp.mean(x * x, axis=-1, keepdims=True) + EPS) * g_ref[...]
    y = y * (1.0 + sc_ref[0]) + sh_ref[0]
    if packed:
        _store_slabs(o_ref, (), _pack_halves(y))
    else:
        o_ref[...] = y.astype(o_ref.dtype)


def _norm_mod(x, g, shift, scale, tp, ls, packed):
    t, d = x.shape
    tm = _tile(math.gcd(tp, ls), 256)
    mod_spec = pl.BlockSpec((1, 1, d), lambda i: (_mod_row(i, tm, tp, ls), 0, 0))
    s = d // 2 // LANES
    out_spec = pl.BlockSpec((tm * s, LANES), lambda i: (i, 0)) if packed else pl.BlockSpec((tm, d), lambda i: (i, 0))
    out_shape = jax.ShapeDtypeStruct((t * s, LANES), jnp.uint32) if packed else jax.ShapeDtypeStruct((t, d), BF16)
    return pl.pallas_call(
        functools.partial(_norm_mod_kernel, packed=packed),
        grid=(t // tm,),
        in_specs=[pl.BlockSpec((tm, d), lambda i: (i, 0)), pl.BlockSpec((1, d), lambda i: (0, 0)), mod_spec, mod_spec],
        out_specs=out_spec,
        out_shape=out_shape,
        compiler_params=_cparams("parallel"),
        name="norm_mod",
    )(x, g.reshape(1, d), shift, scale)


def _rmsnorm_kernel(x_ref, g_ref, o_ref):
    x = x_ref[...]
    o_ref[...] = x * lax.rsqrt(jnp.mean(x * x, axis=-1, keepdims=True) + EPS) * g_ref[...]


def _rmsnorm(x, g, row0, n):
    d = x.shape[1]
    tm = _tile(math.gcd(row0, n) if row0 else n, 256)
    return pl.pallas_call(
        _rmsnorm_kernel,
        grid=(n // tm,),
        in_specs=[pl.BlockSpec((tm, d), lambda i: (row0 // tm + i, 0)), pl.BlockSpec((1, d), lambda i: (0, 0))],
        out_specs=pl.BlockSpec((tm, d), lambda i: (i, 0)),
        out_shape=jax.ShapeDtypeStruct((n, d), F32),
        compiler_params=_cparams("parallel"),
        name="final_norm",
    )(x, g.reshape(1, d))


def _mm_kernel(*refs, n_pairs, epilogue):
    o_ref = refs[-1]
    acc = None
    for a_ref, b_ref in zip(refs[:n_pairs], refs[n_pairs : 2 * n_pairs]):
        part = _dot(a_ref[...].astype(BF16), b_ref[...])
        acc = part if acc is None else acc + part
    o_ref[...] = epilogue(acc, *refs[2 * n_pairs : -1]).astype(o_ref.dtype)


def _plain(acc):
    return acc


def _residual(acc, x_ref, g_ref):
    return x_ref[...] + g_ref[0] * acc


def _glu(acc, y_ref, b_ref):
    return y_ref[...] * jax.nn.sigmoid(acc + b_ref[...])


def _matmul(pairs, n, out_dtype, tm, tn, epilogue=_plain, extra=(), extra_specs=(), name="matmul"):
    t = pairs[0][0].shape[0]
    a_specs = [pl.BlockSpec((tm, a.shape[1]), lambda i, j: (i, 0)) for a, _ in pairs]
    b_specs = [pl.BlockSpec((b.shape[0], tn), lambda i, j: (0, j)) for _, b in pairs]
    return pl.pallas_call(
        functools.partial(_mm_kernel, n_pairs=len(pairs), epilogue=epilogue),
        grid=(t // tm, n // tn),
        in_specs=a_specs + b_specs + list(extra_specs),
        out_specs=pl.BlockSpec((tm, tn), lambda i, j: (i, j)),
        out_shape=jax.ShapeDtypeStruct((t, n), out_dtype),
        compiler_params=_cparams("parallel", "parallel"),
        name=name,
    )(*[a for a, _ in pairs], *[b for _, b in pairs], *extra)


def _matmul_residual(pairs, x, gate, tp, ls, name):
    t, d = x.shape
    tm = _tile(math.gcd(tp, ls), 1024)
    tn = _tile(d, 1024, 128)
    specs = [
        pl.BlockSpec((tm, tn), lambda i, j: (i, j)),
        pl.BlockSpec((1, 1, tn), lambda i, j: (_mod_row(i, tm, tp, ls), 0, j)),
    ]
    return _matmul(pairs, d, F32, tm, tn, _residual, (x, gate), specs, name)


def _split3(x):
    hi = x.astype(BF16)
    r1 = x - hi.astype(F32)
    mid = r1.astype(BF16)
    lo = (r1 - mid.astype(F32)).astype(BF16)
    return hi, mid, lo


def _hgrn_kernel(q_ref, f_ref, v_ref, lb_ref, s0_ref, o_ref, sfin_ref, st_ref, *, reverse, hb, c, dk):
    ci = pl.program_id(2)

    @pl.when(ci == 0)
    def _():
        for j in range(hb):
            st_ref[j] = s0_ref[0, j].T

    row = lax.broadcasted_iota(jnp.int32, (c, c), 0)
    col = lax.broadcasted_iota(jnp.int32, (c, c), 1)
    tri = jnp.where((col >= row) if reverse else (col <= row), 1.0, 0.0).astype(BF16)
    before = (col > row) if reverse else (col < row)
    diff = row ^ col
    trow = lax.broadcasted_iota(jnp.int32, (c, dk), 0)
    n_levels = int(math.log2(c))
    level = jnp.where(row == col, n_levels, -1)
    for h in range(n_levels):
        level = jnp.where(before & ((diff >> h) == 1), h, level)

    for j in range(hb):
        sl = slice(j * dk, (j + 1) * dk)
        q = _silu(q_ref[:, sl])
        lb = lb_ref[:, sl]
        f = lb + (1.0 - lb) * jax.nn.sigmoid(f_ref[:, sl])
        k = 1.0 - f
        v = v_ref[:, sl]
        vb = v.astype(BF16)
        hi, mid, lo = _split3(jnp.log(f))
        bcum = _dot(tri, hi) + _dot(tri, mid) + _dot(tri, lo)
        btot = bcum[0:1] if reverse else bcum[c - 1 : c]

        st = st_ref[j]
        o = _dot_nt((q * jnp.exp(bcum)).astype(BF16), st.astype(BF16))
        ke = (k * jnp.exp(btot - bcum)).astype(BF16)
        st_ref[j] = st * jnp.exp(btot) + _dot_tn(vb, ke)

        kb = k.astype(BF16)
        att = jnp.where(level == n_levels, _dot_nt(q.astype(BF16), kb), 0.0)
        f_prev = pltpu.roll(f, 1, 0)
        f_next = pltpu.roll(f, c - 1, 0)
        for h in range(n_levels):
            b = 1 << h
            if b == 1:
                x_l, y_l = (q * f).astype(BF16), kb
            else:
                if b == 2:
                    o4 = trow & 3
                    if reverse:
                        ex = jnp.where(o4 == 0, f * f_next, f)
                        ey = jnp.where(o4 == 3, f_prev, 1.0)
                    else:
                        ex = jnp.where(o4 == 3, f * f_prev, f)
                        ey = jnp.where(o4 == 0, f_next, 1.0)
                else:
                    b3 = bcum.reshape(c // (2 * b), 2 * b, dk)
                    edge = b if reverse else b - 1
                    ex = ey = jnp.exp(-jnp.abs(b3 - b3[:, edge : edge + 1, :])).reshape(c, dk)
                x_l, y_l = (q * ex).astype(BF16), (k * ey).astype(BF16)
            att = jnp.where(level == h, _dot_nt(x_l, y_l), att)
        o_ref[:, sl] = o + _dot(att.astype(BF16), vb)

    @pl.when(ci == pl.num_programs(2) - 1)
    def _():
        for j in range(hb):
            sfin_ref[0, j] = st_ref[j].T


def _hgrn(proj, lb, s0, row0, nb, l, d_a, f_section, reverse, prev=None):
    t = proj.shape[0]
    h, dk, dv = s0.shape[1:]
    c = min(HGRN_CHUNK, l)
    nc = l // c
    hb = next(k for k in (HGRN_HEADS_PER_STEP, 4, 2, 1) if h % k == 0)
    hw = hb * dk
    npb = d_a // hw

    def rows(b, ci):
        return row0 // c + b * nc + ((nc - 1 - ci) if reverse else ci)

    def in_spec(section):
        return pl.BlockSpec((c, hw), lambda b, g, ci: (rows(b, ci), section * npb + g))

    in_specs = [
        in_spec(0),
        in_spec(f_section),
        in_spec(3),
        pl.BlockSpec((1, hw), lambda b, g, ci: (0, g)),
        pl.BlockSpec((1, hb, dk, dv), lambda b, g, ci: (b, g, 0, 0)),
    ]
    args = [proj, proj, proj, lb.reshape(1, d_a), s0]
    kern = functools.partial(_hgrn_kernel, reverse=reverse, hb=hb, c=c, dk=dk)
    aliases = {}
    if prev is not None:
        in_specs.append(pl.BlockSpec(memory_space=pl.ANY))
        args.append(prev)
        aliases = {5: 0}
        kern = _drop_arg(kern, 5)
    return pl.pallas_call(
        kern,
        grid=(nb, npb, nc),
        in_specs=in_specs,
        out_specs=[
            pl.BlockSpec((c, hw), lambda b, g, ci: (rows(b, ci), g)),
            pl.BlockSpec((1, hb, dk, dv), lambda b, g, ci: (b, g, 0, 0)),
        ],
        out_shape=[jax.ShapeDtypeStruct((t, d_a), F32), jax.ShapeDtypeStruct((nb, h, dk, dv), F32)],
        scratch_shapes=[pltpu.VMEM((hb, dv, dk), F32)],
        input_output_aliases=aliases,
        compiler_params=_cparams("parallel", "parallel", "arbitrary"),
        name="hgrn_bwd" if reverse else "hgrn_fwd",
    )(*args)


def _hgrn_post_kernel(of_ref, ob_ref, g_ref, gn_ref, o_ref, *, nh, dv):
    for h in range(nh):
        sl = slice(h * dv, (h + 1) * dv)
        o = of_ref[:, sl] + ob_ref[:, sl]
        o = o * lax.rsqrt(jnp.mean(o * o, axis=-1, keepdims=True) + EPS) * gn_ref[:, sl]
        o_ref[:, sl] = (o * _silu(g_ref[:, sl])).astype(o_ref.dtype)


def _hgrn_post(o_fw, o_bw, proj, gnorm, nh, dv):
    t, d_a = o_fw.shape
    tm = _tile(t, 256)
    spec = pl.BlockSpec((tm, d_a), lambda i: (i, 0))
    return pl.pallas_call(
        functools.partial(_hgrn_post_kernel, nh=nh, dv=dv),
        grid=(t // tm,),
        in_specs=[spec, spec, pl.BlockSpec((tm, d_a), lambda i: (i, 4)), pl.BlockSpec((1, d_a), lambda i: (0, 0))],
        out_specs=spec,
        out_shape=jax.ShapeDtypeStruct((t, d_a), BF16),
        compiler_params=_cparams("parallel"),
        name="hgrn_post",
    )(o_fw, o_bw, proj, gnorm.reshape(1, d_a))


def _s5_tables(lam_re, lam_im, b_re, b_im, c_re, c_im, log_step, n_levels):
    hp = lax.Precision.HIGHEST
    tc = S5_CHUNK
    gb = S5_GB
    g, p, ch = b_re.shape[1:]
    nblk = g // gb
    lam = lax.complex(jnp.minimum(lam_re.astype(F32), -1e-4), lam_im.astype(F32))
    z = lam * jnp.exp(log_step.astype(F32))[..., None]
    b_bar = ((jnp.exp(z) - 1.0) / lam)[..., None] * lax.complex(b_re.astype(F32), b_im.astype(F32))
    cc = lax.complex(c_re.astype(F32), c_im.astype(F32))
    n = jnp.arange(tc + 1, dtype=F32)
    pw = jnp.exp(z[:, :, None, :] * n[None, None, :, None])

    kern = jnp.einsum("dgop,dglp,dgpi->dglio", cc, pw[:, :, :tc], b_bar, precision=hp).real
    lag_g = jnp.concatenate([kern[1][:, :0:-1], (kern[0][:, :1] + kern[1][:, :1]), kern[0][:, 1:]], axis=1)
    lagk = lag_g.reshape(nblk, gb, 2 * tc - 1, ch, ch).transpose(0, 2, 3, 1, 4).reshape(nblk, 2 * tc - 1, ch, gb * ch)

    inc_fw = (pw[0][:, tc - 1 - jnp.arange(tc), :, None] * b_bar[0][:, None]).transpose(0, 1, 3, 2)
    inc_bw = (pw[1][:, jnp.arange(tc), :, None] * b_bar[1][:, None]).transpose(0, 1, 3, 2)
    inc = jnp.concatenate([inc_fw.real, inc_fw.imag, inc_bw.real, inc_bw.imag], axis=-1)
    inct = inc.reshape(nblk, gb, tc, ch, 4 * p).transpose(0, 2, 3, 1, 4).reshape(nblk, tc, ch, gb * 4 * p)

    out_fw = cc[0][:, None] * pw[0][:, 1 + jnp.arange(tc), None, :]
    out_bw = cc[1][:, None] * pw[1][:, tc - jnp.arange(tc), None, :]
    out = jnp.concatenate([out_fw.real, -out_fw.imag, out_bw.real, -out_bw.imag], axis=-1)
    outt = out.reshape(nblk, gb, tc, ch, 4 * p).transpose(0, 2, 4, 1, 3).reshape(nblk, tc, 4 * p, gb * ch)

    steps = (tc * (2 ** jnp.arange(n_levels))).astype(F32)
    lp = jnp.exp(z[:, :, None, :] * steps[None, None, :, None])
    a, b = lp.real, lp.imag
    lev = jnp.stack(
        [jnp.concatenate([a, a], -1), jnp.concatenate([-b, b], -1), jnp.concatenate([b, -b], -1)], axis=3
    )
    lev = lev.transpose(1, 0, 2, 3, 4).reshape(g, 2 * n_levels * 3, 2 * p)
    pad = (-lev.shape[1]) % 8
    lev = jnp.pad(lev, ((0, 0), (0, pad), (0, 0)))
    return lagk, inct.astype(BF16), outt.astype(BF16), lev


def _s5_kernel(u_ref, lagk_ref, inct_ref, outt_ref, lev_ref, h0_ref, d_ref, *rest, nch, n_levels, max_levels, want_fin):
    if want_fin:
        o_ref, fin_ref, w1_ref, wt_ref, w2_ref, a_ref = rest
    else:
        o_ref, w1_ref, wt_ref, w2_ref, a_ref = rest
        fin_ref = None
    tc = S5_CHUNK
    gb = S5_GB
    lanes = u_ref.shape[1]
    ch = lanes // gb
    sw = lev_ref.shape[2]
    m = u_ref.shape[0] // tc
    nbk = m // nch
    b0 = pl.program_id(1) * nbk

    @pl.when(pl.program_id(1) == 0)
    def _():
        r1 = lax.broadcasted_iota(jnp.int32, (lanes, gb * 2 * sw), 0) // ch
        c1 = lax.broadcasted_iota(jnp.int32, (lanes, gb * 2 * sw), 1) // (2 * sw)
        for s in range(tc):
            w1_ref[s * lanes : (s + 1) * lanes, :] = jnp.where(r1 == c1, jnp.tile(inct_ref[0, s], (gb, 1)), 0.0)
        rd = lax.broadcasted_iota(jnp.int32, (lanes, lanes), 0) // ch
        cd = lax.broadcasted_iota(jnp.int32, (lanes, lanes), 1) // ch
        taps = [jnp.where(rd == cd, jnp.tile(lagk_ref[0, li], (gb, 1)), 0.0).astype(BF16) for li in range(2 * tc - 1)]
        for s in range(tc):
            for t in range(tc):
                wt_ref[s * lanes : (s + 1) * lanes, t * lanes : (t + 1) * lanes] = taps[t - s + tc - 1]
        r2 = lax.broadcasted_iota(jnp.int32, (gb * 2 * sw, lanes), 0) // (2 * sw)
        c2 = lax.broadcasted_iota(jnp.int32, (gb * 2 * sw, lanes), 1) // ch
        for t in range(tc):
            w2_ref[:, t * lanes : (t + 1) * lanes] = jnp.where(r2 == c2, jnp.tile(outt_ref[0, t], (gb, 1)), 0.0)

    xs_f32 = [u_ref[pl.ds(s, m, stride=tc), :] for s in range(tc)]
    xcat = jnp.concatenate([x.astype(BF16) for x in xs_f32], axis=1)
    inc = _dot(xcat, w1_ref[...])

    ridx = lax.broadcasted_iota(jnp.int32, (m, sw), 0)
    cidx = ridx % nch
    bidx = ridx // nch
    carry = []
    for g in range(gb):
        for d in range(2):
            fwd = d == 0
            x = inc[:, (2 * g + d) * sw : (2 * g + d + 1) * sw]
            xr = pltpu.roll(x, sw // 2, 1)
            h0 = jnp.zeros((m, sw), F32)
            h0r = jnp.zeros((m, sw), F32)
            for b in range(nbk):
                h0 = jnp.where(bidx == b, h0_ref[g, 2 * d, pl.ds(b0 + b, 1), :], h0)
                h0r = jnp.where(bidx == b, h0_ref[g, 2 * d + 1, pl.ds(b0 + b, 1), :], h0r)

            def const(j, which, d=d, g=g):
                r = (d * max_levels + j) * 3 + which
                return lev_ref[g, r : r + 1, :]

            entry = (cidx == 0) if fwd else (cidx == nch - 1)
            x = x + jnp.where(entry, h0 * const(0, 0) + h0r * const(0, 1), 0.0)
            xr = xr + jnp.where(entry, h0r * const(0, 0) + h0 * const(0, 2), 0.0)
            for j in range(n_levels):
                k = 1 << j
                keep = (cidx >= k) if fwd else (cidx < nch - k)
                shift = k if fwd else m - k
                sx = jnp.where(keep, pltpu.roll(x, shift, 0), 0.0)
                sxr = jnp.where(keep, pltpu.roll(xr, shift, 0), 0.0)
                x, xr = x + const(j, 0) * sx + const(j, 1) * sxr, xr + const(j, 0) * sxr + const(j, 2) * sx
            carry.append(jnp.where(entry, h0, pltpu.roll(x, 1 if fwd else m - 1, 0)).astype(BF16))
            if want_fin:
                a_ref[...] = x
                fin_ref[g, d] = a_ref[pl.ds(nch - 1 if fwd else 0, nbk, stride=nch), :]
    y = _dot(xcat, wt_ref[...]) + _dot(jnp.concatenate(carry, axis=1), w2_ref[...])
    for t in range(tc):
        o_ref[pl.ds(t, m, stride=tc), :] = jax.nn.gelu(y[:, t * lanes : (t + 1) * lanes] + d_ref[...] * xs_f32[t])


def _s5(proj, u_col0, tables, h0, s5_d, row0, nb, l, max_levels, want_fin, prev=None):
    lagk, inct, outt, lev = tables
    t = proj.shape[0]
    tc, gb = S5_CHUNK, S5_GB
    nblk = lagk.shape[0]
    lanes = lagk.shape[3]
    g = nblk * gb
    sw = lev.shape[2]
    nch = l // tc
    n_levels = int(math.log2(nch))
    nbk = nb
    if not want_fin:
        nbk = max(k for k in range(1, nb + 1) if nb % k == 0 and row0 % (k * l) == 0 and (k == 1 or k * nch <= 256))
    m = nbk * nch
    rows = m * tc
    in_specs = [
        pl.BlockSpec((rows, lanes), lambda i, b: (row0 // rows + b, u_col0 // lanes + i)),
        pl.BlockSpec((1,) + lagk.shape[1:], lambda i, b: (i, 0, 0, 0)),
        pl.BlockSpec((1,) + inct.shape[1:], lambda i, b: (i, 0, 0, 0)),
        pl.BlockSpec((1,) + outt.shape[1:], lambda i, b: (i, 0, 0, 0)),
        pl.BlockSpec((gb,) + lev.shape[1:], lambda i, b: (i, 0, 0)),
        pl.BlockSpec((gb, 4, nb, sw), lambda i, b: (i, 0, 0, 0)),
        pl.BlockSpec((1, lanes), lambda i, b: (0, i)),
    ]
    args = [proj, lagk, inct, outt, lev, h0, s5_d.reshape(1, g * (lanes // gb))]
    out_specs = [pl.BlockSpec((rows, lanes), lambda i, b: (row0 // rows + b, i))]
    out_shape = [jax.ShapeDtypeStruct((t, g * (lanes // gb)), F32)]
    if want_fin:
        out_specs.append(pl.BlockSpec((gb, 2, nbk, sw), lambda i, b: (i, 0, b, 0)))
        out_shape.append(jax.ShapeDtypeStruct((g, 2, nb, sw), F32))
    kern = functools.partial(_s5_kernel, nch=nch, n_levels=n_levels, max_levels=max_levels, want_fin=want_fin)
    aliases = {}
    if prev is not None:
        in_specs.append(pl.BlockSpec(memory_space=pl.ANY))
        args.append(prev)
        aliases = {7: 0}
        kern = _drop_arg(kern, 7)
    out = pl.pallas_call(
        kern,
        grid=(nblk, nb // nbk),
        in_specs=in_specs,
        out_specs=out_specs,
        out_shape=out_shape,
        scratch_shapes=[
            pltpu.VMEM((tc * lanes, gb * 2 * sw), BF16),
            pltpu.VMEM((tc * lanes, tc * lanes), BF16),
            pltpu.VMEM((gb * 2 * sw, tc * lanes), BF16),
            pltpu.VMEM((m, sw), F32),
        ],
        input_output_aliases=aliases,
        compiler_params=_cparams("parallel", "arbitrary"),
        name="s5",
    )(*args)
    return (out[0], out[1]) if want_fin else (out[0], None)


def _dft_cos_sin(num, den, scale):
    ang = (2.0 * math.pi / den) * (num % den).astype(np.float64)
    return jnp.asarray(np.cos(ang) * scale, dtype=BF16), jnp.asarray(np.sin(ang) * scale, dtype=BF16)


def _fnet_pos_kernel(c_ref, s_ref, uv_ref, o_ref, *, gd):
    uv = uv_ref[...]
    o_ref[...] = (_dot(c_ref[...], uv[:, :gd]) - _dot(s_ref[...], uv[:, gd:])).astype(o_ref.dtype)


def _fnet_pos(cp, sp, uv, row0, nb, l, gd, prev=None):
    t = uv.shape[0]
    ng = uv.shape[1] // (2 * gd)
    const = pl.BlockSpec((l, l), lambda b, g: (0, 0))
    in_specs = [const, const, pl.BlockSpec((l, 2 * gd), lambda b, g: (row0 // l + b, g))]
    args = [cp, sp, uv]
    kern = functools.partial(_fnet_pos_kernel, gd=gd)
    aliases = {}
    if prev is not None:
        in_specs.append(pl.BlockSpec(memory_space=pl.ANY))
        args.append(prev)
        aliases = {3: 0}
        kern = _drop_arg(kern, 3)
    return pl.pallas_call(
        kern,
        grid=(nb, ng),
        in_specs=in_specs,
        out_specs=pl.BlockSpec((l, gd), lambda b, g: (row0 // l + b, g)),
        out_shape=jax.ShapeDtypeStruct((t, ng * gd), BF16),
        input_output_aliases=aliases,
        compiler_params=_cparams("parallel", "parallel"),
        name="fnet_pos",
    )(*args)


def _fourier_mix(xn, tp, lp, ls):
    t, d = xn.shape
    gd = d // FNET_GROUPS
    kk = np.arange(gd)
    cc, sc = _dft_cos_sin(kk[:, None] * kk[None, :], gd, gd**-0.5)
    cs = jnp.concatenate([cc, sc], axis=1)
    tm = _tile(t, 1024)
    uv = pl.pallas_call(
        functools.partial(_mm_kernel, n_pairs=1, epilogue=_plain),
        grid=(t // tm, FNET_GROUPS),
        in_specs=[pl.BlockSpec((tm, gd), lambda i, g: (i, g)), pl.BlockSpec((gd, 2 * gd), lambda i, g: (0, 0))],
        out_specs=pl.BlockSpec((tm, 2 * gd), lambda i, g: (i, g)),
        out_shape=jax.ShapeDtypeStruct((t, 2 * d), BF16),
        compiler_params=_cparams("parallel", "parallel"),
        name="fnet_chan",
    )(xn, cs)
    pp = np.arange(lp)
    cp_p, sp_p = _dft_cos_sin(pp[:, None] * pp[None, :], lp, lp**-0.5)
    ps = np.arange(ls)
    r, c = ps // GRID_W, ps % GRID_W
    rows = ls // GRID_W
    lcm = rows * GRID_W // math.gcd(rows, GRID_W)
    num = (lcm // rows) * (r[:, None] * r[None, :]) + (lcm // GRID_W) * (c[:, None] * c[None, :])
    cp_s, sp_s = _dft_cos_sin(num, lcm, ls**-0.5)
    z = _fnet_pos(cp_p, sp_p, uv, 0, tp // lp, lp, gd)
    return _fnet_pos(cp_s, sp_s, uv, tp, (t - tp) // ls, ls, gd, prev=z)


def _router_kernel(x_ref, w_ref, b_ref, o_ref, *, ng, npg):
    h = w_ref.shape[0] // 2
    x_lo, x_hi = _unpack_halves(_load_slabs(x_ref, (), o_ref.shape[0], h // LANES))
    logits = _dot(x_lo.astype(BF16), w_ref[:h, :]) + _dot(x_hi.astype(BF16), w_ref[h:, :]) + b_ref[...]
    lane = lax.broadcasted_iota(jnp.int32, logits.shape, 1).astype(F32)
    big = 1e9

    def first_max(vals):
        m = jnp.max(vals, axis=-1, keepdims=True)
        return m, jnp.min(jnp.where(vals == m, lane, big), axis=-1, keepdims=True)

    gl = jnp.where(lane < ng, logits, NEG)
    gmax, gidx = first_max(gl)
    gprob = 1.0 / jnp.sum(jnp.exp(gl - gmax), axis=-1, keepdims=True)
    lo = ng + gidx * npg
    el = jnp.where((lane >= lo) & (lane < lo + npg), logits, NEG)
    m1, i1 = first_max(el)
    m2, i2 = first_max(jnp.where(lane == i1, NEG, el))
    e = jnp.exp(m2 - m1)
    w1 = gprob / (1.0 + e)
    out = jnp.where(lane == 0, i1 - ng, jnp.where(lane == 1, i2 - ng, jnp.where(lane == 2, w1, jnp.where(lane == 3, w1 * e, 0.0))))
    o_ref[...] = out


def _router(xn, wr_g, br_g, wr_e, br_e):
    d = wr_g.shape[0]
    t = xn.shape[0] // (d // 2 // LANES)
    ng, ne = wr_g.shape[1], wr_e.shape[1]
    w = jnp.zeros((d, 128), F32).at[:, :ng].set(wr_g).at[:, ng : ng + ne].set(wr_e).astype(BF16)
    b = jnp.zeros((1, 128), F32).at[0, :ng].set(br_g).at[0, ng : ng + ne].set(br_e)
    tm = _tile(t, 512)
    return pl.pallas_call(
        functools.partial(_router_kernel, ng=ng, npg=ne // ng),
        grid=(t // tm,),
        in_specs=[pl.BlockSpec((tm * (d // 2 // LANES), LANES), lambda i: (i, 0)), pl.BlockSpec((d, 128), lambda i: (0, 0)), pl.BlockSpec((1, 128), lambda i: (0, 0))],
        out_specs=pl.BlockSpec((tm, 128), lambda i: (i, 0)),
        out_shape=jax.ShapeDtypeStruct((t, 128), F32),
        compiler_params=_cparams("parallel"),
        name="router",
    )(xn, w, b)


def _moe_plan(rout, n_exp, tm, nt):
    t = rout.shape[0]
    flat = rout[:, :2].astype(jnp.int32).T.reshape(-1)
    oh = (flat[:, None] == jnp.arange(n_exp)[None, :]).astype(jnp.int32)
    blk = _tile(2 * t, 128)
    c1 = jnp.cumsum(oh.reshape(-1, blk, n_exp), axis=1)
    tot = c1[:, -1]
    cs = (c1 + (jnp.cumsum(tot, axis=0) - tot)[:, None]).reshape(2 * t, n_exp)
    rank = jnp.sum(oh * cs, axis=1) - 1
    counts = cs[-1]
    tiles = (counts + tm - 1) // tm
    tile_end = jnp.cumsum(tiles)
    pos = (tile_end - tiles)[flat] * tm + rank
    row_token = jnp.zeros((nt * tm,), jnp.int32).at[pos].set(jnp.tile(jnp.arange(t, dtype=jnp.int32), 2))
    tile_ids = jnp.arange(nt)
    tile_expert = jnp.minimum(jnp.searchsorted(tile_end, tile_ids, side="right"), n_exp - 1).astype(jnp.int32)
    tile_valid = (tile_ids < tile_end[-1]).astype(jnp.int32)
    prev_expert = jnp.concatenate([jnp.full((1,), -1, jnp.int32), tile_expert[:-1]])
    tile_first = (tile_valid * (tile_expert != prev_expert)).astype(jnp.int32)
    owner = jnp.where(tiles > 0, jnp.arange(n_exp), n_exp)
    nxt = jnp.concatenate([lax.cummin(owner[::-1])[::-1][1:], jnp.full((1,), n_exp)])
    tile_next = jnp.where(nxt < n_exp, nxt, -1)[tile_expert].astype(jnp.int32)
    return row_token, pos.astype(jnp.int32), tile_expert, tile_valid, tile_first, tile_next


def _moe_kernel(te_ref, tv_ref, tf_ref, tn_ref, rt_ref, x_hbm, wg_hbm, wu_hbm, wd_hbm, y_hbm,
                xbuf, stg_g, stg_u, stg_d, wb_g, wb_u, wb_d, obuf, sem_x, sem_w, sem_o, *, tm, layer):
    i = pl.program_id(0)
    nt = pl.num_programs(0)
    slot = i % 2
    xslot = i % (MOE_GATHER_AHEAD + 1)
    d, f = stg_g.shape
    sl = d // 2 // LANES

    def issue_rows(tile, s, r0, r1):
        def body(r, carry):
            src = pl.multiple_of(rt_ref[tile * tm + r] * sl, sl)
            dst = pl.multiple_of(r * sl, sl)
            pltpu.make_async_copy(x_hbm.at[pl.ds(src, sl)], xbuf.at[s, pl.ds(dst, sl)], sem_x.at[s]).start()
            return carry

        lax.fori_loop(r0, r1, body, 0, unroll=8)

    def weight_copies(e):
        return (
            pltpu.make_async_copy(wg_hbm.at[layer, e], stg_g, sem_w.at[0]),
            pltpu.make_async_copy(wu_hbm.at[layer, e], stg_u, sem_w.at[1]),
            pltpu.make_async_copy(wd_hbm.at[layer, e], stg_d, sem_w.at[2]),
        )

    def out_copy(tile):
        rows = tm * sl
        dst = y_hbm.at[pl.ds(pl.multiple_of(tile * rows, rows), rows)]
        return pltpu.make_async_copy(obuf.at[tile % 2], dst, sem_o.at[tile % 2])

    def cast(src, dst, ck):
        def body(k, carry):
            r = pl.multiple_of(k * ck, ck)
            dst[pl.ds(r, ck), :] = src[pl.ds(r, ck), :].astype(BF16)
            return carry

        lax.fori_loop(0, src.shape[0] // ck, body, 0)

    @pl.when(i == 0)
    def _():
        issue_rows(0, 0, 0, tm)
        for cp in weight_copies(te_ref[0]):
            cp.start(priority=1)

        for k in range(1, MOE_GATHER_AHEAD):
            @pl.when(jnp.logical_and(nt > k, tv_ref[jnp.minimum(k, nt - 1)] == 1))
            def _(k=k):
                issue_rows(k, k, 0, tm)

    ahead = jnp.minimum(i + MOE_GATHER_AHEAD, nt - 1)

    @pl.when(jnp.logical_and(i + MOE_GATHER_AHEAD < nt, tv_ref[ahead] == 1))
    def _():
        issue_rows(i + MOE_GATHER_AHEAD, (i + MOE_GATHER_AHEAD) % (MOE_GATHER_AHEAD + 1), 0, tm)

    @pl.when(jnp.logical_and(i >= 2, tv_ref[i] == 1))
    def _():
        out_copy(i - 2).wait()

    @pl.when(jnp.logical_and(tv_ref[i] == 0, jnp.logical_and(i >= 1, tv_ref[jnp.maximum(i - 1, 0)] == 1)))
    def _():
        out_copy(i - 1).wait()

        @pl.when(i >= 2)
        def _():
            out_copy(i - 2).wait()

    @pl.when(tv_ref[i] == 1)
    def _():
        @pl.when(tf_ref[i] == 1)
        def _():
            for cp in weight_copies(0):
                cp.wait()
            cast(stg_g, wb_g, min(256, d))
            cast(stg_u, wb_u, min(256, d))
            cast(stg_d, wb_d, min(32, f))

            @pl.when(tn_ref[i] >= 0)
            def _():
                for cp in weight_copies(tn_ref[i]):
                    cp.start(priority=1)

        pltpu.make_async_copy(x_hbm.at[pl.ds(0, tm * sl)], xbuf.at[xslot], sem_x.at[xslot]).wait()
        x_lo, x_hi = (v.astype(BF16) for v in _unpack_halves(_load_slabs(xbuf, (xslot,), tm, sl)))
        dh = d // 2
        hg = _dot(x_lo, wb_g[:dh, :]) + _dot(x_hi, wb_g[dh:, :])
        hu = _dot(x_lo, wb_u[:dh, :]) + _dot(x_hi, wb_u[dh:, :])
        h = (_silu(hg) * hu).astype(BF16)
        _store_slabs(obuf, (slot,), _pack_pair(_dot(h, wb_d[:, :dh]), _dot(h, wb_d[:, dh:])))
        out_copy(i).start()

        @pl.when(i == nt - 1)
        def _():

            @pl.when(i >= 1)
            def _():
                out_copy(i - 1).wait()

            out_copy(i).wait()


def _combine_kernel(pos_ref, y_hbm, x_ref, r_ref, g_ref, o_ref, buf, sem, *, tm, t):
    i = pl.program_id(0)
    n = pl.num_programs(0)
    nbuf = MOE_GATHER_AHEAD + 1
    slot = i % nbuf

    sl = buf.shape[2] // tm

    def issue(tile, s):
        def body(r, carry):
            dst = pl.multiple_of(r * sl, sl)
            for k in range(2):
                src = pl.multiple_of(pos_ref[k * t + tile * tm + r] * sl, sl)
                pltpu.make_async_copy(y_hbm.at[pl.ds(src, sl)], buf.at[s, k, pl.ds(dst, sl)], sem.at[s]).start()
            return carry

        lax.fori_loop(0, tm, body, 0, unroll=4)

    @pl.when(i == 0)
    def _():
        for k in range(MOE_GATHER_AHEAD):
            @pl.when(k < n)
            def _(k=k):
                issue(k, k)

    @pl.when(i + MOE_GATHER_AHEAD < n)
    def _():
        issue(i + MOE_GATHER_AHEAD, (i + MOE_GATHER_AHEAD) % nbuf)

    for k in range(2):
        pltpu.make_async_copy(y_hbm.at[pl.ds(0, tm * sl)], buf.at[slot, k], sem.at[slot]).wait()
    w = r_ref[...]
    w1, w2 = w[:, 2:3], w[:, 3:4]
    a_lo, a_hi = _unpack_halves(_load_slabs(buf, (slot, 0), tm, sl))
    b_lo, b_hi = _unpack_halves(_load_slabs(buf, (slot, 1), tm, sl))
    dh = a_lo.shape[1]
    o_ref[:, :dh] = x_ref[:, :dh] + g_ref[0, :, :dh] * (w1 * a_lo + w2 * b_lo)
    o_ref[:, dh:] = x_ref[:, dh:] + g_ref[0, :, dh:] * (w1 * a_hi + w2 * b_hi)


def _moe(xn, x, gate, params, layer, tp, ls):
    wr_g, br_g, wr_e, br_e, w_gate, w_up, w_down = params
    t, d = x.shape
    n_exp, _, f = w_gate.shape[1:]
    rout = _router(xn, wr_g, br_g, wr_e, br_e)
    sl = d // 2 // LANES
    tm = 256 if t >= 4096 else 64
    nt = (2 * t + tm - 1) // tm + n_exp
    row_token, pos, tile_expert, tile_valid, tile_first, tile_next = _moe_plan(rout, n_exp, tm, nt)
    hbm = pl.BlockSpec(memory_space=pl.ANY)
    y = pl.pallas_call(
        functools.partial(_moe_kernel, tm=tm, layer=layer),
        grid_spec=pltpu.PrefetchScalarGridSpec(
            num_scalar_prefetch=5,
            grid=(nt,),
            in_specs=[hbm, hbm, hbm, hbm],
            out_specs=hbm,
            scratch_shapes=[
                pltpu.VMEM((MOE_GATHER_AHEAD + 1, tm * sl, LANES), jnp.uint32),
                pltpu.VMEM((d, f), F32),
                pltpu.VMEM((d, f), F32),
                pltpu.VMEM((f, d), F32),
                pltpu.VMEM((d, f), BF16),
                pltpu.VMEM((d, f), BF16),
                pltpu.VMEM((f, d), BF16),
                pltpu.VMEM((2, tm * sl, LANES), jnp.uint32),
                pltpu.SemaphoreType.DMA((MOE_GATHER_AHEAD + 1,)),
                pltpu.SemaphoreType.DMA((3,)),
                pltpu.SemaphoreType.DMA((2,)),
            ],
        ),
        out_shape=jax.ShapeDtypeStruct((nt * tm * sl, LANES), jnp.uint32),
        compiler_params=_cparams("arbitrary"),
        name="moe_experts",
    )(tile_expert, tile_valid, tile_first, tile_next, row_token, xn, w_gate, w_up, w_down)

    tc = _tile(math.gcd(tp, ls), 128)
    return pl.pallas_call(
        functools.partial(_combine_kernel, tm=tc, t=t),
        grid_spec=pltpu.PrefetchScalarGridSpec(
            num_scalar_prefetch=1,
            grid=(t // tc,),
            in_specs=[
                pl.BlockSpec(memory_space=pl.ANY),
                pl.BlockSpec((tc, d), lambda i, p: (i, 0)),
                pl.BlockSpec((tc, 128), lambda i, p: (i, 0)),
                pl.BlockSpec((1, 1, d), lambda i, p: (_mod_row(i, tc, tp, ls), 0, 0)),
            ],
            out_specs=pl.BlockSpec((tc, d), lambda i, p: (i, 0)),
            scratch_shapes=[
                pltpu.VMEM((MOE_GATHER_AHEAD + 1, 2, tc * sl, LANES), jnp.uint32),
                pltpu.SemaphoreType.DMA((MOE_GATHER_AHEAD + 1,)),
            ],
        ),
        out_shape=jax.ShapeDtypeStruct((t, d), F32),
        compiler_params=_cparams("arbitrary"),
        name="moe_combine",
    )(pos, y, x, rout, gate)


def kernel(x_prompt, x_sample, state_hgrn, state_s5_re, state_s5_im, c, c_ctx, ada_w, ada_b, norm_mix, norm_ffn, norm_final, ab_w_in, ab_w_out, hgrn_lb_theta, hgrn_gnorm, s5_lam_re, s5_lam_im, s5_b_re, s5_b_im, s5_c_re, s5_c_im, s5_log_step, s5_d, s5_glu_w, s5_glu_b, fnet_w_out, moe_wr_group, moe_br_group, moe_wr_expert, moe_br_expert, moe_w_gate, moe_w_up, moe_w_down):
    bp, lp, d = x_prompt.shape
    bs, ls, _ = x_sample.shape
    tp, ts = bp * lp, bs * ls
    depth = ada_w.shape[0]
    nh, dk, dv = state_hgrn.shape[3:]
    d_a = nh * dk
    n_grp, s5_p = state_s5_re.shape[3:]
    d_b = s5_d.shape[1]
    nmod = 1 + bs

    x = jnp.concatenate([x_prompt.reshape(tp, d), x_sample.reshape(ts, d)], axis=0)
    c8 = jnp.zeros((8, d), F32).at[0].set(c_ctx).at[1:nmod].set(c)
    mod_all = _ada(c8, ada_w, ada_b)
    lb_all = jnp.cumsum(jax.nn.softmax(hgrn_lb_theta.astype(F32), axis=1), axis=1)
    max_levels = int(math.log2(max(lp, ls) // S5_CHUNK))

    new_h, new_re, new_im = [], [], []
    i_ab = i_c = 0
    for l in range(depth):
        mod = mod_all[l, :nmod].reshape(nmod, 6, 1, d).transpose(1, 0, 2, 3)
        xn = _norm_mod(x, norm_mix[l], mod[0], mod[1], tp, ls, False)
        if l % 2 == 0:
            tm = _tile(math.gcd(tp, ls), 1024)
            n_in = ab_w_in.shape[2]
            proj = _matmul([(xn, ab_w_in[i_ab].astype(BF16))], n_in, F32, tm, _tile(n_in, 1024, 128), name="proj_in")
            tables = _s5_tables(s5_lam_re[i_ab], s5_lam_im[i_ab], s5_b_re[i_ab], s5_b_im[i_ab], s5_c_re[i_ab], s5_c_im[i_ab], s5_log_step[i_ab], max_levels)
            o_dirs, yg, h_fin, x_fin = [None, None], None, None, None
            for row0, nb, ln, h0, x0 in (
                (0, bp, lp, jnp.zeros((bp, 2, nh, dk, dv), F32), jnp.zeros((bp, 2, n_grp, 2 * s5_p), F32)),
                (tp, bs, ls, state_hgrn[:, i_ab].astype(F32), jnp.concatenate([state_s5_re[:, i_ab], state_s5_im[:, i_ab]], axis=-1).astype(F32)),
            ):
                fins = []
                for dr in range(2):
                    o_dirs[dr], s_fin = _hgrn(proj, lb_all[dr, i_ab], h0[:, dr], row0, nb, ln, d_a, 1 + dr, dr == 1, prev=o_dirs[dr])
                    fins.append(s_fin)
                x0s = jnp.concatenate([x0[..., s5_p:], x0[..., :s5_p]], axis=-1)
                h0_s5 = jnp.stack([x0[:, 0], x0s[:, 0], x0[:, 1], x0s[:, 1]], axis=0).transpose(2, 0, 1, 3)
                yg, fin_s5 = _s5(proj, 5 * d_a, tables, h0_s5, s5_d[i_ab], row0, nb, ln, max_levels, row0 == 0, prev=yg)
                if row0 == 0:
                    h_fin = jnp.stack(fins, axis=1)
                    x_fin = fin_s5.transpose(2, 1, 0, 3)
            o_a = _hgrn_post(o_dirs[0], o_dirs[1], proj, hgrn_gnorm[i_ab], nh, dv)
            tn = _tile(d_b, 1024, 128)
            y_glu = _matmul(
                [(yg, s5_glu_w[i_ab].astype(BF16))], d_b, BF16, tm, tn, _glu, (yg, s5_glu_b[i_ab].reshape(1, d_b)),
                (pl.BlockSpec((tm, tn), lambda i, j: (i, j)), pl.BlockSpec((1, tn), lambda i, j: (0, j))), "s5_glu",
            )
            w_out = ab_w_out[i_ab].astype(BF16)
            x = _matmul_residual([(o_a, w_out[:d_a]), (y_glu, w_out[d_a:])], x, mod[2], tp, ls, "mix_out")
            new_h.append(h_fin)
            new_re.append(x_fin[..., :s5_p])
            new_im.append(x_fin[..., s5_p:])
            i_ab += 1
        else:
            z = _fourier_mix(xn, tp, lp, ls)
            x = _matmul_residual([(z, fnet_w_out[i_c].astype(BF16))], x, mod[2], tp, ls, "fnet_out")
            i_c += 1
        xn = _norm_mod(x, norm_ffn[l], mod[3], mod[4], tp, ls, True)
        moe_params = (moe_wr_group[l], moe_br_group[l], moe_wr_expert[l], moe_br_expert[l], moe_w_gate, moe_w_up, moe_w_down)
        x = _moe(xn, x, mod[5], moe_params, l, tp, ls)
    return (
        _rmsnorm(x, norm_final, 0, tp).reshape(bp, lp, d),
        _rmsnorm(x, norm_final, tp, ts).reshape(bs, ls, d),
        jnp.stack(new_h, axis=1),
        jnp.stack(new_re, axis=1),
        jnp.stack(new_im, axis=1),
    )
```

```python
import functools
import math

import jax
import jax.numpy as jnp
import numpy as np
from jax import lax
from jax.experimental import pallas as pl
from jax.experimental.pallas import tpu as pltpu

F32 = jnp.float32
BF16 = jnp.bfloat16
EPS = 1e-6
NEG = -1e30
V7X_VMEM_LIMIT_BYTES = 56 * 1024 * 1024
FNET_GROUPS = 8
GRID_W = 64
HGRN_CHUNK = 128
HGRN_HEADS_PER_STEP = 8
S5_CHUNK = 16
S5_GB = 8
MOE_GATHER_AHEAD = 3


def _cparams(*sem):
    return pltpu.CompilerParams(dimension_semantics=sem, vmem_limit_bytes=V7X_VMEM_LIMIT_BYTES)


def _tile(n, pref, mult=8):
    if n <= pref:
        return n
    for t in range(pref - pref % mult, 0, -mult):
        if n % t == 0:
            return t
    raise ValueError(f"no tile for {n}")


def _drop_arg(body, index):
    def wrapped(*refs):
        return body(*refs[:index], *refs[index + 1 :])

    return wrapped


def _dot(a, b):
    return jnp.dot(a, b, preferred_element_type=F32)


def _dot_nt(a, b):
    return lax.dot_general(a, b, (((1,), (1,)), ((), ())), preferred_element_type=F32)


def _dot_tn(a, b):
    return lax.dot_general(a, b, (((0,), (0,)), ((), ())), preferred_element_type=F32)


def _silu(x):
    return x * jax.nn.sigmoid(x)


def _ada_kernel(c_ref, w_ref, b_ref, o_ref):
    a = _silu(c_ref[...]).astype(BF16)
    o_ref[0] = _dot(a, w_ref[0].astype(BF16)) + b_ref[0]


def _ada(c8, ada_w, ada_b):
    depth, d, n = ada_w.shape
    tn = _tile(n, 1024, 128)
    return pl.pallas_call(
        _ada_kernel,
        grid=(depth, n // tn),
        in_specs=[
            pl.BlockSpec((8, d), lambda l, j: (0, 0)),
            pl.BlockSpec((1, d, tn), lambda l, j: (l, 0, j)),
            pl.BlockSpec((1, 1, tn), lambda l, j: (l, 0, j)),
        ],
        out_specs=pl.BlockSpec((1, 8, tn), lambda l, j: (l, 0, j)),
        out_shape=jax.ShapeDtypeStruct((depth, 8, n), F32),
        compiler_params=_cparams("parallel", "parallel"),
        name="ada",
    )(c8, ada_w, ada_b.reshape(depth, 1, n))


def _mod_row(i, tm, tp, ls):
    r0 = i * tm
    return jnp.where(r0 < tp, 0, 1 + (r0 - tp) // ls)


def _pack_halves(y):
    h = y.shape[1] // 2
    return _pack_pair(y[:, :h], y[:, h:])


def _pack_pair(lo, hi):
    def bf16_bits(v):
        return lax.bitcast_convert_type(v.astype(BF16).astype(F32), jnp.uint32)

    return (bf16_bits(lo) >> 16) | (bf16_bits(hi) & jnp.uint32(0xFFFF0000))


def _unpack_halves(w):
    lo = lax.bitcast_convert_type(w << 16, F32)
    hi = lax.bitcast_convert_type(w & jnp.uint32(0xFFFF0000), F32)
    return lo, hi


LANES = 128


def _store_slabs(ref, lead, words):
    m, w = words.shape
    s = w // LANES
    for j in range(s):
        ref[(*lead, pl.ds(j, m, stride=s), slice(None))] = words[:, j * LANES : (j + 1) * LANES]


def _load_slabs(ref, lead, m, s):
    return jnp.concatenate([ref[(*lead, pl.ds(j, m, stride=s), slice(None))] for j in range(s)], axis=1)


def _norm_mod_kernel(x_ref, g_ref, sh_ref, sc_ref, o_ref, *, packed):
    x = x_ref[...]
    y = x * lax.rsqrt(jnp.mean(x * x, axis=-1, keepdims=True) + EPS) * g_ref[...]
    y = y * (1.0 + sc_ref[0]) + sh_ref[0]
    if packed:
        _store_slabs(o_ref, (), _pack_halves(y))
    else:
        o_ref[...] = y.astype(o_ref.dtype)


def _norm_mod(x, g, shift, scale, tp, ls, packed):
    t, d = x.shape
    tm = _tile(math.gcd(tp, ls), 256)
    mod_spec = pl.BlockSpec((1, 1, d), lambda i: (_mod_row(i, tm, tp, ls), 0, 0))
    s = d // 2 // LANES
    out_spec = pl.BlockSpec((tm * s, LANES), lambda i: (i, 0)) if packed else pl.BlockSpec((tm, d), lambda i: (i, 0))
    out_shape = jax.ShapeDtypeStruct((t * s, LANES), jnp.uint32) if packed else jax.ShapeDtypeStruct((t, d), BF16)
    return pl.pallas_call(
        functools.partial(_norm_mod_kernel, packed=packed),
        grid=(t // tm,),
        in_specs=[pl.BlockSpec((tm, d), lambda i: (i, 0)), pl.BlockSpec((1, d), lambda i: (0, 0)), mod_spec, mod_spec],
        out_specs=out_spec,
        out_shape=out_shape,
        compiler_params=_cparams("parallel"),
        name="norm_mod",
    )(x, g.reshape(1, d), shift, scale)


def _rmsnorm_kernel(x_ref, g_ref, o_ref):
    x = x_ref[...]
    o_ref[...] = x * lax.rsqrt(jnp.mean(x * x, axis=-1, keepdims=True) + EPS) * g_ref[...]


def _rmsnorm(x, g, row0, n):
    d = x.shape[1]
    tm = _tile(math.gcd(row0, n) if row0 else n, 256)
    return pl.pallas_call(
        _rmsnorm_kernel,
        grid=(n // tm,),
        in_specs=[pl.BlockSpec((tm, d), lambda i: (row0 // tm + i, 0)), pl.BlockSpec((1, d), lambda i: (0, 0))],
        out_specs=pl.BlockSpec((tm, d), lambda i: (i, 0)),
        out_shape=jax.ShapeDtypeStruct((n, d), F32),
        compiler_params=_cparams("parallel"),
        name="final_norm",
    )(x, g.reshape(1, d))


def _mm_kernel(*refs, n_pairs, epilogue):
    o_ref = refs[-1]
    acc = None
    for a_ref, b_ref in zip(refs[:n_pairs], refs[n_pairs : 2 * n_pairs]):
        part = _dot(a_ref[...].astype(BF16), b_ref[...])
        acc = part if acc is None else acc + part
    o_ref[...] = epilogue(acc, *refs[2 * n_pairs : -1]).astype(o_ref.dtype)


def _plain(acc):
    return acc


def _residual(acc, x_ref, g_ref):
    return x_ref[...] + g_ref[0] * acc


def _glu(acc, y_ref, b_ref):
    return y_ref[...] * jax.nn.sigmoid(acc + b_ref[...])


def _matmul(pairs, n, out_dtype, tm, tn, epilogue=_plain, extra=(), extra_specs=(), name="matmul"):
    t = pairs[0][0].shape[0]
    a_specs = [pl.BlockSpec((tm, a.shape[1]), lambda i, j: (i, 0)) for a, _ in pairs]
    b_specs = [pl.BlockSpec((b.shape[0], tn), lambda i, j: (0, j)) for _, b in pairs]
    return pl.pallas_call(
        functools.partial(_mm_kernel, n_pairs=len(pairs), epilogue=epilogue),
        grid=(t // tm, n // tn),
        in_specs=a_specs + b_specs + list(extra_specs),
        out_specs=pl.BlockSpec((tm, tn), lambda i, j: (i, j)),
        out_shape=jax.ShapeDtypeStruct((t, n), out_dtype),
        compiler_params=_cparams("parallel", "parallel"),
        name=name,
    )(*[a for a, _ in pairs], *[b for _, b in pairs], *extra)


def _matmul_residual(pairs, x, gate, tp, ls, name):
    t, d = x.shape
    tm = _tile(math.gcd(tp, ls), 1024)
    tn = _tile(d, 1024, 128)
    specs = [
        pl.BlockSpec((tm, tn), lambda i, j: (i, j)),
        pl.BlockSpec((1, 1, tn), lambda i, j: (_mod_row(i, tm, tp, ls), 0, j)),
    ]
    return _matmul(pairs, d, F32, tm, tn, _residual, (x, gate), specs, name)


def _split3(x):
    hi = x.astype(BF16)
    r1 = x - hi.astype(F32)
    mid = r1.astype(BF16)
    lo = (r1 - mid.astype(F32)).astype(BF16)
    return hi, mid, lo


def _hgrn_kernel(q_ref, f_ref, v_ref, lb_ref, s0_ref, o_ref, sfin_ref, st_ref, *, reverse, hb, c, dk):
    ci = pl.program_id(2)

    @pl.when(ci == 0)
    def _():
        for j in range(hb):
            st_ref[j] = s0_ref[0, j].T

    row = lax.broadcasted_iota(jnp.int32, (c, c), 0)
    col = lax.broadcasted_iota(jnp.int32, (c, c), 1)
    tri = jnp.where((col >= row) if reverse else (col <= row), 1.0, 0.0).astype(BF16)
    before = (col > row) if reverse else (col < row)
    diff = row ^ col
    trow = lax.broadcasted_iota(jnp.int32, (c, dk), 0)
    n_levels = int(math.log2(c))
    level = jnp.where(row == col, n_levels, -1)
    for h in range(n_levels):
        level = jnp.where(before & ((diff >> h) == 1), h, level)

    for j in range(hb):
        sl = slice(j * dk, (j + 1) * dk)
        q = _silu(q_ref[:, sl])
        lb = lb_ref[:, sl]
        f = lb + (1.0 - lb) * jax.nn.sigmoid(f_ref[:, sl])
        k = 1.0 - f
        v = v_ref[:, sl]
        vb = v.astype(BF16)
        hi, mid, lo = _split3(jnp.log(f))
        bcum = _dot(tri, hi) + _dot(tri, mid) + _dot(tri, lo)
        btot = bcum[0:1] if reverse else bcum[c - 1 : c]

        st = st_ref[j]
        o = _dot_nt((q * jnp.exp(bcum)).astype(BF16), st.astype(BF16))
        ke = (k * jnp.exp(btot - bcum)).astype(BF16)
        st_ref[j] = st * jnp.exp(btot) + _dot_tn(vb, ke)

        kb = k.astype(BF16)
        att = jnp.where(level == n_levels, _dot_nt(q.astype(BF16), kb), 0.0)
        f_prev = pltpu.roll(f, 1, 0)
        f_next = pltpu.roll(f, c - 1, 0)
        for h in range(n_levels):
            b = 1 << h
            if b == 1:
                x_l, y_l = (q * f).astype(BF16), kb
            else:
                if b == 2:
                    o4 = trow & 3
                    if reverse:
                        ex = jnp.where(o4 == 0, f * f_next, f)
                        ey = jnp.where(o4 == 3, f_prev, 1.0)
                    else:
                        ex = jnp.where(o4 == 3, f * f_prev, f)
                        ey = jnp.where(o4 == 0, f_next, 1.0)
                else:
                    b3 = bcum.reshape(c // (2 * b), 2 * b, dk)
                    edge = b if reverse else b - 1
                    ex = ey = jnp.exp(-jnp.abs(b3 - b3[:, edge : edge + 1, :])).reshape(c, dk)
                x_l, y_l = (q * ex).astype(BF16), (k * ey).astype(BF16)
            att = jnp.where(level == h, _dot_nt(x_l, y_l), att)
        o_ref[:, sl] = o + _dot(att.astype(BF16), vb)

    @pl.when(ci == pl.num_programs(2) - 1)
    def _():
        for j in range(hb):
            sfin_ref[0, j] = st_ref[j].T


def _hgrn(proj, lb, s0, row0, nb, l, d_a, f_section, reverse, prev=None):
    t = proj.shape[0]
    h, dk, dv = s0.shape[1:]
    c = min(HGRN_CHUNK, l)
    nc = l // c
    hb = next(k for k in (HGRN_HEADS_PER_STEP, 4, 2, 1) if h % k == 0)
    hw = hb * dk
    npb = d_a // hw

    def rows(b, ci):
        return row0 // c + b * nc + ((nc - 1 - ci) if reverse else ci)

    def in_spec(section):
        return pl.BlockSpec((c, hw), lambda b, g, ci: (rows(b, ci), section * npb + g))

    in_specs = [
        in_spec(0),
        in_spec(f_section),
        in_spec(3),
        pl.BlockSpec((1, hw), lambda b, g, ci: (0, g)),
        pl.BlockSpec((1, hb, dk, dv), lambda b, g, ci: (b, g, 0, 0)),
    ]
    args = [proj, proj, proj, lb.reshape(1, d_a), s0]
    kern = functools.partial(_hgrn_kernel, reverse=reverse, hb=hb, c=c, dk=dk)
    aliases = {}
    if prev is not None:
        in_specs.append(pl.BlockSpec(memory_space=pl.ANY))
        args.append(prev)
        aliases = {5: 0}
        kern = _drop_arg(kern, 5)
    return pl.pallas_call(
        kern,
        grid=(nb, npb, nc),
        in_specs=in_specs,
        out_specs=[
            pl.BlockSpec((c, hw), lambda b, g, ci: (rows(b, ci), g)),
            pl.BlockSpec((1, hb, dk, dv), lambda b, g, ci: (b, g, 0, 0)),
        ],
        out_shape=[jax.ShapeDtypeStruct((t, d_a), F32), jax.ShapeDtypeStruct((nb, h, dk, dv), F32)],
        scratch_shapes=[pltpu.VMEM((hb, dv, dk), F32)],
        input_output_aliases=aliases,
        compiler_params=_cparams("parallel", "parallel", "arbitrary"),
        name="hgrn_bwd" if reverse else "hgrn_fwd",
    )(*args)


def _hgrn_post_kernel(of_ref, ob_ref, g_ref, gn_ref, o_ref, *, nh, dv):
    for h in range(nh):
        sl = slice(h * dv, (h + 1) * dv)
        o = of_ref[:, sl] + ob_ref[:, sl]
        o = o * lax.rsqrt(jnp.mean(o * o, axis=-1, keepdims=True) + EPS) * gn_ref[:, sl]
        o_ref[:, sl] = (o * _silu(g_ref[:, sl])).astype(o_ref.dtype)


def _hgrn_post(o_fw, o_bw, proj, gnorm, nh, dv):
    t, d_a = o_fw.shape
    tm = _tile(t, 256)
    spec = pl.BlockSpec((tm, d_a), lambda i: (i, 0))
    return pl.pallas_call(
        functools.partial(_hgrn_post_kernel, nh=nh, dv=dv),
        grid=(t // tm,),
        in_specs=[spec, spec, pl.BlockSpec((tm, d_a), lambda i: (i, 4)), pl.BlockSpec((1, d_a), lambda i: (0, 0))],
        out_specs=spec,
        out_shape=jax.ShapeDtypeStruct((t, d_a), BF16),
        compiler_params=_cparams("parallel"),
        name="hgrn_post",
    )(o_fw, o_bw, proj, gnorm.reshape(1, d_a))


def _s5_tables(lam_re, lam_im, b_re, b_im, c_re, c_im, log_step, n_levels):
    hp = lax.Precision.HIGHEST
    tc = S5_CHUNK
    gb = S5_GB
    g, p, ch = b_re.shape[1:]
    nblk = g // gb
    lam = lax.complex(jnp.minimum(lam_re.astype(F32), -1e-4), lam_im.astype(F32))
    z = lam * jnp.exp(log_step.astype(F32))[..., None]
    b_bar = ((jnp.exp(z) - 1.0) / lam)[..., None] * lax.complex(b_re.astype(F32), b_im.astype(F32))
    cc = lax.complex(c_re.astype(F32), c_im.astype(F32))
    n = jnp.arange(tc + 1, dtype=F32)
    pw = jnp.exp(z[:, :, None, :] * n[None, None, :, None])

    kern = jnp.einsum("dgop,dglp,dgpi->dglio", cc, pw[:, :, :tc], b_bar, precision=hp).real
    lag_g = jnp.concatenate([kern[1][:, :0:-1], (kern[0][:, :1] + kern[1][:, :1]), kern[0][:, 1:]], axis=1)
    lagk = lag_g.reshape(nblk, gb, 2 * tc - 1, ch, ch).transpose(0, 2, 3, 1, 4).reshape(nblk, 2 * tc - 1, ch, gb * ch)

    inc_fw = (pw[0][:, tc - 1 - jnp.arange(tc), :, None] * b_bar[0][:, None]).transpose(0, 1, 3, 2)
    inc_bw = (pw[1][:, jnp.arange(tc), :, None] * b_bar[1][:, None]).transpose(0, 1, 3, 2)
    inc = jnp.concatenate([inc_fw.real, inc_fw.imag, inc_bw.real, inc_bw.imag], axis=-1)
    inct = inc.reshape(nblk, gb, tc, ch, 4 * p).transpose(0, 2, 3, 1, 4).reshape(nblk, tc, ch, gb * 4 * p)

    out_fw = cc[0][:, None] * pw[0][:, 1 + jnp.arange(tc), None, :]
    out_bw = cc[1][:, None] * pw[1][:, tc - jnp.arange(tc), None, :]
    out = jnp.concatenate([out_fw.real, -out_fw.imag, out_bw.real, -out_bw.imag], axis=-1)
    outt = out.reshape(nblk, gb, tc, ch, 4 * p).transpose(0, 2, 4, 1, 3).reshape(nblk, tc, 4 * p, gb * ch)

    steps = (tc * (2 ** jnp.arange(n_levels))).astype(F32)
    lp = jnp.exp(z[:, :, None, :] * steps[None, None, :, None])
    a, b = lp.real, lp.imag
    lev = jnp.stack(
        [jnp.concatenate([a, a], -1), jnp.concatenate([-b, b], -1), jnp.concatenate([b, -b], -1)], axis=3
    )
    lev = lev.transpose(1, 0, 2, 3, 4).reshape(g, 2 * n_levels * 3, 2 * p)
    pad = (-lev.shape[1]) % 8
    lev = jnp.pad(lev, ((0, 0), (0, pad), (0, 0)))
    return lagk, inct.astype(BF16), outt.astype(BF16), lev


def _s5_kernel(u_ref, lagk_ref, inct_ref, outt_ref, lev_ref, h0_ref, d_ref, *rest, nch, n_levels, max_levels, want_fin):
    if want_fin:
        o_ref, fin_ref, w1_ref, wt_ref, w2_ref, a_ref = rest
    else:
        o_ref, w1_ref, wt_ref, w2_ref, a_ref = rest
        fin_ref = None
    tc = S5_CHUNK
    gb = S5_GB
    lanes = u_ref.shape[1]
    ch = lanes // gb
    sw = lev_ref.shape[2]
    m = u_ref.shape[0] // tc
    nbk = m // nch
    b0 = pl.program_id(1) * nbk

    @pl.when(pl.program_id(1) == 0)
    def _():
        r1 = lax.broadcasted_iota(jnp.int32, (lanes, gb * 2 * sw), 0) // ch
        c1 = lax.broadcasted_iota(jnp.int32, (lanes, gb * 2 * sw), 1) // (2 * sw)
        for s in range(tc):
            w1_ref[s * lanes : (s + 1) * lanes, :] = jnp.where(r1 == c1, jnp.tile(inct_ref[0, s], (gb, 1)), 0.0)
        rd = lax.broadcasted_iota(jnp.int32, (lanes, lanes), 0) // ch
        cd = lax.broadcasted_iota(jnp.int32, (lanes, lanes), 1) // ch
        taps = [jnp.where(rd == cd, jnp.tile(lagk_ref[0, li], (gb, 1)), 0.0).astype(BF16) for li in range(2 * tc - 1)]
        for s in range(tc):
            for t in range(tc):
                wt_ref[s * lanes : (s + 1) * lanes, t * lanes : (t + 1) * lanes] = taps[t - s + tc - 1]
        r2 = lax.broadcasted_iota(jnp.int32, (gb * 2 * sw, lanes), 0) // (2 * sw)
        c2 = lax.broadcasted_iota(jnp.int32, (gb * 2 * sw, lanes), 1) // ch
        for t in range(tc):
            w2_ref[:, t * lanes : (t + 1) * lanes] = jnp.where(r2 == c2, jnp.tile(outt_ref[0, t], (gb, 1)), 0.0)

    xs_f32 = [u_ref[pl.ds(s, m, stride=tc), :] for s in range(tc)]
    xcat = jnp.concatenate([x.astype(BF16) for x in xs_f32], axis=1)
    inc = _dot(xcat, w1_ref[...])

    ridx = lax.broadcasted_iota(jnp.int32, (m, sw), 0)
    cidx = ridx % nch
    bidx = ridx // nch
    carry = []
    for g in range(gb):
        for d in range(2):
            fwd = d == 0
            x = inc[:, (2 * g + d) * sw : (2 * g + d + 1) * sw]
            xr = pltpu.roll(x, sw // 2, 1)
            h0 = jnp.zeros((m, sw), F32)
            h0r = jnp.zeros((m, sw), F32)
            for b in range(nbk):
                h0 = jnp.where(bidx == b, h0_ref[g, 2 * d, pl.ds(b0 + b, 1), :], h0)
                h0r = jnp.where(bidx == b, h0_ref[g, 2 * d + 1, pl.ds(b0 + b, 1), :], h0r)

            def const(j, which, d=d, g=g):
                r = (d * max_levels + j) * 3 + which
                return lev_ref[g, r : r + 1, :]

            entry = (cidx == 0) if fwd else (cidx == nch - 1)
            x = x + jnp.where(entry, h0 * const(0, 0) + h0r * const(0, 1), 0.0)
            xr = xr + jnp.where(entry, h0r * const(0, 0) + h0 * const(0, 2), 0.0)
            for j in range(n_levels):
                k = 1 << j
                keep = (cidx >= k) if fwd else (cidx < nch - k)
                shift = k if fwd else m - k
                sx = jnp.where(keep, pltpu.roll(x, shift, 0), 0.0)
                sxr = jnp.where(keep, pltpu.roll(xr, shift, 0), 0.0)
                x, xr = x + const(j, 0) * sx + const(j, 1) * sxr, xr + const(j, 0) * sxr + const(j, 2) * sx
            carry.append(jnp.where(entry, h0, pltpu.roll(x, 1 if fwd else m - 1, 0)).astype(BF16))
            if want_fin:
                a_ref[...] = x
                fin_ref[g, d] = a_ref[pl.ds(nch - 1 if fwd else 0, nbk, stride=nch), :]
    y = _dot(xcat, wt_ref[...]) + _dot(jnp.concatenate(carry, axis=1), w2_ref[...])
    for t in range(tc):
        o_ref[pl.ds(t, m, stride=tc), :] = jax.nn.gelu(y[:, t * lanes : (t + 1) * lanes] + d_ref[...] * xs_f32[t])


def _s5(proj, u_col0, tables, h0, s5_d, row0, nb, l, max_levels, want_fin, prev=None):
    lagk, inct, outt, lev = tables
    t = proj.shape[0]
    tc, gb = S5_CHUNK, S5_GB
    nblk = lagk.shape[0]
    lanes = lagk.shape[3]
    g = nblk * gb
    sw = lev.shape[2]
    nch = l // tc
    n_levels = int(math.log2(nch))
    nbk = nb
    if not want_fin:
        nbk = max(k for k in range(1, nb + 1) if nb % k == 0 and row0 % (k * l) == 0 and (k == 1 or k * nch <= 512))
    m = nbk * nch
    rows = m * tc
    in_specs = [
        pl.BlockSpec((rows, lanes), lambda i, b: (row0 // rows + b, u_col0 // lanes + i)),
        pl.BlockSpec((1,) + lagk.shape[1:], lambda i, b: (i, 0, 0, 0)),
        pl.BlockSpec((1,) + inct.shape[1:], lambda i, b: (i, 0, 0, 0)),
        pl.BlockSpec((1,) + outt.shape[1:], lambda i, b: (i, 0, 0, 0)),
        pl.BlockSpec((gb,) + lev.shape[1:], lambda i, b: (i, 0, 0)),
        pl.BlockSpec((gb, 4, nb, sw), lambda i, b: (i, 0, 0, 0)),
        pl.BlockSpec((1, lanes), lambda i, b: (0, i)),
    ]
    args = [proj, lagk, inct, outt, lev, h0, s5_d.reshape(1, g * (lanes // gb))]
    out_specs = [pl.BlockSpec((rows, lanes), lambda i, b: (row0 // rows + b, i))]
    out_shape = [jax.ShapeDtypeStruct((t, g * (lanes // gb)), F32)]
    if want_fin:
        out_specs.append(pl.BlockSpec((gb, 2, nbk, sw), lambda i, b: (i, 0, b, 0)))
        out_shape.append(jax.ShapeDtypeStruct((g, 2, nb, sw), F32))
    kern = functools.partial(_s5_kernel, nch=nch, n_levels=n_levels, max_levels=max_levels, want_fin=want_fin)
    aliases = {}
    if prev is not None:
        in_specs.append(pl.BlockSpec(memory_space=pl.ANY))
        args.append(prev)
        aliases = {7: 0}
        kern = _drop_arg(kern, 7)
    out = pl.pallas_call(
        kern,
        grid=(nblk, nb // nbk),
        in_specs=in_specs,
        out_specs=out_specs,
        out_shape=out_shape,
        scratch_shapes=[
            pltpu.VMEM((tc * lanes, gb * 2 * sw), BF16),
            pltpu.VMEM((tc * lanes, tc * lanes), BF16),
            pltpu.VMEM((gb * 2 * sw, tc * lanes), BF16),
            pltpu.VMEM((m, sw), F32),
        ],
        input_output_aliases=aliases,
        compiler_params=_cparams("parallel", "arbitrary"),
        name="s5",
    )(*args)
    return (out[0], out[1]) if want_fin else (out[0], None)


def _dft_cos_sin(num, den, scale):
    ang = (2.0 * math.pi / den) * (num % den).astype(np.float64)
    return jnp.asarray(np.cos(ang) * scale, dtype=BF16), jnp.asarray(np.sin(ang) * scale, dtype=BF16)


def _fnet_pos_kernel(c_ref, s_ref, uv_ref, o_ref, *, gd):
    uv = uv_ref[...]
    o_ref[...] = (_dot(c_ref[...], uv[:, :gd]) - _dot(s_ref[...], uv[:, gd:])).astype(o_ref.dtype)


def _fnet_pos(cp, sp, uv, row0, nb, l, gd, prev=None):
    t = uv.shape[0]
    ng = uv.shape[1] // (2 * gd)
    const = pl.BlockSpec((l, l), lambda b, g: (0, 0))
    in_specs = [const, const, pl.BlockSpec((l, 2 * gd), lambda b, g: (row0 // l + b, g))]
    args = [cp, sp, uv]
    kern = functools.partial(_fnet_pos_kernel, gd=gd)
    aliases = {}
    if prev is not None:
        in_specs.append(pl.BlockSpec(memory_space=pl.ANY))
        args.append(prev)
        aliases = {3: 0}
        kern = _drop_arg(kern, 3)
    return pl.pallas_call(
        kern,
        grid=(nb, ng),
        in_specs=in_specs,
        out_specs=pl.BlockSpec((l, gd), lambda b, g: (row0 // l + b, g)),
        out_shape=jax.ShapeDtypeStruct((t, ng * gd), BF16),
        input_output_aliases=aliases,
        compiler_params=_cparams("parallel", "parallel"),
        name="fnet_pos",
    )(*args)


def _fourier_mix(xn, tp, lp, ls):
    t, d = xn.shape
    gd = d // FNET_GROUPS
    kk = np.arange(gd)
    cc, sc = _dft_cos_sin(kk[:, None] * kk[None, :], gd, gd**-0.5)
    cs = jnp.concatenate([cc, sc], axis=1)
    tm = _tile(t, 1024)
    uv = pl.pallas_call(
        functools.partial(_mm_kernel, n_pairs=1, epilogue=_plain),
        grid=(t // tm, FNET_GROUPS),
        in_specs=[pl.BlockSpec((tm, gd), lambda i, g: (i, g)), pl.BlockSpec((gd, 2 * gd), lambda i, g: (0, 0))],
        out_specs=pl.BlockSpec((tm, 2 * gd), lambda i, g: (i, g)),
        out_shape=jax.ShapeDtypeStruct((t, 2 * d), BF16),
        compiler_params=_cparams("parallel", "parallel"),
        name="fnet_chan",
    )(xn, cs)
    pp = np.arange(lp)
    cp_p, sp_p = _dft_cos_sin(pp[:, None] * pp[None, :], lp, lp**-0.5)
    ps = np.arange(ls)
    r, c = ps // GRID_W, ps % GRID_W
    rows = ls // GRID_W
    lcm = rows * GRID_W // math.gcd(rows, GRID_W)
    num = (lcm // rows) * (r[:, None] * r[None, :]) + (lcm // GRID_W) * (c[:, None] * c[None, :])
    cp_s, sp_s = _dft_cos_sin(num, lcm, ls**-0.5)
    z = _fnet_pos(cp_p, sp_p, uv, 0, tp // lp, lp, gd)
    return _fnet_pos(cp_s, sp_s, uv, tp, (t - tp) // ls, ls, gd, prev=z)


def _router_kernel(x_ref, w_ref, b_ref, o_ref, *, ng, npg):
    h = w_ref.shape[0] // 2
    x_lo, x_hi = _unpack_halves(_load_slabs(x_ref, (), o_ref.shape[0], h // LANES))
    logits = _dot(x_lo.astype(BF16), w_ref[:h, :]) + _dot(x_hi.astype(BF16), w_ref[h:, :]) + b_ref[...]
    lane = lax.broadcasted_iota(jnp.int32, logits.shape, 1).astype(F32)
    big = 1e9

    def first_max(vals):
        m = jnp.max(vals, axis=-1, keepdims=True)
        return m, jnp.min(jnp.where(vals == m, lane, big), axis=-1, keepdims=True)

    gl = jnp.where(lane < ng, logits, NEG)
    gmax, gidx = first_max(gl)
    gprob = 1.0 / jnp.sum(jnp.exp(gl - gmax), axis=-1, keepdims=True)
    lo = ng + gidx * npg
    el = jnp.where((lane >= lo) & (lane < lo + npg), logits, NEG)
    m1, i1 = first_max(el)
    m2, i2 = first_max(jnp.where(lane == i1, NEG, el))
    e = jnp.exp(m2 - m1)
    w1 = gprob / (1.0 + e)
    out = jnp.where(lane == 0, i1 - ng, jnp.where(lane == 1, i2 - ng, jnp.where(lane == 2, w1, jnp.where(lane == 3, w1 * e, 0.0))))
    o_ref[...] = out


def _router(xn, wr_g, br_g, wr_e, br_e):
    d = wr_g.shape[0]
    t = xn.shape[0] // (d // 2 // LANES)
    ng, ne = wr_g.shape[1], wr_e.shape[1]
    w = jnp.zeros((d, 128), F32).at[:, :ng].set(wr_g).at[:, ng : ng + ne].set(wr_e).astype(BF16)
    b = jnp.zeros((1, 128), F32).at[0, :ng].set(br_g).at[0, ng : ng + ne].set(br_e)
    tm = _tile(t, 512)
    return pl.pallas_call(
        functools.partial(_router_kernel, ng=ng, npg=ne // ng),
        grid=(t // tm,),
        in_specs=[pl.BlockSpec((tm * (d // 2 // LANES), LANES), lambda i: (i, 0)), pl.BlockSpec((d, 128), lambda i: (0, 0)), pl.BlockSpec((1, 128), lambda i: (0, 0))],
        out_specs=pl.BlockSpec((tm, 128), lambda i: (i, 0)),
        out_shape=jax.ShapeDtypeStruct((t, 128), F32),
        compiler_params=_cparams("parallel"),
        name="router",
    )(xn, w, b)


def _moe_plan(rout, n_exp, tm, nt):
    t = rout.shape[0]
    flat = rout[:, :2].astype(jnp.int32).T.reshape(-1)
    oh = (flat[:, None] == jnp.arange(n_exp)[None, :]).astype(jnp.int32)
    blk = _tile(2 * t, 128)
    c1 = jnp.cumsum(oh.reshape(-1, blk, n_exp), axis=1)
    tot = c1[:, -1]
    cs = (c1 + (jnp.cumsum(tot, axis=0) - tot)[:, None]).reshape(2 * t, n_exp)
    rank = jnp.sum(oh * cs, axis=1) - 1
    counts = cs[-1]
    tiles = (counts + tm - 1) // tm
    tile_end = jnp.cumsum(tiles)
    pos = (tile_end - tiles)[flat] * tm + rank
    row_token = jnp.zeros((nt * tm,), jnp.int32).at[pos].set(jnp.tile(jnp.arange(t, dtype=jnp.int32), 2))
    tile_ids = jnp.arange(nt)
    tile_expert = jnp.minimum(jnp.searchsorted(tile_end, tile_ids, side="right"), n_exp - 1).astype(jnp.int32)
    tile_valid = (tile_ids < tile_end[-1]).astype(jnp.int32)
    prev_expert = jnp.concatenate([jnp.full((1,), -1, jnp.int32), tile_expert[:-1]])
    tile_first = (tile_valid * (tile_expert != prev_expert)).astype(jnp.int32)
    owner = jnp.where(tiles > 0, jnp.arange(n_exp), n_exp)
    nxt = jnp.concatenate([lax.cummin(owner[::-1])[::-1][1:], jnp.full((1,), n_exp)])
    tile_next = jnp.where(nxt < n_exp, nxt, -1)[tile_expert].astype(jnp.int32)
    return row_token, pos.astype(jnp.int32), tile_expert, tile_valid, tile_first, tile_next


def _moe_kernel(te_ref, tv_ref, tf_ref, tn_ref, rt_ref, x_hbm, wg_hbm, wu_hbm, wd_hbm, y_hbm,
                xbuf, stg_g, stg_u, stg_d, wb_g, wb_u, wb_d, obuf, sem_x, sem_w, sem_o, *, tm, layer):
    i = pl.program_id(0)
    nt = pl.num_programs(0)
    slot = i % 2
    xslot = i % (MOE_GATHER_AHEAD + 1)
    d, f = stg_g.shape
    sl = d // 2 // LANES

    def issue_rows(tile, s, r0, r1):
        def body(r, carry):
            src = pl.multiple_of(rt_ref[tile * tm + r] * sl, sl)
            dst = pl.multiple_of(r * sl, sl)
            pltpu.make_async_copy(x_hbm.at[pl.ds(src, sl)], xbuf.at[s, pl.ds(dst, sl)], sem_x.at[s]).start()
            return carry

        lax.fori_loop(r0, r1, body, 0, unroll=8)

    def weight_copies(e):
        return (
            pltpu.make_async_copy(wg_hbm.at[layer, e], stg_g, sem_w.at[0]),
            pltpu.make_async_copy(wu_hbm.at[layer, e], stg_u, sem_w.at[1]),
            pltpu.make_async_copy(wd_hbm.at[layer, e], stg_d, sem_w.at[2]),
        )

    def out_copy(tile):
        rows = tm * sl
        dst = y_hbm.at[pl.ds(pl.multiple_of(tile * rows, rows), rows)]
        return pltpu.make_async_copy(obuf.at[tile % 2], dst, sem_o.at[tile % 2])

    def cast(src, dst, ck):
        def body(k, carry):
            r = pl.multiple_of(k * ck, ck)
            dst[pl.ds(r, ck), :] = src[pl.ds(r, ck), :].astype(BF16)
            return carry

        lax.fori_loop(0, src.shape[0] // ck, body, 0)

    @pl.when(i == 0)
    def _():
        issue_rows(0, 0, 0, tm)
        for cp in weight_copies(te_ref[0]):
            cp.start(priority=1)

        for k in range(1, MOE_GATHER_AHEAD):
            @pl.when(jnp.logical_and(nt > k, tv_ref[jnp.minimum(k, nt - 1)] == 1))
            def _(k=k):
                issue_rows(k, k, 0, tm)

    ahead = jnp.minimum(i + MOE_GATHER_AHEAD, nt - 1)

    @pl.when(jnp.logical_and(i + MOE_GATHER_AHEAD < nt, tv_ref[ahead] == 1))
    def _():
        issue_rows(i + MOE_GATHER_AHEAD, (i + MOE_GATHER_AHEAD) % (MOE_GATHER_AHEAD + 1), 0, tm)

    @pl.when(jnp.logical_and(i >= 2, tv_ref[i] == 1))
    def _():
        out_copy(i - 2).wait()

    @pl.when(jnp.logical_and(tv_ref[i] == 0, jnp.logical_and(i >= 1, tv_ref[jnp.maximum(i - 1, 0)] == 1)))
    def _():
        out_copy(i - 1).wait()

        @pl.when(i >= 2)
        def _():
            out_copy(i - 2).wait()

    @pl.when(tv_ref[i] == 1)
    def _():
        @pl.when(tf_ref[i] == 1)
        def _():
            for cp in weight_copies(0):
                cp.wait()
            cast(stg_g, wb_g, min(256, d))
            cast(stg_u, wb_u, min(256, d))
            cast(stg_d, wb_d, min(32, f))

            @pl.when(tn_ref[i] >= 0)
            def _():
                for cp in weight_copies(tn_ref[i]):
                    cp.start(priority=1)

        pltpu.make_async_copy(x_hbm.at[pl.ds(0, tm * sl)], xbuf.at[xslot], sem_x.at[xslot]).wait()
        x_lo, x_hi = (v.astype(BF16) for v in _unpack_halves(_load_slabs(xbuf, (xslot,), tm, sl)))
        dh = d // 2
        hg = _dot(x_lo, wb_g[:dh, :]) + _dot(x_hi, wb_g[dh:, :])
        hu = _dot(x_lo, wb_u[:dh, :]) + _dot(x_hi, wb_u[dh:, :])
        h = (_silu(hg) * hu).astype(BF16)
        _store_slabs(obuf, (slot,), _pack_pair(_dot(h, wb_d[:, :dh]), _dot(h, wb_d[:, dh:])))
        out_copy(i).start()

        @pl.when(i == nt - 1)
        def _():

            @pl.when(i >= 1)
            def _():
                out_copy(i - 1).wait()

            out_copy(i).wait()


def _combine_kernel(pos_ref, y_hbm, x_ref, r_ref, g_ref, o_ref, buf, sem, *, tm, t):
    i = pl.program_id(0)
    n = pl.num_programs(0)
    nbuf = MOE_GATHER_AHEAD + 1
    slot = i % nbuf

    sl = buf.shape[2] // tm

    def issue(tile, s):
        def body(r, carry):
            dst = pl.multiple_of(r * sl, sl)
            for k in range(2):
                src = pl.multiple_of(pos_ref[k * t + tile * tm + r] * sl, sl)
                pltpu.make_async_copy(y_hbm.at[pl.ds(src, sl)], buf.at[s, k, pl.ds(dst, sl)], sem.at[s]).start()
            return carry

        lax.fori_loop(0, tm, body, 0, unroll=4)

    @pl.when(i == 0)
    def _():
        for k in range(MOE_GATHER_AHEAD):
            @pl.when(k < n)
            def _(k=k):
                issue(k, k)

    @pl.when(i + MOE_GATHER_AHEAD < n)
    def _():
        issue(i + MOE_GATHER_AHEAD, (i + MOE_GATHER_AHEAD) % nbuf)

    for k in range(2):
        pltpu.make_async_copy(y_hbm.at[pl.ds(0, tm * sl)], buf.at[slot, k], sem.at[slot]).wait()
    w = r_ref[...]
    w1, w2 = w[:, 2:3], w[:, 3:4]
    a_lo, a_hi = _unpack_halves(_load_slabs(buf, (slot, 0), tm, sl))
    b_lo, b_hi = _unpack_halves(_load_slabs(buf, (slot, 1), tm, sl))
    dh = a_lo.shape[1]
    o_ref[:, :dh] = x_ref[:, :dh] + g_ref[0, :, :dh] * (w1 * a_lo + w2 * b_lo)
    o_ref[:, dh:] = x_ref[:, dh:] + g_ref[0, :, dh:] * (w1 * a_hi + w2 * b_hi)


def _moe(xn, x, gate, params, layer, tp, ls):
    wr_g, br_g, wr_e, br_e, w_gate, w_up, w_down = params
    t, d = x.shape
    n_exp, _, f = w_gate.shape[1:]
    rout = _router(xn, wr_g, br_g, wr_e, br_e)
    sl = d // 2 // LANES
    tm = 256 if t >= 4096 else 64
    nt = (2 * t + tm - 1) // tm + n_exp
    row_token, pos, tile_expert, tile_valid, tile_first, tile_next = _moe_plan(rout, n_exp, tm, nt)
    hbm = pl.BlockSpec(memory_space=pl.ANY)
    y = pl.pallas_call(
        functools.partial(_moe_kernel, tm=tm, layer=layer),
        grid_spec=pltpu.PrefetchScalarGridSpec(
            num_scalar_prefetch=5,
            grid=(nt,),
            in_specs=[hbm, hbm, hbm, hbm],
            out_specs=hbm,
            scratch_shapes=[
                pltpu.VMEM((MOE_GATHER_AHEAD + 1, tm * sl, LANES), jnp.uint32),
                pltpu.VMEM((d, f), F32),
                pltpu.VMEM((d, f), F32),
                pltpu.VMEM((f, d), F32),
                pltpu.VMEM((d, f), BF16),
                pltpu.VMEM((d, f), BF16),
                pltpu.VMEM((f, d), BF16),
                pltpu.VMEM((2, tm * sl, LANES), jnp.uint32),
                pltpu.SemaphoreType.DMA((MOE_GATHER_AHEAD + 1,)),
                pltpu.SemaphoreType.DMA((3,)),
                pltpu.SemaphoreType.DMA((2,)),
            ],
        ),
        out_shape=jax.ShapeDtypeStruct((nt * tm * sl, LANES), jnp.uint32),
        compiler_params=_cparams("arbitrary"),
        name="moe_experts",
    )(tile_expert, tile_valid, tile_first, tile_next, row_token, xn, w_gate, w_up, w_down)

    tc = _tile(math.gcd(tp, ls), 128)
    return pl.pallas_call(
        functools.partial(_combine_kernel, tm=tc, t=t),
        grid_spec=pltpu.PrefetchScalarGridSpec(
            num_scalar_prefetch=1,
            grid=(t // tc,),
            in_specs=[
                pl.BlockSpec(memory_space=pl.ANY),
                pl.BlockSpec((tc, d), lambda i, p: (i, 0)),
                pl.BlockSpec((tc, 128), lambda i, p: (i, 0)),
                pl.BlockSpec((1, 1, d), lambda i, p: (_mod_row(i, tc, tp, ls), 0, 0)),
            ],
            out_specs=pl.BlockSpec((tc, d), lambda i, p: (i, 0)),
            scratch_shapes=[
                pltpu.VMEM((MOE_GATHER_AHEAD + 1, 2, tc * sl, LANES), jnp.uint32),
                pltpu.SemaphoreType.DMA((MOE_GATHER_AHEAD + 1,)),
            ],
        ),
        out_shape=jax.ShapeDtypeStruct((t, d), F32),
        compiler_params=_cparams("arbitrary"),
        name="moe_combine",
    )(pos, y, x, rout, gate)


def kernel(x_prompt, x_sample, state_hgrn, state_s5_re, state_s5_im, c, c_ctx, ada_w, ada_b, norm_mix, norm_ffn, norm_final, ab_w_in, ab_w_out, hgrn_lb_theta, hgrn_gnorm, s5_lam_re, s5_lam_im, s5_b_re, s5_b_im, s5_c_re, s5_c_im, s5_log_step, s5_d, s5_glu_w, s5_glu_b, fnet_w_out, moe_wr_group, moe_br_group, moe_wr_expert, moe_br_expert, moe_w_gate, moe_w_up, moe_w_down):
    bp, lp, d = x_prompt.shape
    bs, ls, _ = x_sample.shape
    tp, ts = bp * lp, bs * ls
    depth = ada_w.shape[0]
    nh, dk, dv = state_hgrn.shape[3:]
    d_a = nh * dk
    n_grp, s5_p = state_s5_re.shape[3:]
    d_b = s5_d.shape[1]
    nmod = 1 + bs

    x = jnp.concatenate([x_prompt.reshape(tp, d), x_sample.reshape(ts, d)], axis=0)
    c8 = jnp.zeros((8, d), F32).at[0].set(c_ctx).at[1:nmod].set(c)
    mod_all = _ada(c8, ada_w, ada_b)
    lb_all = jnp.cumsum(jax.nn.softmax(hgrn_lb_theta.astype(F32), axis=1), axis=1)
    max_levels = int(math.log2(max(lp, ls) // S5_CHUNK))

    new_h, new_re, new_im = [], [], []
    i_ab = i_c = 0
    for l in range(depth):
        mod = mod_all[l, :nmod].reshape(nmod, 6, 1, d).transpose(1, 0, 2, 3)
        xn = _norm_mod(x, norm_mix[l], mod[0], mod[1], tp, ls, False)
        if l % 2 == 0:
            tm = _tile(math.gcd(tp, ls), 1024)
            n_in = ab_w_in.shape[2]
            proj = _matmul([(xn, ab_w_in[i_ab].astype(BF16))], n_in, F32, tm, _tile(n_in, 1024, 128), name="proj_in")
            tables = _s5_tables(s5_lam_re[i_ab], s5_lam_im[i_ab], s5_b_re[i_ab], s5_b_im[i_ab], s5_c_re[i_ab], s5_c_im[i_ab], s5_log_step[i_ab], max_levels)
            o_dirs, yg, h_fin, x_fin = [None, None], None, None, None
            for row0, nb, ln, h0, x0 in (
                (0, bp, lp, jnp.zeros((bp, 2, nh, dk, dv), F32), jnp.zeros((bp, 2, n_grp, 2 * s5_p), F32)),
                (tp, bs, ls, state_hgrn[:, i_ab].astype(F32), jnp.concatenate([state_s5_re[:, i_ab], state_s5_im[:, i_ab]], axis=-1).astype(F32)),
            ):
                fins = []
                for dr in range(2):
                    o_dirs[dr], s_fin = _hgrn(proj, lb_all[dr, i_ab], h0[:, dr], row0, nb, ln, d_a, 1 + dr, dr == 1, prev=o_dirs[dr])
                    fins.append(s_fin)
                x0s = jnp.concatenate([x0[..., s5_p:], x0[..., :s5_p]], axis=-1)
                h0_s5 = jnp.stack([x0[:, 0], x0s[:, 0], x0[:, 1], x0s[:, 1]], axis=0).transpose(2, 0, 1, 3)
                yg, fin_s5 = _s5(proj, 5 * d_a, tables, h0_s5, s5_d[i_ab], row0, nb, ln, max_levels, row0 == 0, prev=yg)
                if row0 == 0:
                    h_fin = jnp.stack(fins, axis=1)
                    x_fin = fin_s5.transpose(2, 1, 0, 3)
            o_a = _hgrn_post(o_dirs[0], o_dirs[1], proj, hgrn_gnorm[i_ab], nh, dv)
            tn = _tile(d_b, 1024, 128)
            y_glu = _matmul(
                [(yg, s5_glu_w[i_ab].astype(BF16))], d_b, BF16, tm, tn, _glu, (yg, s5_glu_b[i_ab].reshape(1, d_b)),
                (pl.BlockSpec((tm, tn), lambda i, j: (i, j)), pl.BlockSpec((1, tn), lambda i, j: (0, j))), "s5_glu",
            )
            w_out = ab_w_out[i_ab].astype(BF16)
            x = _matmul_residual([(o_a, w_out[:d_a]), (y_glu, w_out[d_a:])], x, mod[2], tp, ls, "mix_out")
            new_h.append(h_fin)
            new_re.append(x_fin[..., :s5_p])
            new_im.append(x_fin[..., s5_p:])
            i_ab += 1
        else:
            z = _fourier_mix(xn, tp, lp, ls)
            x = _matmul_residual([(z, fnet_w_out[i_c].astype(BF16))], x, mod[2], tp, ls, "fnet_out")
            i_c += 1
        xn = _norm_mod(x, norm_ffn[l], mod[3], mod[4], tp, ls, True)
        moe_params = (moe_wr_group[l], moe_br_group[l], moe_wr_expert[l], moe_br_expert[l], moe_w_gate, moe_w_up, moe_w_down)
        x = _moe(xn, x, mod[5], moe_params, l, tp, ls)
    return (
        _rmsnorm(x, norm_final, 0, tp).reshape(bp, lp, d),
        _rmsnorm(x, norm_final, tp, ts).reshape(bs, ls, d),
        jnp.stack(new_h, axis=1),
        jnp.stack(new_re, axis=1),
        jnp.stack(new_im, axis=1),
    )
```
